```python
import math
import jax, jax.numpy as jnp
from jax import lax
import numpy as np

D_MODEL = 1024
BATCH = 8
SEQ = 8192
DEPTH = 4

GRID_W = 64
ROPE_THETA = 10000.0
Q_BLOCK = 128
EPS = 1e-6

GQA_HEADS = 8
GQA_KV_HEADS = 2
GQA_GROUP = GQA_HEADS // GQA_KV_HEADS
GQA_HEAD_DIM = D_MODEL // 16
GQA_Q_W = GQA_HEADS * GQA_HEAD_DIM
GQA_KV_W = GQA_KV_HEADS * GQA_HEAD_DIM

MLA_HEADS = 8
MLA_NOPE_DIM = D_MODEL // 16
MLA_ROPE_DIM = D_MODEL // 32
MLA_V_DIM = D_MODEL // 16
MLA_QK_DIM = MLA_NOPE_DIM + MLA_ROPE_DIM
MLA_Q_RANK = (3 * D_MODEL) // 8
MLA_KV_RANK = D_MODEL // 4
MLA_OUT_W = MLA_HEADS * MLA_V_DIM

D_FF = 4 * D_MODEL

SPLIT_SIZES = (GQA_Q_W, GQA_KV_W, GQA_KV_W, MLA_Q_RANK, MLA_KV_RANK, MLA_ROPE_DIM, 2 * D_MODEL)
IN_W = sum(SPLIT_SIZES)
SPLIT_POINTS = [int(v) for v in np.cumsum(SPLIT_SIZES)[:-1]]

kernel_name = "hybrid_gqa_mla_sandwich_encoder"


def rmsnorm(x, g):
    xf = x.astype(jnp.float32)
    y = xf * lax.rsqrt(jnp.mean(xf * xf, axis=-1, keepdims=True) + EPS)
    return (y * g.astype(jnp.float32)).astype(x.dtype)


def axial_rope_tables(seq, rot_dim):
    rows = seq // GRID_W
    row = jnp.repeat(jnp.arange(rows, dtype=jnp.float32), GRID_W)
    col = jnp.tile(jnp.arange(GRID_W, dtype=jnp.float32), rows)
    half = rot_dim // 2
    inv = ROPE_THETA ** (-jnp.arange(0, half, 2, dtype=jnp.float32) / half)
    ar = row[:, None] * inv[None, :]
    ac = col[:, None] * inv[None, :]
    ang = jnp.concatenate([ar, ar, ac, ac], axis=-1)
    return jnp.cos(ang), jnp.sin(ang)


def apply_axial_rope(x, cos, sin):
    d = x.shape[-1]
    h = d // 2
    q = h // 2
    shape = (cos.shape[0],) + (1,) * (x.ndim - 3) + (d,)
    c = cos.reshape(shape).astype(x.dtype)
    s = sin.reshape(shape).astype(x.dtype)
    xr, xc = x[..., :h], x[..., h:]
    rot = lambda z: jnp.concatenate([-z[..., q:], z[..., :q]], axis=-1)
    x_rot = jnp.concatenate([rot(xr), rot(xc)], axis=-1)
    return x * c + x_rot * s


def blocked_attention(q, k, v, scale):
    b, s, hk, g, dk = q.shape
    dv = v.shape[-1]
    nb = s // Q_BLOCK
    qb = q.reshape(b, nb, Q_BLOCK, hk, g, dk).swapaxes(0, 1)

    def one_block(qblk):
        sc = jnp.einsum('bqhgd,bkhd->bhgqk', qblk, k).astype(jnp.float32) * scale
        p = jax.nn.softmax(sc, axis=-1).astype(v.dtype)
        return jnp.einsum('bhgqk,bkhd->bqhgd', p, v)

    out = lax.map(one_block, qb)
    return out.swapaxes(0, 1).reshape(b, s, hk * g * dv)


def token_mixer(u, w_in, b_gate, q_norm_g, k_norm_g, q_a_norm_g, kv_a_norm_g,
                w_q_up, w_kv_up, w_branch_a, w_branch_b, w_o,
                cos_a, sin_a, cos_b, sin_b):
    b, s, _ = u.shape
    z = u @ w_in
    qa, ka, va, cq, ckv, kr, gl = jnp.split(z, SPLIT_POINTS, axis=-1)

    qa = qa.reshape(b, s, GQA_HEADS, GQA_HEAD_DIM)
    ka = ka.reshape(b, s, GQA_KV_HEADS, GQA_HEAD_DIM)
    va = va.reshape(b, s, GQA_KV_HEADS, GQA_HEAD_DIM)
    qa = apply_axial_rope(rmsnorm(qa, q_norm_g), cos_a, sin_a)
    ka = apply_axial_rope(rmsnorm(ka, k_norm_g), cos_a, sin_a)
    qa = qa.reshape(b, s, GQA_KV_HEADS, GQA_GROUP, GQA_HEAD_DIM)
    ya = blocked_attention(qa, ka, va, 1.0 / math.sqrt(GQA_HEAD_DIM))

    qb = (rmsnorm(cq, q_a_norm_g) @ w_q_up).reshape(b, s, MLA_HEADS, MLA_QK_DIM)
    q_nope, q_rope = qb[..., :MLA_NOPE_DIM], qb[..., MLA_NOPE_DIM:]
    q_rope = apply_axial_rope(q_rope, cos_b, sin_b)
    kvb = (rmsnorm(ckv, kv_a_norm_g) @ w_kv_up).reshape(b, s, MLA_HEADS, MLA_NOPE_DIM + MLA_V_DIM)
    k_nope, vb = kvb[..., :MLA_NOPE_DIM], kvb[..., MLA_NOPE_DIM:]
    k_rope = apply_axial_rope(kr, cos_b, sin_b)
    k_rope = jnp.broadcast_to(k_rope[:, :, None, :], (b, s, MLA_HEADS, MLA_ROPE_DIM))
    qb = jnp.concatenate([q_nope, q_rope], axis=-1)[:, :, :, None, :]
    kb = jnp.concatenate([k_nope, k_rope], axis=-1)
    yb = blocked_attention(qb, kb, vb, 1.0 / math.sqrt(MLA_QK_DIM))

    gates = jax.nn.sigmoid((gl + b_gate).astype(jnp.float32)).astype(u.dtype)
    g_a, g_b = gates[..., :D_MODEL], gates[..., D_MODEL:]
    merged = g_a * (ya @ w_branch_a) + g_b * (yb @ w_branch_b)
    return merged @ w_o


def _fwd_setup_inputs(seed: int = 0) -> dict:
    key = jax.random.key(seed)
    ks = jax.random.split(key, 20)
    f32 = jnp.float32

    def w(k, fan_in, fan_out):
        return jax.random.normal(k, (DEPTH, fan_in, fan_out), f32) * fan_in ** -0.5

    def gain(k, n):
        return 1.0 + 0.05 * jax.random.normal(k, (DEPTH, n), f32)

    return {
        "x": jax.random.normal(ks[0], (BATCH, SEQ, D_MODEL), f32),
        "w_in": w(ks[1], D_MODEL, IN_W),
        "b_gate": 0.1 * jax.random.normal(ks[2], (DEPTH, 2 * D_MODEL), f32),
        "q_norm_g": gain(ks[3], GQA_HEAD_DIM),
        "k_norm_g": gain(ks[4], GQA_HEAD_DIM),
        "q_a_norm_g": gain(ks[5], MLA_Q_RANK),
        "kv_a_norm_g": gain(ks[6], MLA_KV_RANK),
        "w_q_up": w(ks[7], MLA_Q_RANK, MLA_HEADS * MLA_QK_DIM),
        "w_kv_up": w(ks[8], MLA_KV_RANK, MLA_HEADS * (MLA_NOPE_DIM + MLA_V_DIM)),
        "w_branch_a": w(ks[9], GQA_Q_W, D_MODEL),
        "w_branch_b": w(ks[10], MLA_OUT_W, D_MODEL),
        "w_o": w(ks[11], D_MODEL, D_MODEL),
        "w_ffn_up": w(ks[12], D_MODEL, D_FF),
        "w_ffn_down": w(ks[13], D_FF, D_MODEL),
        "pre_mix_g": gain(ks[14], D_MODEL),
        "post_mix_g": gain(ks[15], D_MODEL),
        "pre_ffn_g": gain(ks[16], D_MODEL),
        "post_ffn_g": gain(ks[17], D_MODEL),
    }


def _fwd_reference(x, w_in, b_gate, q_norm_g, k_norm_g, q_a_norm_g, kv_a_norm_g,
              w_q_up, w_kv_up, w_branch_a, w_branch_b, w_o, w_ffn_up, w_ffn_down,
              pre_mix_g, post_mix_g, pre_ffn_g, post_ffn_g):
    seq = x.shape[1]
    cos_a, sin_a = axial_rope_tables(seq, GQA_HEAD_DIM)
    cos_b, sin_b = axial_rope_tables(seq, MLA_ROPE_DIM)
    for l in range(DEPTH):
        u = rmsnorm(x, pre_mix_g[l])
        m = token_mixer(u, w_in[l], b_gate[l], q_norm_g[l], k_norm_g[l],
                        q_a_norm_g[l], kv_a_norm_g[l], w_q_up[l], w_kv_up[l],
                        w_branch_a[l], w_branch_b[l], w_o[l],
                        cos_a, sin_a, cos_b, sin_b)
        x = x + rmsnorm(m, post_mix_g[l])
        h = rmsnorm(x, pre_ffn_g[l]) @ w_ffn_up[l]
        f = jnp.square(jax.nn.relu(h)) @ w_ffn_down[l]
        x = x + rmsnorm(f, post_ffn_g[l])
    return x


import jax as _jax
import jax.numpy as _jnp

TWIN_FORMAT = 'train_step'
FWD_PARAMS = ['x', 'w_in', 'b_gate', 'q_norm_g', 'k_norm_g', 'q_a_norm_g', 'kv_a_norm_g', 'w_q_up', 'w_kv_up', 'w_branch_a', 'w_branch_b', 'w_o', 'w_ffn_up', 'w_ffn_down', 'pre_mix_g', 'post_mix_g', 'pre_ffn_g', 'post_ffn_g']
TWIN_WEIGHTS = ['w_in', 'b_gate', 'q_norm_g', 'k_norm_g', 'q_a_norm_g', 'kv_a_norm_g', 'w_q_up', 'w_kv_up', 'w_branch_a', 'w_branch_b', 'w_o', 'w_ffn_up', 'w_ffn_down', 'pre_mix_g', 'post_mix_g', 'pre_ffn_g', 'post_ffn_g']
TWIN_DIFF_INPUT = 'x'
TWIN_INPUTS = ['x', 'w_in', 'b_gate', 'q_norm_g', 'k_norm_g', 'q_a_norm_g', 'kv_a_norm_g', 'w_q_up', 'w_kv_up', 'w_branch_a', 'w_branch_b', 'w_o', 'w_ffn_up', 'w_ffn_down', 'pre_mix_g', 'post_mix_g', 'pre_ffn_g', 'post_ffn_g', 'loss_target', 'm_w_in', 'm_b_gate', 'm_q_norm_g', 'm_k_norm_g', 'm_q_a_norm_g', 'm_kv_a_norm_g', 'm_w_q_up', 'm_w_kv_up', 'm_w_branch_a', 'm_w_branch_b', 'm_w_o', 'm_w_ffn_up', 'm_w_ffn_down', 'm_pre_mix_g', 'm_post_mix_g', 'm_pre_ffn_g', 'm_post_ffn_g', 'v_w_in', 'v_b_gate', 'v_q_norm_g', 'v_k_norm_g', 'v_q_a_norm_g', 'v_kv_a_norm_g', 'v_w_q_up', 'v_w_kv_up', 'v_w_branch_a', 'v_w_branch_b', 'v_w_o', 'v_w_ffn_up', 'v_w_ffn_down', 'v_pre_mix_g', 'v_post_mix_g', 'v_pre_ffn_g', 'v_post_ffn_g']
TWIN_OUTPUTS = ['loss', 'grad_x', 'grad_w_in', 'grad_b_gate', 'grad_q_norm_g', 'grad_k_norm_g', 'grad_q_a_norm_g', 'grad_kv_a_norm_g', 'grad_w_q_up', 'grad_w_kv_up', 'grad_w_branch_a', 'grad_w_branch_b', 'grad_w_o', 'grad_w_ffn_up', 'grad_w_ffn_down', 'grad_pre_mix_g', 'grad_post_mix_g', 'grad_pre_ffn_g', 'grad_post_ffn_g', 'delta_w_in', 'delta_b_gate', 'delta_q_norm_g', 'delta_k_norm_g', 'delta_q_a_norm_g', 'delta_kv_a_norm_g', 'delta_w_q_up', 'delta_w_kv_up', 'delta_w_branch_a', 'delta_w_branch_b', 'delta_w_o', 'delta_w_ffn_up', 'delta_w_ffn_down', 'delta_pre_mix_g', 'delta_post_mix_g', 'delta_pre_ffn_g', 'delta_post_ffn_g', 'new_m_w_in', 'new_m_b_gate', 'new_m_q_norm_g', 'new_m_k_norm_g', 'new_m_q_a_norm_g', 'new_m_kv_a_norm_g', 'new_m_w_q_up', 'new_m_w_kv_up', 'new_m_w_branch_a', 'new_m_w_branch_b', 'new_m_w_o', 'new_m_w_ffn_up', 'new_m_w_ffn_down', 'new_m_pre_mix_g', 'new_m_post_mix_g', 'new_m_pre_ffn_g', 'new_m_post_ffn_g', 'new_v_w_in', 'new_v_b_gate', 'new_v_q_norm_g', 'new_v_k_norm_g', 'new_v_q_a_norm_g', 'new_v_kv_a_norm_g', 'new_v_w_q_up', 'new_v_w_kv_up', 'new_v_w_branch_a', 'new_v_w_branch_b', 'new_v_w_o', 'new_v_w_ffn_up', 'new_v_w_ffn_down', 'new_v_pre_mix_g', 'new_v_post_mix_g', 'new_v_pre_ffn_g', 'new_v_post_ffn_g']
TWIN_LEAF_KINDS = {'loss': 'loss', 'grad_x': 'grad_x', 'grad_w_in': 'grad_w', 'grad_b_gate': 'grad_w', 'grad_q_norm_g': 'grad_w', 'grad_k_norm_g': 'grad_w', 'grad_q_a_norm_g': 'grad_w', 'grad_kv_a_norm_g': 'grad_w', 'grad_w_q_up': 'grad_w', 'grad_w_kv_up': 'grad_w', 'grad_w_branch_a': 'grad_w', 'grad_w_branch_b': 'grad_w', 'grad_w_o': 'grad_w', 'grad_w_ffn_up': 'grad_w', 'grad_w_ffn_down': 'grad_w', 'grad_pre_mix_g': 'grad_w', 'grad_post_mix_g': 'grad_w', 'grad_pre_ffn_g': 'grad_w', 'grad_post_ffn_g': 'grad_w', 'delta_w_in': 'delta_w', 'delta_b_gate': 'delta_w', 'delta_q_norm_g': 'delta_w', 'delta_k_norm_g': 'delta_w', 'delta_q_a_norm_g': 'delta_w', 'delta_kv_a_norm_g': 'delta_w', 'delta_w_q_up': 'delta_w', 'delta_w_kv_up': 'delta_w', 'delta_w_branch_a': 'delta_w', 'delta_w_branch_b': 'delta_w', 'delta_w_o': 'delta_w', 'delta_w_ffn_up': 'delta_w', 'delta_w_ffn_down': 'delta_w', 'delta_pre_mix_g': 'delta_w', 'delta_post_mix_g': 'delta_w', 'delta_pre_ffn_g': 'delta_w', 'delta_post_ffn_g': 'delta_w', 'new_m_w_in': 'new_m', 'new_m_b_gate': 'new_m', 'new_m_q_norm_g': 'new_m', 'new_m_k_norm_g': 'new_m', 'new_m_q_a_norm_g': 'new_m', 'new_m_kv_a_norm_g': 'new_m', 'new_m_w_q_up': 'new_m', 'new_m_w_kv_up': 'new_m', 'new_m_w_branch_a': 'new_m', 'new_m_w_branch_b': 'new_m', 'new_m_w_o': 'new_m', 'new_m_w_ffn_up': 'new_m', 'new_m_w_ffn_down': 'new_m', 'new_m_pre_mix_g': 'new_m', 'new_m_post_mix_g': 'new_m', 'new_m_pre_ffn_g': 'new_m', 'new_m_post_ffn_g': 'new_m', 'new_v_w_in': 'new_v', 'new_v_b_gate': 'new_v', 'new_v_q_norm_g': 'new_v', 'new_v_k_norm_g': 'new_v', 'new_v_q_a_norm_g': 'new_v', 'new_v_kv_a_norm_g': 'new_v', 'new_v_w_q_up': 'new_v', 'new_v_w_kv_up': 'new_v', 'new_v_w_branch_a': 'new_v', 'new_v_w_branch_b': 'new_v', 'new_v_w_o': 'new_v', 'new_v_w_ffn_up': 'new_v', 'new_v_w_ffn_down': 'new_v', 'new_v_pre_mix_g': 'new_v', 'new_v_post_mix_g': 'new_v', 'new_v_pre_ffn_g': 'new_v', 'new_v_post_ffn_g': 'new_v'}


def _forward(args):
    return _fwd_reference(*[args[k] for k in FWD_PARAMS])


def _output_shape():
    def fwd():
        inp = _fwd_setup_inputs(0)
        return _fwd_reference(*[inp[k] for k in FWD_PARAMS])
    out = _jax.eval_shape(fwd)
    return out.shape, out.dtype

N_MICROBATCH = 1
ADAM_LR = 0.001
ADAM_B1 = 0.9
ADAM_B2 = 0.999
ADAM_EPS = 1e-08
ADAM_WD = 0.01
ADAM_STEP = 10
PER_EXAMPLE_BATCH_AXIS = {'x': 0, 'loss_target': 0}
SHARED_INPUTS = []
_WEIGHT_DTYPES = {'w_in': _jnp.float32, 'b_gate': _jnp.float32, 'q_norm_g': _jnp.float32, 'k_norm_g': _jnp.float32, 'q_a_norm_g': _jnp.float32, 'kv_a_norm_g': _jnp.float32, 'w_q_up': _jnp.float32, 'w_kv_up': _jnp.float32, 'w_branch_a': _jnp.float32, 'w_branch_b': _jnp.float32, 'w_o': _jnp.float32, 'w_ffn_up': _jnp.float32, 'w_ffn_down': _jnp.float32, 'pre_mix_g': _jnp.float32, 'post_mix_g': _jnp.float32, 'pre_ffn_g': _jnp.float32, 'post_ffn_g': _jnp.float32}
MOMENT_SCALE = {'w_in': 4.214110e+01, 'b_gate': 2.199936e+01, 'q_norm_g': 9.605570e+00, 'k_norm_g': 1.057521e+01, 'q_a_norm_g': 8.424268e+00, 'kv_a_norm_g': 1.079216e+02, 'w_q_up': 4.173266e+00, 'w_kv_up': 5.258044e+01, 'w_branch_a': 5.484780e+01, 'w_branch_b': 5.348318e+01, 'w_o': 7.683796e+01, 'w_ffn_up': 2.019185e+01, 'w_ffn_down': 7.247130e+01, 'pre_mix_g': 7.949552e+01, 'post_mix_g': 1.044691e+02, 'pre_ffn_g': 3.977833e+01, 'post_ffn_g': 9.799995e+01}


def _to_microbatches(a, axis):
    t = _jnp.moveaxis(a, axis, 0)
    t = t.reshape((N_MICROBATCH, t.shape[0] // N_MICROBATCH) + t.shape[1:])
    return _jnp.moveaxis(t, 1, axis + 1)


def setup_inputs(seed: int = 0) -> dict:
    inp = _fwd_setup_inputs(seed)
    key = _jax.random.fold_in(_jax.random.key(seed), 7919)
    shape, _ = _output_shape()
    out = dict(inp)
    out["loss_target"] = _jax.random.normal(_jax.random.fold_in(key, 0), shape, _jnp.float32)
    for i, name in enumerate(TWIN_WEIGHTS):
        w = inp[name].astype(_jnp.float32)
        if MOMENT_SCALE is None:
            s = _jnp.sqrt(_jnp.mean(_jnp.square(w)) + 1e-30)
        else:
            s = MOMENT_SCALE[name]
        km, kv = _jax.random.split(_jax.random.fold_in(key, i + 1))
        out[name] = w
        out["m_" + name] = s * _jax.random.normal(km, w.shape, _jnp.float32)
        out["v_" + name] = (s * s) * _jax.random.uniform(kv, w.shape, _jnp.float32, 0.5, 1.5)
    if N_MICROBATCH > 1:
        for name, axis in PER_EXAMPLE_BATCH_AXIS.items():
            out[name] = _to_microbatches(out[name], axis)
    return {'x': out['x'], 'w_in': out['w_in'], 'b_gate': out['b_gate'], 'q_norm_g': out['q_norm_g'], 'k_norm_g': out['k_norm_g'], 'q_a_norm_g': out['q_a_norm_g'], 'kv_a_norm_g': out['kv_a_norm_g'], 'w_q_up': out['w_q_up'], 'w_kv_up': out['w_kv_up'], 'w_branch_a': out['w_branch_a'], 'w_branch_b': out['w_branch_b'], 'w_o': out['w_o'], 'w_ffn_up': out['w_ffn_up'], 'w_ffn_down': out['w_ffn_down'], 'pre_mix_g': out['pre_mix_g'], 'post_mix_g': out['post_mix_g'], 'pre_ffn_g': out['pre_ffn_g'], 'post_ffn_g': out['post_ffn_g'], 'loss_target': out['loss_target'], 'm_w_in': out['m_w_in'], 'm_b_gate': out['m_b_gate'], 'm_q_norm_g': out['m_q_norm_g'], 'm_k_norm_g': out['m_k_norm_g'], 'm_q_a_norm_g': out['m_q_a_norm_g'], 'm_kv_a_norm_g': out['m_kv_a_norm_g'], 'm_w_q_up': out['m_w_q_up'], 'm_w_kv_up': out['m_w_kv_up'], 'm_w_branch_a': out['m_w_branch_a'], 'm_w_branch_b': out['m_w_branch_b'], 'm_w_o': out['m_w_o'], 'm_w_ffn_up': out['m_w_ffn_up'], 'm_w_ffn_down': out['m_w_ffn_down'], 'm_pre_mix_g': out['m_pre_mix_g'], 'm_post_mix_g': out['m_post_mix_g'], 'm_pre_ffn_g': out['m_pre_ffn_g'], 'm_post_ffn_g': out['m_post_ffn_g'], 'v_w_in': out['v_w_in'], 'v_b_gate': out['v_b_gate'], 'v_q_norm_g': out['v_q_norm_g'], 'v_k_norm_g': out['v_k_norm_g'], 'v_q_a_norm_g': out['v_q_a_norm_g'], 'v_kv_a_norm_g': out['v_kv_a_norm_g'], 'v_w_q_up': out['v_w_q_up'], 'v_w_kv_up': out['v_w_kv_up'], 'v_w_branch_a': out['v_w_branch_a'], 'v_w_branch_b': out['v_w_branch_b'], 'v_w_o': out['v_w_o'], 'v_w_ffn_up': out['v_w_ffn_up'], 'v_w_ffn_down': out['v_w_ffn_down'], 'v_pre_mix_g': out['v_pre_mix_g'], 'v_post_mix_g': out['v_post_mix_g'], 'v_pre_ffn_g': out['v_pre_ffn_g'], 'v_post_ffn_g': out['v_post_ffn_g']}


def _loss(weights, diff, rest, loss_target):
    with _jax.named_scope("forward"):
        args = {**rest, TWIN_DIFF_INPUT: diff, **{k: w.astype(_WEIGHT_DTYPES[k]) for k, w in weights.items()}}
        y = _forward(args)
    with _jax.named_scope("loss_head"):
        err = _jnp.square(y.astype(_jnp.float32) - loss_target)
        return 0.5 * _jnp.sum(_jnp.mean(err, axis=-1)) if err.ndim else 0.5 * err


def _adamw(w, g, m, v):
    m = ADAM_B1 * m + (1.0 - ADAM_B1) * g
    v = ADAM_B2 * v + (1.0 - ADAM_B2) * _jnp.square(g)
    m_hat = m / (1.0 - ADAM_B1 ** ADAM_STEP)
    v_hat = v / (1.0 - ADAM_B2 ** ADAM_STEP)
    delta = -ADAM_LR * (m_hat / (_jnp.sqrt(v_hat) + ADAM_EPS) + ADAM_WD * w)
    return delta, m, v


def reference(x, w_in, b_gate, q_norm_g, k_norm_g, q_a_norm_g, kv_a_norm_g, w_q_up, w_kv_up, w_branch_a, w_branch_b, w_o, w_ffn_up, w_ffn_down, pre_mix_g, post_mix_g, pre_ffn_g, post_ffn_g, loss_target, m_w_in, m_b_gate, m_q_norm_g, m_k_norm_g, m_q_a_norm_g, m_kv_a_norm_g, m_w_q_up, m_w_kv_up, m_w_branch_a, m_w_branch_b, m_w_o, m_w_ffn_up, m_w_ffn_down, m_pre_mix_g, m_post_mix_g, m_pre_ffn_g, m_post_ffn_g, v_w_in, v_b_gate, v_q_norm_g, v_k_norm_g, v_q_a_norm_g, v_kv_a_norm_g, v_w_q_up, v_w_kv_up, v_w_branch_a, v_w_branch_b, v_w_o, v_w_ffn_up, v_w_ffn_down, v_pre_mix_g, v_post_mix_g, v_pre_ffn_g, v_post_ffn_g):
    given = dict(x=x, w_in=w_in, b_gate=b_gate, q_norm_g=q_norm_g, k_norm_g=k_norm_g, q_a_norm_g=q_a_norm_g, kv_a_norm_g=kv_a_norm_g, w_q_up=w_q_up, w_kv_up=w_kv_up, w_branch_a=w_branch_a, w_branch_b=w_branch_b, w_o=w_o, w_ffn_up=w_ffn_up, w_ffn_down=w_ffn_down, pre_mix_g=pre_mix_g, post_mix_g=post_mix_g, pre_ffn_g=pre_ffn_g, post_ffn_g=post_ffn_g, loss_target=loss_target, m_w_in=m_w_in, m_b_gate=m_b_gate, m_q_norm_g=m_q_norm_g, m_k_norm_g=m_k_norm_g, m_q_a_norm_g=m_q_a_norm_g, m_kv_a_norm_g=m_kv_a_norm_g, m_w_q_up=m_w_q_up, m_w_kv_up=m_w_kv_up, m_w_branch_a=m_w_branch_a, m_w_branch_b=m_w_branch_b, m_w_o=m_w_o, m_w_ffn_up=m_w_ffn_up, m_w_ffn_down=m_w_ffn_down, m_pre_mix_g=m_pre_mix_g, m_post_mix_g=m_post_mix_g, m_pre_ffn_g=m_pre_ffn_g, m_post_ffn_g=m_post_ffn_g, v_w_in=v_w_in, v_b_gate=v_b_gate, v_q_norm_g=v_q_norm_g, v_k_norm_g=v_k_norm_g, v_q_a_norm_g=v_q_a_norm_g, v_kv_a_norm_g=v_kv_a_norm_g, v_w_q_up=v_w_q_up, v_w_kv_up=v_w_kv_up, v_w_branch_a=v_w_branch_a, v_w_branch_b=v_w_branch_b, v_w_o=v_w_o, v_w_ffn_up=v_w_ffn_up, v_w_ffn_down=v_w_ffn_down, v_pre_mix_g=v_pre_mix_g, v_post_mix_g=v_post_mix_g, v_pre_ffn_g=v_pre_ffn_g, v_post_ffn_g=v_post_ffn_g)
    weights = {n: given[n] for n in TWIN_WEIGHTS}
    shared = {n: given[n] for n in SHARED_INPUTS}
    per_example = {n: given[n] for n in ['x']}
    grad_fn = _jax.value_and_grad(_loss, argnums=(0, 1))

    def one_microbatch(ex, loss_target):
        ex = dict(ex)
        diff = ex.pop(TWIN_DIFF_INPUT)
        return grad_fn(weights, diff, {**shared, **ex}, loss_target)

    if N_MICROBATCH == 1:
        loss, (grad_w, grad_x) = one_microbatch(per_example, given["loss_target"])
    else:
        def body(carry, xs):
            loss_sum, grad_sum = carry
            l_k, (gw_k, gx_k) = one_microbatch(xs[0], xs[1])
            with _jax.named_scope("update"):
                return (loss_sum + l_k, _jax.tree.map(_jnp.add, grad_sum, gw_k)), gx_k

        init = (_jnp.zeros((), _jnp.float32), _jax.tree.map(_jnp.zeros_like, weights))
        (loss, grad_w), grad_x = _jax.lax.scan(body, init, (per_example, given["loss_target"]))
    with _jax.named_scope("update"):
        delta_w, new_m, new_v = {}, {}, {}
        for n in TWIN_WEIGHTS:
            delta_w[n], new_m[n], new_v[n] = _adamw(weights[n], grad_w[n], given["m_" + n], given["v_" + n])
    return (loss, grad_x, *[grad_w[n] for n in TWIN_WEIGHTS], *[delta_w[n] for n in TWIN_WEIGHTS],
            *[new_m[n] for n in TWIN_WEIGHTS], *[new_v[n] for n in TWIN_WEIGHTS])
```

```python
import functools
import math

import jax
import jax.numpy as jnp
from jax import lax
from jax.experimental import pallas as pl
from jax.experimental.pallas import tpu as pltpu

F32 = jnp.float32
BF = jnp.bfloat16
MESH = pl.DeviceIdType.MESH

EPS = 1e-6
D = 1024
NH = 8
HW = 128
GQA_KV = 2
GQA_G = 4
HEAD = 64
MLA_NOPE = 64
MLA_ROPE = 32
MLA_QK = 96
Q_RANK = 384
KV_RANK = 256
DFF = 4096
GRID_W = 64
ROPE_THETA = 10000.0

O_QA, O_KA, O_VA, O_CQ, O_CKV, O_KR, ZM_W = 0, 1024, 1280, 1536, 1920, 2176, 2304
ZG_W = 2048

ADAM_LR, ADAM_B1, ADAM_B2, ADAM_EPS, ADAM_WD, ADAM_STEP = 0.001, 0.9, 0.999, 1e-08, 0.01, 10

TM = 256
TQ = 256
TK = 512
VMEM_LIMIT = 56 * 1024 * 1024

BIG = ("w_in", "w_q_up", "w_kv_up", "w_branch_a", "w_branch_b", "w_o", "w_ffn_up", "w_ffn_down")
SMALL = ("b_gate", "q_norm_g", "k_norm_g", "q_a_norm_g", "kv_a_norm_g", "pre_mix_g", "post_mix_g", "pre_ffn_g", "post_ffn_g")
ORDER = ("w_in", "b_gate", "q_norm_g", "k_norm_g", "q_a_norm_g", "kv_a_norm_g", "w_q_up", "w_kv_up", "w_branch_a",
         "w_branch_b", "w_o", "w_ffn_up", "w_ffn_down", "pre_mix_g", "post_mix_g", "pre_ffn_g", "post_ffn_g")


def _params(sem=None):
    return pltpu.CompilerParams(dimension_semantics=sem, vmem_limit_bytes=VMEM_LIMIT)


def _rows(tm, w):
    return pl.BlockSpec((tm, w), lambda i: (i, 0))


def _whole(shape):
    return pl.BlockSpec(shape, lambda i: (0,) * len(shape))


def _dot(a, b):
    return jnp.dot(a, b, preferred_element_type=F32)


def _dot_nt(a, b):
    return lax.dot_general(a, b, (((1,), (1,)), ((), ())), preferred_element_type=F32)


def _dot_tn(a, b):
    return lax.dot_general(a, b, (((0,), (0,)), ((), ())), preferred_element_type=F32)


def _norm_fwd(xv, g, n=None):
    n = xv.shape[-1] if n is None else n
    r = lax.rsqrt(jnp.sum(xv * xv, axis=-1, keepdims=True) * (1.0 / n) + EPS)
    return (xv * r) * g


def _norm_bwd(xv, g, dy, n=None):
    n = xv.shape[-1] if n is None else n
    r = lax.rsqrt(jnp.sum(xv * xv, axis=-1, keepdims=True) * (1.0 / n) + EPS)
    xh = xv * r
    dxh = dy * g
    dg = jnp.sum(dy * xh, axis=0, keepdims=True)
    dx = r * (dxh - xh * (jnp.sum(dxh * xh, axis=-1, keepdims=True) * (1.0 / n)))
    return dx, dg


def _accumulate(ref, val):
    @pl.when(pl.program_id(0) == 0)
    def _():
        ref[...] = jnp.zeros_like(ref)

    ref[...] += val


def _rot(xv, q, neg):
    w = xv.shape[-1]
    return jnp.where(neg, -pltpu.roll(xv, w - q, 1), pltpu.roll(xv, q, 1))


def _rope(xv, c, s, q, neg):
    return xv * c + _rot(xv, q, neg) * s


def _rope_bwd(dy, c, s, q, neg):
    return dy * c - _rot(dy * s, q, neg)


def _neg_mask(tm, q):
    lane = lax.broadcasted_iota(jnp.int32, (tm, HW), 1)
    return (lane % (2 * q)) < q


def _sigmoid(z):
    return 1.0 / (1.0 + jnp.exp(-z))


def _fwd_inproj(x, g1, w1, w2):
    t = x.shape[0]

    def body(x_ref, g_ref, w1_ref, w2_ref, u_ref, zm_ref, zg_ref):
        u = _norm_fwd(x_ref[...], g_ref[...]).astype(BF)
        u_ref[...] = u
        for c in range(0, ZM_W, 768):
            zm_ref[:, c:c + 768] = _dot(u, w1_ref[:, c:c + 768])
        for c in range(0, ZG_W, 512):
            zg_ref[:, c:c + 512] = _dot(u, w2_ref[:, c:c + 512])

    return pl.pallas_call(
        body, name="fwd_inproj", grid=(t // TM,),
        in_specs=[_rows(TM, D), _whole((1, D)), _whole((D, ZM_W)), _whole((D, ZG_W))],
        out_specs=[_rows(TM, D), _rows(TM, ZM_W), _rows(TM, ZG_W)],
        out_shape=[jax.ShapeDtypeStruct((t, D), BF), jax.ShapeDtypeStruct((t, ZM_W), F32),
                   jax.ShapeDtypeStruct((t, ZG_W), F32)],
        compiler_params=_params(("parallel",)),
    )(x, g1, w1, w2)


def _fwd_mid(zm, gq, gk, gqa, gkva, wq, wk, wv, ca, sa, cb, sb):
    t = zm.shape[0]

    def body(zm_ref, gq_ref, gk_ref, gqa_ref, gkva_ref, wq_ref, wk_ref, wv_ref, ca_ref, sa_ref, cb_ref, sb_ref,
             qa_ref, ka_ref, va_ref, qb_ref, kb_ref, vb_ref, cqn_ref, ckvn_ref):
        ca_, sa_, cb_, sb_ = ca_ref[...], sa_ref[...], cb_ref[...], sb_ref[...]
        nega, negb = _neg_mask(TM, 16), _neg_mask(TM, 8)
        for h in range(NH):
            xv = zm_ref[:, O_QA + h * HW:O_QA + (h + 1) * HW]
            qa_ref[:, h * HW:(h + 1) * HW] = _rope(_norm_fwd(xv, gq_ref[...], HEAD), ca_, sa_, 16, nega).astype(BF)
        for h in range(GQA_KV):
            xv = zm_ref[:, O_KA + h * HW:O_KA + (h + 1) * HW]
            ka_ref[:, h * HW:(h + 1) * HW] = _rope(_norm_fwd(xv, gk_ref[...], HEAD), ca_, sa_, 16, nega).astype(BF)
        va_ref[...] = zm_ref[:, O_VA:O_VA + GQA_KV * HW].astype(BF)
        cqn = _norm_fwd(zm_ref[:, O_CQ:O_CQ + Q_RANK], gqa_ref[...]).astype(BF)
        ckvn = _norm_fwd(zm_ref[:, O_CKV:O_CKV + KV_RANK], gkva_ref[...]).astype(BF)
        cqn_ref[...] = cqn
        ckvn_ref[...] = ckvn
        qb = _dot(cqn, wq_ref[...])
        kpre = _dot(ckvn, wk_ref[...])
        kr = _rope(zm_ref[:, O_KR:O_KR + HW], cb_, sb_, 8, negb)
        for h in range(NH):
            sl = slice(h * HW, (h + 1) * HW)
            qb_ref[:, sl] = _rope(qb[:, sl], cb_, sb_, 8, negb).astype(BF)
            kb_ref[:, sl] = (kpre[:, sl] + kr).astype(BF)
        vb_ref[...] = _dot(ckvn, wv_ref[...]).astype(BF)

    hw8 = NH * HW
    return pl.pallas_call(
        body, name="fwd_mid", grid=(t // TM,),
        in_specs=[_rows(TM, ZM_W), _whole((1, HW)), _whole((1, HW)), _whole((1, Q_RANK)), _whole((1, KV_RANK)),
                  _whole((Q_RANK, hw8)), _whole((KV_RANK, hw8)), _whole((KV_RANK, hw8)),
                  _rows(TM, HW), _rows(TM, HW), _rows(TM, HW), _rows(TM, HW)],
        out_specs=[_rows(TM, hw8), _rows(TM, GQA_KV * HW), _rows(TM, GQA_KV * HW), _rows(TM, hw8), _rows(TM, hw8),
                   _rows(TM, hw8), _rows(TM, Q_RANK), _rows(TM, KV_RANK)],
        out_shape=[jax.ShapeDtypeStruct((t, hw8), BF), jax.ShapeDtypeStruct((t, GQA_KV * HW), BF),
                   jax.ShapeDtypeStruct((t, GQA_KV * HW), BF), jax.ShapeDtypeStruct((t, hw8), BF),
                   jax.ShapeDtypeStruct((t, hw8), BF), jax.ShapeDtypeStruct((t, hw8), BF),
                   jax.ShapeDtypeStruct((t, Q_RANK), BF), jax.ShapeDtypeStruct((t, KV_RANK), BF)],
        compiler_params=_params(("parallel",)),
    )(zm, gq, gk, gqa, gkva, wq, wk, wv, ca, sa, cb, sb)


def _flash_fwd(q, k, v, group, scale, name):
    t = q.shape[0]
    tq, tk = min(TQ, t), min(TK, t)
    nk = t // tk

    def body(q_ref, k_ref, v_ref, o_ref, lse_ref, lser_ref):
        qv = q_ref[...]

        def step(c, carry):
            m, l, acc = carry
            off = pl.multiple_of(c * tk, tk)
            ks = k_ref[pl.ds(off, tk), :]
            vs = v_ref[pl.ds(off, tk), :]
            s = _dot_nt(qv, ks) * scale
            mn = jnp.maximum(m, jnp.max(s, axis=-1, keepdims=True))
            al = jnp.exp(m - mn)
            p = jnp.exp(s - mn)
            l = al * l + jnp.sum(p, axis=-1, keepdims=True)
            acc = al * acc + _dot(p.astype(BF), vs)
            return mn, l, acc

        m0 = jnp.full((tq, 1), -jnp.inf, F32)
        l0 = jnp.zeros((tq, 1), F32)
        a0 = jnp.zeros((tq, HW), F32)
        m, l, acc = lax.fori_loop(0, nk, step, (m0, l0, a0))
        o_ref[...] = acc / l
        lse = jnp.broadcast_to(m + jnp.log(l), (tq, HW))
        lse_ref[...] = lse
        lser_ref[...] = lse.T[:8, :]

    qspec = pl.BlockSpec((tq, HW), lambda h, i: (i, h))
    kspec = pl.BlockSpec((t, HW), lambda h, i: (0, h // group))
    return pl.pallas_call(
        body, name=name, grid=(NH, t // tq),
        in_specs=[qspec, kspec, kspec],
        out_specs=[qspec, qspec, pl.BlockSpec((8, tq), lambda h, i: (h, i))],
        out_shape=[jax.ShapeDtypeStruct((t, NH * HW), F32), jax.ShapeDtypeStruct((t, NH * HW), F32),
                   jax.ShapeDtypeStruct((NH * 8, t), F32)],
        compiler_params=_params(("parallel", "parallel")),
    )(q, k, v)


def _fwd_merge(ya, yb, zg, b, wa, wb, wo, x, g2):
    t = x.shape[0]

    def body(ya_ref, yb_ref, zg_ref, b_ref, wa_ref, wb_ref, wo_ref, x_ref, g_ref, pa_ref, pb_ref, mg_ref, m_ref, x1_ref):
        pa = _dot(ya_ref[...].astype(BF), wa_ref[...])
        pb = _dot(yb_ref[...].astype(BF), wb_ref[...])
        pa_ref[...] = pa
        pb_ref[...] = pb
        ga = _sigmoid(zg_ref[:, :D] + b_ref[:, :D])
        gb = _sigmoid(zg_ref[:, D:] + b_ref[:, D:])
        mg = (ga * pa + gb * pb).astype(BF)
        mg_ref[...] = mg
        m = _dot(mg, wo_ref[...])
        m_ref[...] = m
        x1_ref[...] = x_ref[...] + _norm_fwd(m, g_ref[...])

    return pl.pallas_call(
        body, name="fwd_merge", grid=(t // TM,),
        in_specs=[_rows(TM, D), _rows(TM, D), _rows(TM, ZG_W), _whole((1, ZG_W)), _whole((D, D)), _whole((D, D)),
                  _whole((D, D)), _rows(TM, D), _whole((1, D))],
        out_specs=[_rows(TM, D), _rows(TM, D), _rows(TM, D), _rows(TM, D), _rows(TM, D)],
        out_shape=[jax.ShapeDtypeStruct((t, D), F32), jax.ShapeDtypeStruct((t, D), F32), jax.ShapeDtypeStruct((t, D), BF),
                   jax.ShapeDtypeStruct((t, D), F32), jax.ShapeDtypeStruct((t, D), F32)],
        compiler_params=_params(("parallel",)),
    )(ya, yb, zg, b, wa, wb, wo, x, g2)


def _fwd_ffn_up(x1, g3, wup):
    t = x1.shape[0]

    def body(x_ref, g_ref, w_ref, hn_ref, h_ref, a_ref):
        hn = _norm_fwd(x_ref[...], g_ref[...]).astype(BF)
        hn_ref[...] = hn
        for c in range(0, DFF, 1024):
            h = _dot(hn, w_ref[:, c:c + 1024])
            h_ref[:, c:c + 1024] = h
            r = jnp.maximum(h, 0.0)
            a_ref[:, c:c + 1024] = (r * r).astype(BF)

    return pl.pallas_call(
        body, name="fwd_ffn_up", grid=(t // TM,),
        in_specs=[_rows(TM, D), _whole((1, D)), _whole((D, DFF))],
        out_specs=[_rows(TM, D), _rows(TM, DFF), _rows(TM, DFF)],
        out_shape=[jax.ShapeDtypeStruct((t, D), BF), jax.ShapeDtypeStruct((t, DFF), F32), jax.ShapeDtypeStruct((t, DFF), BF)],
        compiler_params=_params(("parallel",)),
    )(x1, g3, wup)


def _fwd_ffn_down(a, wdn, x1, g4):
    t = x1.shape[0]

    def body(a_ref, w_ref, x_ref, g_ref, f_ref, x2_ref):
        f = _dot(a_ref[...], w_ref[...])
        f_ref[...] = f
        x2_ref[...] = x_ref[...] + _norm_fwd(f, g_ref[...])

    return pl.pallas_call(
        body, name="fwd_ffn_down", grid=(t // TM,),
        in_specs=[_rows(TM, DFF), _whole((DFF, D)), _rows(TM, D), _whole((1, D))],
        out_specs=[_rows(TM, D), _rows(TM, D)],
        out_shape=[jax.ShapeDtypeStruct((t, D), F32), jax.ShapeDtypeStruct((t, D), F32)],
        compiler_params=_params(("parallel",)),
    )(a, wdn, x1, g4)


def _loss_head(y, target):
    t = y.shape[0]

    def body(y_ref, t_ref, dy_ref, loss_ref):
        d = y_ref[...] - t_ref[...]
        dy_ref[...] = d * (1.0 / D)
        part = 0.5 * jnp.sum(jnp.sum(d * d, axis=-1, keepdims=True) * (1.0 / D), axis=0, keepdims=True)
        _accumulate(loss_ref, jnp.broadcast_to(part, (8, HW)))

    return pl.pallas_call(
        body, name="loss_head", grid=(t // TM,),
        in_specs=[_rows(TM, D), _rows(TM, D)],
        out_specs=[_rows(TM, D), _whole((8, HW))],
        out_shape=[jax.ShapeDtypeStruct((t, D), F32), jax.ShapeDtypeStruct((8, HW), F32)],
        compiler_params=_params(("arbitrary",)),
    )(y, target)


def _bwd_ffn_down(f, dx2, g4, wdn, h):
    t = f.shape[0]

    def body(f_ref, dx2_ref, g_ref, w_ref, h_ref, df_ref, dh_ref, dg_ref):
        df, dg = _norm_bwd(f_ref[...], g_ref[...], dx2_ref[...])
        _accumulate(dg_ref, dg)
        df16 = df.astype(BF)
        df_ref[...] = df16
        for c in range(0, DFF, 1024):
            da = _dot_nt(df16, w_ref[c:c + 1024, :])
            dh_ref[:, c:c + 1024] = (da * (2.0 * jnp.maximum(h_ref[:, c:c + 1024], 0.0))).astype(BF)

    return pl.pallas_call(
        body, name="bwd_ffn_down", grid=(t // TM,),
        in_specs=[_rows(TM, D), _rows(TM, D), _whole((1, D)), _whole((DFF, D)), _rows(TM, DFF)],
        out_specs=[_rows(TM, D), _rows(TM, DFF), _whole((1, D))],
        out_shape=[jax.ShapeDtypeStruct((t, D), BF), jax.ShapeDtypeStruct((t, DFF), BF), jax.ShapeDtypeStruct((1, D), F32)],
        compiler_params=_params(("arbitrary",)),
    )(f, dx2, g4, wdn, h)


def _bwd_ffn_up(dh, wup, x1, g3, dx2):
    t = x1.shape[0]

    def body(dh_ref, w_ref, x_ref, g_ref, dx2_ref, dx1_ref, dg_ref):
        dhn = _dot_nt(dh_ref[...], w_ref[...])
        dx, dg = _norm_bwd(x_ref[...], g_ref[...], dhn)
        _accumulate(dg_ref, dg)
        dx1_ref[...] = dx2_ref[...] + dx

    return pl.pallas_call(
        body, name="bwd_ffn_up", grid=(t // TM,),
        in_specs=[_rows(TM, DFF), _whole((D, DFF)), _rows(TM, D), _whole((1, D)), _rows(TM, D)],
        out_specs=[_rows(TM, D), _whole((1, D))],
        out_shape=[jax.ShapeDtypeStruct((t, D), F32), jax.ShapeDtypeStruct((1, D), F32)],
        compiler_params=_params(("arbitrary",)),
    )(dh, wup, x1, g3, dx2)


def _bwd_merge(m, dx1, g2, wo, zg, b, pa, pb, wa, wb):
    t = m.shape[0]

    def body(m_ref, dx1_ref, g_ref, wo_ref, zg_ref, b_ref, pa_ref, pb_ref, wa_ref, wb_ref,
             dm_ref, dpa_ref, dpb_ref, dzg_ref, dya_ref, dyb_ref, dg_ref, db_ref):
        dm, dg = _norm_bwd(m_ref[...], g_ref[...], dx1_ref[...])
        _accumulate(dg_ref, dg)
        dm16 = dm.astype(BF)
        dm_ref[...] = dm16
        dmg = _dot_nt(dm16, wo_ref[...])
        ga = _sigmoid(zg_ref[:, :D] + b_ref[:, :D])
        gb = _sigmoid(zg_ref[:, D:] + b_ref[:, D:])
        dpa = (dmg * ga).astype(BF)
        dpb = (dmg * gb).astype(BF)
        dpa_ref[...] = dpa
        dpb_ref[...] = dpb
        dza = (dmg * pa_ref[...]) * (ga * (1.0 - ga))
        dzb = (dmg * pb_ref[...]) * (gb * (1.0 - gb))
        dzg_ref[:, :D] = dza.astype(BF)
        dzg_ref[:, D:] = dzb.astype(BF)

        @pl.when(pl.program_id(0) == 0)
        def _():
            db_ref[...] = jnp.zeros_like(db_ref)

        db_ref[:, :D] += jnp.sum(dza, axis=0, keepdims=True)
        db_ref[:, D:] += jnp.sum(dzb, axis=0, keepdims=True)
        dya_ref[...] = _dot_nt(dpa, wa_ref[...]).astype(BF)
        dyb_ref[...] = _dot_nt(dpb, wb_ref[...]).astype(BF)

    return pl.pallas_call(
        body, name="bwd_merge", grid=(t // TM,),
        in_specs=[_rows(TM, D), _rows(TM, D), _whole((1, D)), _whole((D, D)), _rows(TM, ZG_W), _whole((1, ZG_W)),
                  _rows(TM, D), _rows(TM, D), _whole((D, D)), _whole((D, D))],
        out_specs=[_rows(TM, D), _rows(TM, D), _rows(TM, D), _rows(TM, ZG_W), _rows(TM, D), _rows(TM, D),
                   _whole((1, D)), _whole((1, ZG_W))],
        out_shape=[jax.ShapeDtypeStruct((t, D), BF), jax.ShapeDtypeStruct((t, D), BF), jax.ShapeDtypeStruct((t, D), BF),
                   jax.ShapeDtypeStruct((t, ZG_W), BF), jax.ShapeDtypeStruct((t, D), BF), jax.ShapeDtypeStruct((t, D), BF),
                   jax.ShapeDtypeStruct((1, D), F32), jax.ShapeDtypeStruct((1, ZG_W), F32)],
        compiler_params=_params(("arbitrary",)),
    )(m, dx1, g2, wo, zg, b, pa, pb, wa, wb)


def _flash_bwd_dq(q, do, o, lse, k, v, group, scale, name):
    t = q.shape[0]
    tq, tk = min(TQ, t), min(TK, t)
    nk = t // tk

    def body(q_ref, do_ref, o_ref, lse_ref, k_ref, v_ref, dq_ref, drow_ref):
        qv = q_ref[...]
        dov = do_ref[...]
        dsum = jnp.sum(dov.astype(F32) * o_ref[...], axis=-1, keepdims=True)
        drow_ref[...] = jnp.broadcast_to(dsum, (tq, HW)).T[:8, :]
        lse1 = lse_ref[:, :1]

        def step(c, acc):
            off = pl.multiple_of(c * tk, tk)
            ks = k_ref[pl.ds(off, tk), :]
            vs = v_ref[pl.ds(off, tk), :]
            p = jnp.exp(_dot_nt(qv, ks) * scale - lse1)
            ds = p * (_dot_nt(dov, vs) - dsum)
            return acc + _dot(ds.astype(BF), ks)

        acc = lax.fori_loop(0, nk, step, jnp.zeros((tq, HW), F32))
        dq_ref[...] = acc * scale

    qspec = pl.BlockSpec((tq, HW), lambda h, i: (i, h))
    kspec = pl.BlockSpec((t, HW), lambda h, i: (0, h // group))
    return pl.pallas_call(
        body, name=name, grid=(NH, t // tq),
        in_specs=[qspec, qspec, qspec, qspec, kspec, kspec],
        out_specs=[qspec, pl.BlockSpec((8, tq), lambda h, i: (h, i))],
        out_shape=[jax.ShapeDtypeStruct((t, NH * HW), F32), jax.ShapeDtypeStruct((NH * 8, t), F32)],
        compiler_params=_params(("parallel", "parallel")),
    )(q, do, o, lse, k, v)


def _flash_bwd_dkv(q, do, lser, drow, k, v, group, scale, name):
    t = q.shape[0]
    tq, tk = min(TQ, t), min(TK, t)
    nq = t // tq

    def body(k_ref, v_ref, q_ref, do_ref, lser_ref, drow_ref, dk_ref, dv_ref):
        kv = k_ref[...]
        vv = v_ref[...]

        def step(c, carry):
            dk, dv = carry
            off = pl.multiple_of(c * tq, tq)
            qs = q_ref[pl.ds(off, tq), :]
            dos = do_ref[pl.ds(off, tq), :]
            lr = lser_ref[0:1, pl.ds(off, tq)]
            dr = drow_ref[0:1, pl.ds(off, tq)]
            pt = jnp.exp(_dot_nt(kv, qs) * scale - lr)
            dst = pt * (_dot_nt(vv, dos) - dr)
            dv = dv + _dot(pt.astype(BF), dos)
            dk = dk + _dot(dst.astype(BF), qs)
            return dk, dv

        z = jnp.zeros((tk, HW), F32)
        dk, dv = lax.fori_loop(0, nq, step, (z, z))
        dk_ref[...] = dk * scale
        dv_ref[...] = dv

    kspec = pl.BlockSpec((tk, HW), lambda h, j: (j, h // group))
    qspec = pl.BlockSpec((t, HW), lambda h, j: (0, h))
    rspec = pl.BlockSpec((8, t), lambda h, j: (h, 0))
    ospec = pl.BlockSpec((tk, HW), lambda h, j: (j, h))
    return pl.pallas_call(
        body, name=name, grid=(NH, t // tk),
        in_specs=[kspec, kspec, qspec, qspec, rspec, rspec],
        out_specs=[ospec, ospec],
        out_shape=[jax.ShapeDtypeStruct((t, NH * HW), F32), jax.ShapeDtypeStruct((t, NH * HW), F32)],
        compiler_params=_params(("parallel", "parallel")),
    )(k, v, q, do, lser, drow)


def _bwd_mid(zm, dqa, dkap, dvap, dqb, dkb, dvb, gq, gk, gqa, gkva, wq, wk, wv, ca, sa, cb, sb):
    t = zm.shape[0]
    hw8 = NH * HW

    def body(zm_ref, dqa_ref, dkap_ref, dvap_ref, dqb_ref, dkb_ref, dvb_ref, gq_ref, gk_ref, gqa_ref, gkva_ref,
             wq_ref, wk_ref, wv_ref, ca_ref, sa_ref, cb_ref, sb_ref,
             dzm_ref, dqbp_ref, dkb16_ref, dvb16_ref, dgq_ref, dgk_ref, dgqa_ref, dgkva_ref):
        ca_, sa_, cb_, sb_ = ca_ref[...], sa_ref[...], cb_ref[...], sb_ref[...]
        nega, negb = _neg_mask(TM, 16), _neg_mask(TM, 8)
        dgq = jnp.zeros((1, HW), F32)
        for h in range(NH):
            sl = slice(h * HW, (h + 1) * HW)
            dqn = _rope_bwd(dqa_ref[:, sl], ca_, sa_, 16, nega)
            dx, dg = _norm_bwd(zm_ref[:, O_QA + h * HW:O_QA + (h + 1) * HW], gq_ref[...], dqn, HEAD)
            dzm_ref[:, O_QA + h * HW:O_QA + (h + 1) * HW] = dx.astype(BF)
            dgq = dgq + dg
        _accumulate(dgq_ref, dgq)
        dgk = jnp.zeros((1, HW), F32)
        for j in range(GQA_KV):
            dk = dkap_ref[:, (GQA_G * j) * HW:(GQA_G * j + 1) * HW]
            dv = dvap_ref[:, (GQA_G * j) * HW:(GQA_G * j + 1) * HW]
            for g in range(1, GQA_G):
                sl = slice((GQA_G * j + g) * HW, (GQA_G * j + g + 1) * HW)
                dk = dk + dkap_ref[:, sl]
                dv = dv + dvap_ref[:, sl]
            dkn = _rope_bwd(dk, ca_, sa_, 16, nega)
            dx, dg = _norm_bwd(zm_ref[:, O_KA + j * HW:O_KA + (j + 1) * HW], gk_ref[...], dkn, HEAD)
            dzm_ref[:, O_KA + j * HW:O_KA + (j + 1) * HW] = dx.astype(BF)
            dzm_ref[:, O_VA + j * HW:O_VA + (j + 1) * HW] = dv.astype(BF)
            dgk = dgk + dg
        _accumulate(dgk_ref, dgk)
        for h in range(NH):
            sl = slice(h * HW, (h + 1) * HW)
            dqbp_ref[:, sl] = _rope_bwd(dqb_ref[:, sl], cb_, sb_, 8, negb).astype(BF)
        dcqn = _dot_nt(dqbp_ref[...], wq_ref[...])
        dx, dg = _norm_bwd(zm_ref[:, O_CQ:O_CQ + Q_RANK], gqa_ref[...], dcqn)
        dzm_ref[:, O_CQ:O_CQ + Q_RANK] = dx.astype(BF)
        _accumulate(dgqa_ref, dg)
        dkb16 = dkb_ref[...].astype(BF)
        dvb16 = dvb_ref[...].astype(BF)
        dkb16_ref[...] = dkb16
        dvb16_ref[...] = dvb16
        dkr = dkb_ref[:, 0:HW]
        for h in range(1, NH):
            dkr = dkr + dkb_ref[:, h * HW:(h + 1) * HW]
        lane = lax.broadcasted_iota(jnp.int32, (TM, HW), 1)
        in_rope = (lane >= MLA_NOPE) & (lane < MLA_QK)
        dzm_ref[:, O_KR:O_KR + HW] = jnp.where(in_rope, _rope_bwd(dkr, cb_, sb_, 8, negb), 0.0).astype(BF)
        dckvn = _dot_nt(dkb16, wk_ref[...]) + _dot_nt(dvb16, wv_ref[...])
        dx, dg = _norm_bwd(zm_ref[:, O_CKV:O_CKV + KV_RANK], gkva_ref[...], dckvn)
        dzm_ref[:, O_CKV:O_CKV + KV_RANK] = dx.astype(BF)
        _accumulate(dgkva_ref, dg)

    return pl.pallas_call(
        body, name="bwd_mid", grid=(t // TM,),
        in_specs=[_rows(TM, ZM_W), _rows(TM, hw8), _rows(TM, hw8), _rows(TM, hw8), _rows(TM, hw8), _rows(TM, hw8),
                  _rows(TM, hw8), _whole((1, HW)), _whole((1, HW)), _whole((1, Q_RANK)), _whole((1, KV_RANK)),
                  _whole((Q_RANK, hw8)), _whole((KV_RANK, hw8)), _whole((KV_RANK, hw8)),
                  _rows(TM, HW), _rows(TM, HW), _rows(TM, HW), _rows(TM, HW)],
        out_specs=[_rows(TM, ZM_W), _rows(TM, hw8), _rows(TM, hw8), _rows(TM, hw8),
                   _whole((1, HW)), _whole((1, HW)), _whole((1, Q_RANK)), _whole((1, KV_RANK))],
        out_shape=[jax.ShapeDtypeStruct((t, ZM_W), BF), jax.ShapeDtypeStruct((t, hw8), BF), jax.ShapeDtypeStruct((t, hw8), BF),
                   jax.ShapeDtypeStruct((t, hw8), BF), jax.ShapeDtypeStruct((1, HW), F32), jax.ShapeDtypeStruct((1, HW), F32),
                   jax.ShapeDtypeStruct((1, Q_RANK), F32), jax.ShapeDtypeStruct((1, KV_RANK), F32)],
        compiler_params=_params(("arbitrary",)),
    )(zm, dqa, dkap, dvap, dqb, dkb, dvb, gq, gk, gqa, gkva, wq, wk, wv, ca, sa, cb, sb)


def _bwd_inproj(dzm, dzg, w1, w2, x, g1, dx1):
    t = x.shape[0]

    def body(dzm_ref, dzg_ref, w1_ref, w2_ref, x_ref, g_ref, dx1_ref, dx_ref, dg_ref):
        du = _dot_nt(dzm_ref[...], w1_ref[...]) + _dot_nt(dzg_ref[...], w2_ref[...])
        dx, dg = _norm_bwd(x_ref[...], g_ref[...], du)
        _accumulate(dg_ref, dg)
        dx_ref[...] = dx1_ref[...] + dx

    return pl.pallas_call(
        body, name="bwd_inproj", grid=(t // TM,),
        in_specs=[_rows(TM, ZM_W), _rows(TM, ZG_W), _whole((D, ZM_W)), _whole((D, ZG_W)), _rows(TM, D), _whole((1, D)),
                  _rows(TM, D)],
        out_specs=[_rows(TM, D), _whole((1, D))],
        out_shape=[jax.ShapeDtypeStruct((t, D), F32), jax.ShapeDtypeStruct((1, D), F32)],
        compiler_params=_params(("arbitrary",)),
    )(dzm, dzg, w1, w2, x, g1, dx1)


def _matmul_tn(a, b, tn, name):
    t, kdim = a.shape
    n = b.shape[1]
    tm = min(512, t)
    nsteps = t // tm

    def body(a_ref, b_ref, o_ref, acc_ref):
        i = pl.program_id(1)

        @pl.when(i == 0)
        def _():
            acc_ref[...] = jnp.zeros_like(acc_ref)

        acc_ref[...] += _dot_tn(a_ref[...].astype(BF), b_ref[...])

        @pl.when(i == nsteps - 1)
        def _():
            o_ref[...] = acc_ref[...].astype(BF)

    return pl.pallas_call(
        body, name=name, grid=(n // tn, nsteps),
        in_specs=[pl.BlockSpec((tm, kdim), lambda j, i: (i, 0)), pl.BlockSpec((tm, tn), lambda j, i: (i, j))],
        out_specs=pl.BlockSpec((kdim, tn), lambda j, i: (0, j)),
        out_shape=jax.ShapeDtypeStruct((kdim, n), BF),
        scratch_shapes=[pltpu.VMEM((kdim, tn), F32)],
        compiler_params=_params(("parallel", "arbitrary")),
    )(a, b)


def _rope_tables(t):
    pos = jnp.arange(t, dtype=jnp.int32)
    row = (pos // GRID_W).astype(F32)
    col = (pos % GRID_W).astype(F32)

    def table(rot_dim):
        half = rot_dim // 2
        inv = ROPE_THETA ** (-jnp.arange(0, half, 2, dtype=F32) / half)
        ar = row[:, None] * inv[None, :]
        ac = col[:, None] * inv[None, :]
        ang = jnp.concatenate([ar, ar, ac, ac], axis=-1)
        return jnp.cos(ang), jnp.sin(ang)

    c64, s64 = table(HEAD)
    c32, s32 = table(MLA_ROPE)
    ones = lambda w: jnp.ones((t, w), F32)
    zeros = lambda w: jnp.zeros((t, w), F32)
    ca = jnp.concatenate([c64, ones(HW - HEAD)], axis=1)
    sa = jnp.concatenate([s64, zeros(HW - HEAD)], axis=1)
    cb = jnp.concatenate([ones(MLA_NOPE), c32, ones(HW - MLA_QK)], axis=1)
    sb = jnp.concatenate([zeros(MLA_NOPE), s32, zeros(HW - MLA_QK)], axis=1)
    return ca, sa, cb, sb


def _pad_heads_cols(w, heads, width):
    k = w.shape[0]
    w = w.reshape(k, heads, width)
    return jnp.pad(w, ((0, 0), (0, 0), (0, HW - width))).reshape(k, heads * HW)


def _pad_heads_rows(w, heads, width):
    n = w.shape[1]
    w = w.reshape(heads, width, n)
    return jnp.pad(w, ((0, 0), (0, HW - width), (0, 0))).reshape(heads * HW, n)


def _unpad_heads_cols(w, heads, width):
    k = w.shape[0]
    return w.reshape(k, heads, HW)[:, :, :width].reshape(k, heads * width)


def _unpad_heads_rows(w, heads, width):
    n = w.shape[1]
    return w.reshape(heads, HW, n)[:, :width, :].reshape(heads * width, n)


def _pad_layer_weights(w):
    w_in = w["w_in"]
    qa = _pad_heads_cols(w_in[:, 0:512], NH, HEAD)
    ka = _pad_heads_cols(w_in[:, 512:640], GQA_KV, HEAD)
    va = _pad_heads_cols(w_in[:, 640:768], GQA_KV, HEAD)
    cq = w_in[:, 768:1152]
    ckv = w_in[:, 1152:1408]
    kr = jnp.pad(w_in[:, 1408:1440], ((0, 0), (MLA_NOPE, HW - MLA_QK)))
    kvu = w["w_kv_up"].reshape(KV_RANK, NH, 2 * HEAD)
    return dict(
        w1=jnp.concatenate([qa, ka, va, cq, ckv, kr], axis=1),
        w2=w_in[:, 1440:],
        wq=_pad_heads_cols(w["w_q_up"], NH, MLA_QK),
        wk=jnp.pad(kvu[:, :, :HEAD], ((0, 0), (0, 0), (0, HEAD))).reshape(KV_RANK, NH * HW),
        wv=jnp.pad(kvu[:, :, HEAD:], ((0, 0), (0, 0), (0, HEAD))).reshape(KV_RANK, NH * HW),
        wa=_pad_heads_rows(w["w_branch_a"], NH, HEAD),
        wb=_pad_heads_rows(w["w_branch_b"], NH, HEAD),
        wo=w["w_o"], wup=w["w_ffn_up"], wdn=w["w_ffn_down"],
    )


def _unpad_layer_grads(g):
    d1 = g["w1"]
    w_in = jnp.concatenate([
        _unpad_heads_cols(d1[:, O_QA:O_KA], NH, HEAD), _unpad_heads_cols(d1[:, O_KA:O_VA], GQA_KV, HEAD),
        _unpad_heads_cols(d1[:, O_VA:O_CQ], GQA_KV, HEAD), d1[:, O_CQ:O_CKV], d1[:, O_CKV:O_KR],
        d1[:, O_KR + MLA_NOPE:O_KR + MLA_QK], g["w2"]], axis=1)
    dk = g["wk"].reshape(KV_RANK, NH, HW)[:, :, :HEAD]
    dv = g["wv"].reshape(KV_RANK, NH, HW)[:, :, :HEAD]
    return dict(
        w_in=w_in,
        w_q_up=_unpad_heads_cols(g["wq"], NH, MLA_QK),
        w_kv_up=jnp.concatenate([dk, dv], axis=2).reshape(KV_RANK, NH * 2 * HEAD),
        w_branch_a=_unpad_heads_rows(g["wa"], NH, HEAD),
        w_branch_b=_unpad_heads_rows(g["wb"], NH, HEAD),
        w_o=g["wo"], w_ffn_up=g["wup"], w_ffn_down=g["wdn"],
    )


def _pad_lanes(v, width):
    return jnp.pad(v, (0, HW - width)).reshape(1, HW)


def _local_step(x, target, layers, smalls):
    t = x.shape[0]
    ca, sa, cb, sb = _rope_tables(t)
    sc_a = 1.0 / math.sqrt(HEAD)
    sc_b = 1.0 / math.sqrt(MLA_QK)
    saved = []
    for w, s in zip(layers, smalls):
        gq, gk = _pad_lanes(s["q_norm_g"], HEAD), _pad_lanes(s["k_norm_g"], HEAD)
        gqa, gkva = s["q_a_norm_g"].reshape(1, -1), s["kv_a_norm_g"].reshape(1, -1)
        g1, g2, g3, g4 = (s[n].reshape(1, D) for n in ("pre_mix_g", "post_mix_g", "pre_ffn_g", "post_ffn_g"))
        b = s["b_gate"].reshape(1, ZG_W)
        u, zm, zg = _fwd_inproj(x, g1, w["w1"], w["w2"])
        qa, ka, va, qb, kb, vb, cqn, ckvn = _fwd_mid(zm, gq, gk, gqa, gkva, w["wq"], w["wk"], w["wv"], ca, sa, cb, sb)
        ya, lse_a, lser_a = _flash_fwd(qa, ka, va, GQA_G, sc_a, "flash_fwd_gqa")
        yb, lse_b, lser_b = _flash_fwd(qb, kb, vb, 1, sc_b, "flash_fwd_mla")
        pa, pb, mg, m, x1 = _fwd_merge(ya, yb, zg, b, w["wa"], w["wb"], w["wo"], x, g2)
        hn, h, a = _fwd_ffn_up(x1, g3, w["wup"])
        f, x2 = _fwd_ffn_down(a, w["wdn"], x1, g4)
        saved.append(dict(x=x, u=u, zm=zm, zg=zg, qa=qa, ka=ka, va=va, qb=qb, kb=kb, vb=vb, cqn=cqn, ckvn=ckvn,
                          ya=ya, lse_a=lse_a, lser_a=lser_a, yb=yb, lse_b=lse_b, lser_b=lser_b, pa=pa, pb=pb, mg=mg,
                          m=m, x1=x1, hn=hn, h=h, a=a, f=f, gq=gq, gk=gk, gqa=gqa, gkva=gkva, g1=g1, g2=g2, g3=g3,
                          g4=g4, b=b))
        x = x2

    dx, loss8 = _loss_head(x, target)
    loss = loss8[0, 0]

    grads = [None] * len(layers)
    for li in range(len(layers) - 1, -1, -1):
        w, r = layers[li], saved[li]
        df, dh, dg4 = _bwd_ffn_down(r["f"], dx, r["g4"], w["wdn"], r["h"])
        dx1, dg3 = _bwd_ffn_up(dh, w["wup"], r["x1"], r["g3"], dx)
        dm, dpa, dpb, dzg, dya, dyb, dg2, db = _bwd_merge(r["m"], dx1, r["g2"], w["wo"], r["zg"], r["b"], r["pa"], r["pb"],
                                                         w["wa"], w["wb"])
        dqa, drow_a = _flash_bwd_dq(r["qa"], dya, r["ya"], r["lse_a"], r["ka"], r["va"], GQA_G, sc_a, "flash_dq_gqa")
        dkap, dvap = _flash_bwd_dkv(r["qa"], dya, r["lser_a"], drow_a, r["ka"], r["va"], GQA_G, sc_a, "flash_dkv_gqa")
        dqb, drow_b = _flash_bwd_dq(r["qb"], dyb, r["yb"], r["lse_b"], r["kb"], r["vb"], 1, sc_b, "flash_dq_mla")
        dkb, dvb = _flash_bwd_dkv(r["qb"], dyb, r["lser_b"], drow_b, r["kb"], r["vb"], 1, sc_b, "flash_dkv_mla")
        dzm, dqbp, dkb16, dvb16, dgq, dgk, dgqa, dgkva = _bwd_mid(
            r["zm"], dqa, dkap, dvap, dqb, dkb, dvb, r["gq"], r["gk"], r["gqa"], r["gkva"], w["wq"], w["wk"], w["wv"],
            ca, sa, cb, sb)
        dx, dg1 = _bwd_inproj(dzm, dzg, w["w1"], w["w2"], r["x"], r["g1"], dx1)
        big = dict(
            w1=_matmul_tn(r["u"], dzm, 768, "dw_in_main"), w2=_matmul_tn(r["u"], dzg, 512, "dw_in_gate"),
            wq=_matmul_tn(r["cqn"], dqbp, 512, "dw_q_up"), wk=_matmul_tn(r["ckvn"], dkb16, 512, "dw_k_up"),
            wv=_matmul_tn(r["ckvn"], dvb16, 512, "dw_v_up"), wa=_matmul_tn(r["ya"], dpa, 512, "dw_branch_a"),
            wb=_matmul_tn(r["yb"], dpb, 512, "dw_branch_b"), wo=_matmul_tn(r["mg"], dm, 512, "dw_o"),
            wup=_matmul_tn(r["hn"], dh, 512, "dw_ffn_up"), wdn=_matmul_tn(r["a"], df, 512, "dw_ffn_down"))
        small = dict(b_gate=db[0], q_norm_g=dgq[0, :HEAD], k_norm_g=dgk[0, :HEAD], q_a_norm_g=dgqa[0],
                     kv_a_norm_g=dgkva[0], pre_mix_g=dg1[0], post_mix_g=dg2[0], pre_ffn_g=dg3[0], post_ffn_g=dg4[0])
        grads[li] = (big, small)
    return loss, dx, grads


def _chip_exchange(src, per_dest, name):
    shape = src.shape[-2:]

    def body(src_ref, out_ref, send_sems, recv_sems, local_sem):
        x, y, c = lax.axis_index("x"), lax.axis_index("y"), lax.axis_index("c")
        me = 2 * x + y
        chips = [(1 - x, y), (x, 1 - y), (1 - x, 1 - y)]

        def piece(k):
            return src_ref.at[k] if per_dest else src_ref

        mine = pltpu.make_async_copy(piece(me), out_ref.at[me], local_sem)
        mine.start()
        sends = []
        for j, (px, py) in enumerate(chips):
            cp = pltpu.make_async_remote_copy(
                src_ref=piece(2 * px + py), dst_ref=out_ref.at[me], send_sem=send_sems.at[j], recv_sem=recv_sems.at[j],
                device_id=(px, py, c), device_id_type=MESH)
            cp.start()
            sends.append(cp)
        for j, (px, py) in enumerate(chips):
            pltpu.make_async_remote_copy(
                src_ref=piece(me), dst_ref=out_ref.at[2 * px + py], send_sem=send_sems.at[j], recv_sem=recv_sems.at[j],
                device_id=(px, py, c), device_id_type=MESH).wait_recv()
        for cp in sends:
            cp.wait_send()
        mine.wait()

    return pl.pallas_call(
        body, name=name,
        in_specs=[pl.BlockSpec(memory_space=pl.ANY)],
        out_specs=pl.BlockSpec(memory_space=pl.ANY),
        out_shape=jax.ShapeDtypeStruct((4,) + shape, src.dtype),
        scratch_shapes=[pltpu.SemaphoreType.DMA((3,)), pltpu.SemaphoreType.DMA((3,)), pltpu.SemaphoreType.DMA],
    )(src)


def _sibling_exchange(src, name):
    def body(src_ref, out_ref, send_sem, recv_sem):
        x, y, c = lax.axis_index("x"), lax.axis_index("y"), lax.axis_index("c")
        cp = pltpu.make_async_remote_copy(src_ref=src_ref, dst_ref=out_ref, send_sem=send_sem, recv_sem=recv_sem,
                                          device_id=(x, y, 1 - c), device_id_type=MESH)
        cp.start()
        cp.wait()

    return pl.pallas_call(
        body, name=name,
        in_specs=[pl.BlockSpec(memory_space=pl.ANY)],
        out_specs=pl.BlockSpec(memory_space=pl.ANY),
        out_shape=jax.ShapeDtypeStruct(src.shape, src.dtype),
        scratch_shapes=[pltpu.SemaphoreType.DMA, pltpu.SemaphoreType.DMA],
    )(src)


def _allgather_small(v):
    m_per, n = v.shape

    def body(x_ref, out_ref, send_sems, recv_sems, local_sem):
        x, y, c = lax.axis_index("x"), lax.axis_index("y"), lax.axis_index("c")
        me, sibling = (x, y, c), (x, y, 1 - c)
        chips = [(1 - x, y), (x, 1 - y), (1 - x, 1 - y)]

        def rows(px, py, pc):
            return out_ref.at[pl.ds((4 * px + 2 * py + pc) * m_per, m_per), :]

        def copy(k, block, to, src=None):
            return pltpu.make_async_remote_copy(
                src_ref=rows(*block) if src is None else src, dst_ref=rows(*block),
                send_sem=send_sems.at[k], recv_sem=recv_sems.at[k], device_id=to, device_id_type=MESH)

        mine = pltpu.make_async_copy(x_ref, rows(*me), local_sem)
        mine.start()
        first = [copy(0, me, sibling, src=x_ref)]
        first += [copy(1 + j, me, (*chip, c), src=x_ref) for j, chip in enumerate(chips)]
        for cp in first:
            cp.start()
        passed = [copy(4 + j, (*chip, c), sibling) for j, chip in enumerate(chips)]
        for j, chip in enumerate(chips):
            copy(1 + j, (*chip, c), me).wait_recv()
            passed[j].start()
        copy(0, sibling, me).wait_recv()
        for j, chip in enumerate(chips):
            copy(4 + j, (*chip, 1 - c), me).wait_recv()
        for cp in first + passed:
            cp.wait_send()
        mine.wait()

    return pl.pallas_call(
        body, name="allgather_small",
        out_shape=jax.ShapeDtypeStruct((8 * m_per, n), v.dtype),
        in_specs=[pl.BlockSpec(memory_space=pltpu.VMEM)],
        out_specs=pl.BlockSpec(memory_space=pltpu.VMEM),
        scratch_shapes=[pltpu.SemaphoreType.DMA((7,)), pltpu.SemaphoreType.DMA((7,)), pltpu.SemaphoreType.DMA],
    )(v)


def _pick_rows(r):
    return next(t for t in (512, 256, 128, 64, 32, 16, 8, r) if r % t == 0)


def _sum_slabs(a, name):
    s, r, c = a.shape
    tm = _pick_rows(r)

    def body(a_ref, o_ref):
        acc = a_ref[0].astype(F32)
        for k in range(1, s):
            acc = acc + a_ref[k].astype(F32)
        o_ref[...] = acc

    return pl.pallas_call(
        body, name=name, grid=(r // tm,),
        in_specs=[pl.BlockSpec((s, tm, c), lambda i: (0, i, 0))],
        out_specs=_rows(tm, c),
        out_shape=jax.ShapeDtypeStruct((r, c), F32),
        compiler_params=_params(("parallel",)),
    )(a)


def _add_pair(a, b, name):
    r, c = a.shape
    tm = _pick_rows(r)

    def body(a_ref, b_ref, o_ref):
        o_ref[...] = a_ref[...] + b_ref[...]

    return pl.pallas_call(
        body, name=name, grid=(r // tm,),
        in_specs=[_rows(tm, c), _rows(tm, c)], out_specs=_rows(tm, c),
        out_shape=jax.ShapeDtypeStruct((r, c), F32),
        compiler_params=_params(("parallel",)),
    )(a, b)


def _adamw(w, g, m, v, name):
    r, c = w.shape
    tm = _pick_rows(r)

    def body(w_ref, g_ref, m_ref, v_ref, d_ref, nm_ref, nv_ref):
        gv = g_ref[...]
        nm = ADAM_B1 * m_ref[...] + (1.0 - ADAM_B1) * gv
        nv = ADAM_B2 * v_ref[...] + (1.0 - ADAM_B2) * (gv * gv)
        m_hat = nm / (1.0 - ADAM_B1 ** ADAM_STEP)
        v_hat = nv / (1.0 - ADAM_B2 ** ADAM_STEP)
        d_ref[...] = -ADAM_LR * (m_hat / (jnp.sqrt(v_hat) + ADAM_EPS) + ADAM_WD * w_ref[...])
        nm_ref[...] = nm
        nv_ref[...] = nv

    spec = _rows(tm, c)
    return pl.pallas_call(
        body, name=name, grid=(r // tm,),
        in_specs=[spec] * 4, out_specs=[spec] * 3,
        out_shape=[jax.ShapeDtypeStruct((r, c), F32)] * 3,
        compiler_params=_params(("parallel",)),
    )(w, g, m, v)


def _adamw_small(w, gparts, m, v):
    mrows, n = w.shape

    def body(w_ref, g_ref, m_ref, v_ref, go_ref, d_ref, nm_ref, nv_ref):
        gv = g_ref[0]
        for k in range(1, 8):
            gv = gv + g_ref[k]
        go_ref[...] = gv
        nm = ADAM_B1 * m_ref[...] + (1.0 - ADAM_B1) * gv
        nv = ADAM_B2 * v_ref[...] + (1.0 - ADAM_B2) * (gv * gv)
        m_hat = nm / (1.0 - ADAM_B1 ** ADAM_STEP)
        v_hat = nv / (1.0 - ADAM_B2 ** ADAM_STEP)
        d_ref[...] = -ADAM_LR * (m_hat / (jnp.sqrt(v_hat) + ADAM_EPS) + ADAM_WD * w_ref[...])
        nm_ref[...] = nm
        nv_ref[...] = nv

    return pl.pallas_call(
        body, name="adamw_small",
        out_shape=[jax.ShapeDtypeStruct((mrows, n), F32)] * 4,
    )(w, gparts, m, v)


def _flat(parts):
    nl = parts[0].shape[0]
    return jnp.concatenate([p.reshape(nl, -1) for p in parts], axis=1)


def _unflat(flat, shapes):
    out, off = [], 0
    nl = flat.shape[0]
    for s in shapes:
        n = math.prod(s)
        out.append(flat[:, off:off + n].reshape((nl,) + tuple(s)))
        off += n
    return out


SHARD_AXIS = dict(w_in=1, w_q_up=1, w_kv_up=1, w_branch_a=1, w_branch_b=1, w_o=0, w_ffn_up=1, w_ffn_down=0)


def kernel(x, w_in, b_gate, q_norm_g, k_norm_g, q_a_norm_g, kv_a_norm_g, w_q_up, w_kv_up, w_branch_a, w_branch_b, w_o, w_ffn_up, w_ffn_down, pre_mix_g, post_mix_g, pre_ffn_g, post_ffn_g, loss_target, m_w_in, m_b_gate, m_q_norm_g, m_k_norm_g, m_q_a_norm_g, m_kv_a_norm_g, m_w_q_up, m_w_kv_up, m_w_branch_a, m_w_branch_b, m_w_o, m_w_ffn_up, m_w_ffn_down, m_pre_mix_g, m_post_mix_g, m_pre_ffn_g, m_post_ffn_g, v_w_in, v_b_gate, v_q_norm_g, v_k_norm_g, v_q_a_norm_g, v_kv_a_norm_g, v_w_q_up, v_w_kv_up, v_w_branch_a, v_w_branch_b, v_w_o, v_w_ffn_up, v_w_ffn_down, v_pre_mix_g, v_post_mix_g, v_pre_ffn_g, v_post_ffn_g):
    wts = dict(w_in=w_in, b_gate=b_gate, q_norm_g=q_norm_g, k_norm_g=k_norm_g, q_a_norm_g=q_a_norm_g,
               kv_a_norm_g=kv_a_norm_g, w_q_up=w_q_up, w_kv_up=w_kv_up, w_branch_a=w_branch_a, w_branch_b=w_branch_b,
               w_o=w_o, w_ffn_up=w_ffn_up, w_ffn_down=w_ffn_down, pre_mix_g=pre_mix_g, post_mix_g=post_mix_g,
               pre_ffn_g=pre_ffn_g, post_ffn_g=post_ffn_g)
    mom = dict(w_in=m_w_in, b_gate=m_b_gate, q_norm_g=m_q_norm_g, k_norm_g=m_k_norm_g, q_a_norm_g=m_q_a_norm_g,
               kv_a_norm_g=m_kv_a_norm_g, w_q_up=m_w_q_up, w_kv_up=m_w_kv_up, w_branch_a=m_w_branch_a,
               w_branch_b=m_w_branch_b, w_o=m_w_o, w_ffn_up=m_w_ffn_up, w_ffn_down=m_w_ffn_down, pre_mix_g=m_pre_mix_g,
               post_mix_g=m_post_mix_g, pre_ffn_g=m_pre_ffn_g, post_ffn_g=m_post_ffn_g)
    var = dict(w_in=v_w_in, b_gate=v_b_gate, q_norm_g=v_q_norm_g, k_norm_g=v_k_norm_g, q_a_norm_g=v_q_a_norm_g,
               kv_a_norm_g=v_kv_a_norm_g, w_q_up=v_w_q_up, w_kv_up=v_w_kv_up, w_branch_a=v_w_branch_a,
               w_branch_b=v_w_branch_b, w_o=v_w_o, w_ffn_up=v_w_ffn_up, w_ffn_down=v_w_ffn_down, pre_mix_g=v_pre_mix_g,
               post_mix_g=v_post_mix_g, pre_ffn_g=v_pre_ffn_g, post_ffn_g=v_post_ffn_g)
    nl = w_in.shape[0]
    shard_shapes = [wts[n].shape[1:] for n in BIG]
    per_layer = sum(math.prod(s) for s in shard_shapes)

    mine = _flat([wts[n].astype(BF) for n in BIG]).reshape(-1, 1024)
    gathered = _chip_exchange(mine, False, "gather_weights").reshape(4, nl, per_layer)
    layers = []
    for li in range(nl):
        pieces = [_unflat(gathered[k, li][None], shard_shapes) for k in range(4)]
        full = {n: jnp.concatenate([pieces[k][i][0] for k in range(4)], axis=SHARD_AXIS[n]) for i, n in enumerate(BIG)}
        layers.append(_pad_layer_weights(full))
    smalls = [{n: wts[n][li] for n in SMALL} for li in range(nl)]

    loss_local, dx, grads = _local_step(x[0], loss_target[0], layers, smalls)
    loss = lax.psum(loss_local, ("x", "y", "c"))

    per_chip = []
    for k in range(4):
        rows = []
        for li in range(nl):
            nat = _unpad_layer_grads(grads[li][0])
            rows.append(jnp.concatenate([jnp.split(nat[n], 4, axis=SHARD_AXIS[n])[k].reshape(-1) for n in BIG]))
        per_chip.append(jnp.stack(rows))
    send = jnp.stack(per_chip).reshape(4, -1, 1024)
    got = _chip_exchange(send, True, "scatter_grads")
    part = _sum_slabs(got, "sum_chips")
    other = _sibling_exchange(part, "swap_cores")
    gsum = _add_pair(part, other, "sum_cores").reshape(nl, per_layer)
    g_big = dict(zip(BIG, _unflat(gsum, shard_shapes)))

    small_shapes = [wts[n].shape[1:] for n in SMALL]
    g_loc = _flat([jnp.stack([grads[li][1][n] for li in range(nl)]) for n in SMALL]).reshape(-1, 128)
    g_all = _allgather_small(g_loc).reshape(8, -1, 128)
    pack = lambda d: _flat([d[n] for n in SMALL]).reshape(-1, 128)
    gs, ds, ms, vs = _adamw_small(pack(wts), g_all, pack(mom), pack(var))
    unpack = lambda a: dict(zip(SMALL, _unflat(a.reshape(nl, -1), small_shapes)))
    g_small, d_small, m_small, v_small = unpack(gs), unpack(ds), unpack(ms), unpack(vs)

    out_g, out_d, out_m, out_v = dict(g_small), dict(d_small), dict(m_small), dict(v_small)
    for n in BIG:
        shp = wts[n].shape
        two = lambda a: a.reshape(-1, shp[-1])
        d, nm, nv = _adamw(two(wts[n]), two(g_big[n]), two(mom[n]), two(var[n]), "adamw_" + n)
        out_g[n], out_d[n], out_m[n], out_v[n] = g_big[n], d.reshape(shp), nm.reshape(shp), nv.reshape(shp)

    return (loss, dx[None], *[out_g[n] for n in ORDER], *[out_d[n] for n in ORDER], *[out_m[n] for n in ORDER],
            *[out_v[n] for n in ORDER])
```

```python
import functools
import math

import jax
import jax.numpy as jnp
from jax import lax
from jax.experimental import pallas as pl
from jax.experimental.pallas import tpu as pltpu

F32 = jnp.float32
BF = jnp.bfloat16
MESH = pl.DeviceIdType.MESH

EPS = 1e-6
D = 1024
NH = 8
HW = 128
GQA_KV = 2
GQA_G = 4
HEAD = 64
MLA_NOPE = 64
MLA_ROPE = 32
MLA_QK = 96
Q_RANK = 384
KV_RANK = 256
DFF = 4096
GRID_W = 64
ROPE_THETA = 10000.0

O_QA, O_KA, O_VA, O_CQ, O_CKV, O_KR, ZM_W = 0, 1024, 1280, 1536, 1920, 2176, 2304
ZG_W = 2048

ADAM_LR, ADAM_B1, ADAM_B2, ADAM_EPS, ADAM_WD, ADAM_STEP = 0.001, 0.9, 0.999, 1e-08, 0.01, 10

TM = 256
TQ = 256
ONES_LANE = 64
LOG2E = 1.4426950408889634
VMEM_LIMIT = 56 * 1024 * 1024

BIG = ("w_in", "w_q_up", "w_kv_up", "w_branch_a", "w_branch_b", "w_o", "w_ffn_up", "w_ffn_down")
SMALL = ("b_gate", "q_norm_g", "k_norm_g", "q_a_norm_g", "kv_a_norm_g", "pre_mix_g", "post_mix_g", "pre_ffn_g", "post_ffn_g")
ORDER = ("w_in", "b_gate", "q_norm_g", "k_norm_g", "q_a_norm_g", "kv_a_norm_g", "w_q_up", "w_kv_up", "w_branch_a",
         "w_branch_b", "w_o", "w_ffn_up", "w_ffn_down", "pre_mix_g", "post_mix_g", "pre_ffn_g", "post_ffn_g")


def _params(sem=None):
    return pltpu.CompilerParams(dimension_semantics=sem, vmem_limit_bytes=VMEM_LIMIT)


def _rows(tm, w):
    return pl.BlockSpec((tm, w), lambda i: (i, 0))


def _cols(h, tm):
    return pl.BlockSpec((h, tm), lambda i: (0, i))


def _whole(shape):
    return pl.BlockSpec(shape, lambda i: (0,) * len(shape))


def _dot(a, b):
    return jnp.dot(a, b, preferred_element_type=F32)


def _dot_nt(a, b):
    return lax.dot_general(a, b, (((1,), (1,)), ((), ())), preferred_element_type=F32)


def _dot_tn(a, b):
    return lax.dot_general(a, b, (((0,), (0,)), ((), ())), preferred_element_type=F32)


def _norm_fwd(xv, g, n=None):
    n = xv.shape[-1] if n is None else n
    r = lax.rsqrt(jnp.sum(xv * xv, axis=-1, keepdims=True) * (1.0 / n) + EPS)
    return (xv * r) * g


def _norm_bwd(xv, g, dy, n=None):
    n = xv.shape[-1] if n is None else n
    r = lax.rsqrt(jnp.sum(xv * xv, axis=-1, keepdims=True) * (1.0 / n) + EPS)
    xh = xv * r
    dxh = dy * g
    dg = jnp.sum(dy * xh, axis=0, keepdims=True)
    dx = r * (dxh - xh * (jnp.sum(dxh * xh, axis=-1, keepdims=True) * (1.0 / n)))
    return dx, dg


def _accumulate(ref, val):
    @pl.when(pl.program_id(0) == 0)
    def _():
        ref[...] = jnp.zeros_like(ref)

    ref[...] += val


def _rot(xv, q, neg):
    w = xv.shape[-1]
    return jnp.where(neg, -pltpu.roll(xv, w - q, 1), pltpu.roll(xv, q, 1))


def _rope(xv, c, s, q, neg):
    return xv * c + _rot(xv, q, neg) * s


def _rope_bwd(dy, c, s, q, neg):
    return dy * c - _rot(dy * s, q, neg)


def _neg_mask(tm, q):
    lane = lax.broadcasted_iota(jnp.int32, (tm, HW), 1)
    return (lane % (2 * q)) < q


def _sigmoid(z):
    return 1.0 / (1.0 + jnp.exp(-z))


def _fwd_inproj(x, g1, w1, w2):
    t = x.shape[0]

    def body(x_ref, g_ref, w1_ref, w2_ref, u_ref, zm_ref, zg_ref):
        u = _norm_fwd(x_ref[...], g_ref[...]).astype(BF)
        u_ref[...] = u
        for c in range(0, ZM_W, 768):
            zm_ref[:, c:c + 768] = _dot(u, w1_ref[:, c:c + 768])
        for c in range(0, ZG_W, 512):
            zg_ref[:, c:c + 512] = _dot(u, w2_ref[:, c:c + 512])

    return pl.pallas_call(
        body, name="fwd_inproj", grid=(t // TM,),
        in_specs=[_rows(TM, D), _whole((1, D)), _whole((D, ZM_W)), _whole((D, ZG_W))],
        out_specs=[_rows(TM, D), _rows(TM, ZM_W), _rows(TM, ZG_W)],
        out_shape=[jax.ShapeDtypeStruct((t, D), BF), jax.ShapeDtypeStruct((t, ZM_W), F32),
                   jax.ShapeDtypeStruct((t, ZG_W), F32)],
        compiler_params=_params(("parallel",)),
    )(x, g1, w1, w2)


def _fwd_mid(zm, gq, gk, gqa, gkva, wq, wk, wv, ca, sa, cb, sb):
    t = zm.shape[0]

    def body(zm_ref, gq_ref, gk_ref, gqa_ref, gkva_ref, wq_ref, wk_ref, wv_ref, ca_ref, sa_ref, cb_ref, sb_ref,
             qa_ref, ka_ref, va_ref, qb_ref, kb_ref, vb_ref, cqn_ref, ckvn_ref):
        ca_, sa_, cb_, sb_ = ca_ref[...], sa_ref[...], cb_ref[...], sb_ref[...]
        nega, negb = _neg_mask(TM, 16), _neg_mask(TM, 8)
        for h in range(NH):
            xv = zm_ref[:, O_QA + h * HW:O_QA + (h + 1) * HW]
            qa_ref[:, h * HW:(h + 1) * HW] = _rope(_norm_fwd(xv, gq_ref[...], HEAD), ca_, sa_, 16, nega).astype(BF)
        for h in range(GQA_KV):
            xv = zm_ref[:, O_KA + h * HW:O_KA + (h + 1) * HW]
            ka_ref[:, h * HW:(h + 1) * HW] = _rope(_norm_fwd(xv, gk_ref[...], HEAD), ca_, sa_, 16, nega).astype(BF)
        ones = lax.broadcasted_iota(jnp.int32, (TM, HW), 1) == ONES_LANE
        for h in range(GQA_KV):
            va_ref[:, h * HW:(h + 1) * HW] = jnp.where(ones, 1.0, zm_ref[:, O_VA + h * HW:O_VA + (h + 1) * HW]).astype(BF)
        cqn = _norm_fwd(zm_ref[:, O_CQ:O_CQ + Q_RANK], gqa_ref[...]).astype(BF)
        ckvn = _norm_fwd(zm_ref[:, O_CKV:O_CKV + KV_RANK], gkva_ref[...]).astype(BF)
        cqn_ref[...] = cqn
        ckvn_ref[...] = ckvn
        qb = _dot(cqn, wq_ref[...])
        kpre = _dot(ckvn, wk_ref[...])
        kr = _rope(zm_ref[:, O_KR:O_KR + HW], cb_, sb_, 8, negb)
        for h in range(NH):
            sl = slice(h * HW, (h + 1) * HW)
            qb_ref[:, sl] = _rope(qb[:, sl], cb_, sb_, 8, negb).astype(BF)
            kb_ref[:, sl] = (kpre[:, sl] + kr).astype(BF)
        vb = _dot(ckvn, wv_ref[...])
        for h in range(NH):
            vb_ref[:, h * HW:(h + 1) * HW] = jnp.where(ones, 1.0, vb[:, h * HW:(h + 1) * HW]).astype(BF)

    hw8 = NH * HW
    return pl.pallas_call(
        body, name="fwd_mid", grid=(t // TM,),
        in_specs=[_rows(TM, ZM_W), _whole((1, HW)), _whole((1, HW)), _whole((1, Q_RANK)), _whole((1, KV_RANK)),
                  _whole((Q_RANK, hw8)), _whole((KV_RANK, hw8)), _whole((KV_RANK, hw8)),
                  _rows(TM, HW), _rows(TM, HW), _rows(TM, HW), _rows(TM, HW)],
        out_specs=[_rows(TM, hw8), _rows(TM, GQA_KV * HW), _rows(TM, GQA_KV * HW), _rows(TM, hw8), _rows(TM, hw8),
                   _rows(TM, hw8), _rows(TM, Q_RANK), _rows(TM, KV_RANK)],
        out_shape=[jax.ShapeDtypeStruct((t, hw8), BF), jax.ShapeDtypeStruct((t, GQA_KV * HW), BF),
                   jax.ShapeDtypeStruct((t, GQA_KV * HW), BF), jax.ShapeDtypeStruct((t, hw8), BF),
                   jax.ShapeDtypeStruct((t, hw8), BF), jax.ShapeDtypeStruct((t, hw8), BF),
                   jax.ShapeDtypeStruct((t, Q_RANK), BF), jax.ShapeDtypeStruct((t, KV_RANK), BF)],
        compiler_params=_params(("parallel",)),
    )(zm, gq, gk, gqa, gkva, wq, wk, wv, ca, sa, cb, sb)


def _flash_fwd(q, k, v, group, scale, name):
    t = q.shape[0]
    tq = min(TQ, t)
    c2 = scale * LOG2E

    def body(q_ref, k_ref, v_ref, o_ref, lse_ref):
        s = _dot_nt(q_ref[...], k_ref[...]) * c2
        m = jnp.max(s, axis=-1, keepdims=True)
        acc = _dot(jnp.exp2(s - m).astype(BF), v_ref[...])
        l = acc[:, ONES_LANE:ONES_LANE + 1]
        o_ref[...] = acc / l
        lse_ref[...] = jnp.broadcast_to(m + jnp.log2(l), (tq, HW))

    qspec = pl.BlockSpec((tq, HW), lambda h, i: (i, h))
    kspec = pl.BlockSpec((t, HW), lambda h, i: (0, h // group))
    return pl.pallas_call(
        body, name=name, grid=(NH, t // tq),
        in_specs=[qspec, kspec, kspec],
        out_specs=[qspec, qspec],
        out_shape=[jax.ShapeDtypeStruct((t, NH * HW), F32), jax.ShapeDtypeStruct((t, NH * HW), F32)],
        compiler_params=_params(("parallel", "parallel")),
    )(q, k, v)


def _fwd_merge(ya, yb, zg, b, wa, wb, wo, x, g2):
    t = x.shape[0]

    def body(ya_ref, yb_ref, zg_ref, b_ref, wa_ref, wb_ref, wo_ref, x_ref, g_ref, pa_ref, pb_ref, mg_ref, m_ref, x1_ref):
        pa = _dot(ya_ref[...].astype(BF), wa_ref[...])
        pb = _dot(yb_ref[...].astype(BF), wb_ref[...])
        pa_ref[...] = pa
        pb_ref[...] = pb
        ga = _sigmoid(zg_ref[:, :D] + b_ref[:, :D])
        gb = _sigmoid(zg_ref[:, D:] + b_ref[:, D:])
        mg = (ga * pa + gb * pb).astype(BF)
        mg_ref[...] = mg
        m = _dot(mg, wo_ref[...])
        m_ref[...] = m
        x1_ref[...] = x_ref[...] + _norm_fwd(m, g_ref[...])

    return pl.pallas_call(
        body, name="fwd_merge", grid=(t // TM,),
        in_specs=[_rows(TM, D), _rows(TM, D), _rows(TM, ZG_W), _whole((1, ZG_W)), _whole((D, D)), _whole((D, D)),
                  _whole((D, D)), _rows(TM, D), _whole((1, D))],
        out_specs=[_rows(TM, D), _rows(TM, D), _rows(TM, D), _rows(TM, D), _rows(TM, D)],
        out_shape=[jax.ShapeDtypeStruct((t, D), F32), jax.ShapeDtypeStruct((t, D), F32), jax.ShapeDtypeStruct((t, D), BF),
                   jax.ShapeDtypeStruct((t, D), F32), jax.ShapeDtypeStruct((t, D), F32)],
        compiler_params=_params(("parallel",)),
    )(ya, yb, zg, b, wa, wb, wo, x, g2)


def _fwd_ffn_up(x1, g3, wup):
    t = x1.shape[0]

    def body(x_ref, g_ref, w_ref, hn_ref, h_ref, a_ref):
        hn = _norm_fwd(x_ref[...], g_ref[...]).astype(BF)
        hn_ref[...] = hn
        for c in range(0, DFF, 1024):
            h = _dot(hn, w_ref[:, c:c + 1024])
            h_ref[:, c:c + 1024] = h
            r = jnp.maximum(h, 0.0)
            a_ref[:, c:c + 1024] = (r * r).astype(BF)

    return pl.pallas_call(
        body, name="fwd_ffn_up", grid=(t // TM,),
        in_specs=[_rows(TM, D), _whole((1, D)), _whole((D, DFF))],
        out_specs=[_rows(TM, D), _rows(TM, DFF), _rows(TM, DFF)],
        out_shape=[jax.ShapeDtypeStruct((t, D), BF), jax.ShapeDtypeStruct((t, DFF), F32), jax.ShapeDtypeStruct((t, DFF), BF)],
        compiler_params=_params(("parallel",)),
    )(x1, g3, wup)


def _fwd_ffn_down(a, wdn, x1, g4):
    t = x1.shape[0]

    def body(a_ref, w_ref, x_ref, g_ref, f_ref, x2_ref):
        f = _dot(a_ref[...], w_ref[...])
        f_ref[...] = f
        x2_ref[...] = x_ref[...] + _norm_fwd(f, g_ref[...])

    return pl.pallas_call(
        body, name="fwd_ffn_down", grid=(t // TM,),
        in_specs=[_rows(TM, DFF), _whole((DFF, D)), _rows(TM, D), _whole((1, D))],
        out_specs=[_rows(TM, D), _rows(TM, D)],
        out_shape=[jax.ShapeDtypeStruct((t, D), F32), jax.ShapeDtypeStruct((t, D), F32)],
        compiler_params=_params(("parallel",)),
    )(a, wdn, x1, g4)


def _loss_head(y, target):
    t = y.shape[0]

    def body(y_ref, t_ref, dy_ref, loss_ref):
        d = y_ref[...] - t_ref[...]
        dy_ref[...] = d * (1.0 / D)
        part = 0.5 * jnp.sum(jnp.sum(d * d, axis=-1, keepdims=True) * (1.0 / D), axis=0, keepdims=True)
        _accumulate(loss_ref, jnp.broadcast_to(part, (8, HW)))

    return pl.pallas_call(
        body, name="loss_head", grid=(t // TM,),
        in_specs=[_rows(TM, D), _rows(TM, D)],
        out_specs=[_rows(TM, D), _whole((8, HW))],
        out_shape=[jax.ShapeDtypeStruct((t, D), F32), jax.ShapeDtypeStruct((8, HW), F32)],
        compiler_params=_params(("arbitrary",)),
    )(y, target)


def _bwd_ffn_down(f, dx2, g4, wdn, h):
    t = f.shape[0]

    def body(f_ref, dx2_ref, g_ref, w_ref, h_ref, df_ref, dh_ref, dg_ref):
        df, dg = _norm_bwd(f_ref[...], g_ref[...], dx2_ref[...])
        _accumulate(dg_ref, dg)
        df16 = df.astype(BF)
        df_ref[...] = df16
        for c in range(0, DFF, 1024):
            da = _dot_nt(df16, w_ref[c:c + 1024, :])
            dh_ref[:, c:c + 1024] = (da * (2.0 * jnp.maximum(h_ref[:, c:c + 1024], 0.0))).astype(BF)

    return pl.pallas_call(
        body, name="bwd_ffn_down", grid=(t // TM,),
        in_specs=[_rows(TM, D), _rows(TM, D), _whole((1, D)), _whole((DFF, D)), _rows(TM, DFF)],
        out_specs=[_rows(TM, D), _rows(TM, DFF), _whole((1, D))],
        out_shape=[jax.ShapeDtypeStruct((t, D), BF), jax.ShapeDtypeStruct((t, DFF), BF), jax.ShapeDtypeStruct((1, D), F32)],
        compiler_params=_params(("arbitrary",)),
    )(f, dx2, g4, wdn, h)


def _bwd_ffn_up(dh, wup, x1, g3, dx2):
    t = x1.shape[0]

    def body(dh_ref, w_ref, x_ref, g_ref, dx2_ref, dx1_ref, dg_ref):
        dhn = _dot_nt(dh_ref[...], w_ref[...])
        dx, dg = _norm_bwd(x_ref[...], g_ref[...], dhn)
        _accumulate(dg_ref, dg)
        dx1_ref[...] = dx2_ref[...] + dx

    return pl.pallas_call(
        body, name="bwd_ffn_up", grid=(t // TM,),
        in_specs=[_rows(TM, DFF), _whole((D, DFF)), _rows(TM, D), _whole((1, D)), _rows(TM, D)],
        out_specs=[_rows(TM, D), _whole((1, D))],
        out_shape=[jax.ShapeDtypeStruct((t, D), F32), jax.ShapeDtypeStruct((1, D), F32)],
        compiler_params=_params(("arbitrary",)),
    )(dh, wup, x1, g3, dx2)


def _bwd_merge(m, dx1, g2, wo, zg, b, pa, pb, wa, wb):
    t = m.shape[0]

    def body(m_ref, dx1_ref, g_ref, wo_ref, zg_ref, b_ref, pa_ref, pb_ref, wa_ref, wb_ref,
             dm_ref, dpa_ref, dpb_ref, dzg_ref, dya_ref, dyb_ref, dg_ref, db_ref):
        dm, dg = _norm_bwd(m_ref[...], g_ref[...], dx1_ref[...])
        _accumulate(dg_ref, dg)
        dm16 = dm.astype(BF)
        dm_ref[...] = dm16
        dmg = _dot_nt(dm16, wo_ref[...])
        ga = _sigmoid(zg_ref[:, :D] + b_ref[:, :D])
        gb = _sigmoid(zg_ref[:, D:] + b_ref[:, D:])
        dpa = (dmg * ga).astype(BF)
        dpb = (dmg * gb).astype(BF)
        dpa_ref[...] = dpa
        dpb_ref[...] = dpb
        dza = (dmg * pa_ref[...]) * (ga * (1.0 - ga))
        dzb = (dmg * pb_ref[...]) * (gb * (1.0 - gb))
        dzg_ref[:, :D] = dza.astype(BF)
        dzg_ref[:, D:] = dzb.astype(BF)

        @pl.when(pl.program_id(0) == 0)
        def _():
            db_ref[...] = jnp.zeros_like(db_ref)

        db_ref[:, :D] += jnp.sum(dza, axis=0, keepdims=True)
        db_ref[:, D:] += jnp.sum(dzb, axis=0, keepdims=True)
        dya_ref[...] = _dot_nt(dpa, wa_ref[...]).astype(BF)
        dyb_ref[...] = _dot_nt(dpb, wb_ref[...]).astype(BF)

    return pl.pallas_call(
        body, name="bwd_merge", grid=(t // TM,),
        in_specs=[_rows(TM, D), _rows(TM, D), _whole((1, D)), _whole((D, D)), _rows(TM, ZG_W), _whole((1, ZG_W)),
                  _rows(TM, D), _rows(TM, D), _whole((D, D)), _whole((D, D))],
        out_specs=[_rows(TM, D), _rows(TM, D), _rows(TM, D), _rows(TM, ZG_W), _rows(TM, D), _rows(TM, D),
                   _whole((1, D)), _whole((1, ZG_W))],
        out_shape=[jax.ShapeDtypeStruct((t, D), BF), jax.ShapeDtypeStruct((t, D), BF), jax.ShapeDtypeStruct((t, D), BF),
                   jax.ShapeDtypeStruct((t, ZG_W), BF), jax.ShapeDtypeStruct((t, D), BF), jax.ShapeDtypeStruct((t, D), BF),
                   jax.ShapeDtypeStruct((1, D), F32), jax.ShapeDtypeStruct((1, ZG_W), F32)],
        compiler_params=_params(("arbitrary",)),
    )(m, dx1, g2, wo, zg, b, pa, pb, wa, wb)


def _flash_bwd(q, do, o, lse, k, v, group, scale, name):
    t = q.shape[0]
    tq = min(TQ, t)
    c2 = scale * LOG2E
    nkv = NH // group

    def body(q_ref, do_ref, o_ref, lse_ref, k_ref, v_ref, dq_ref, dkt_ref, dvt_ref):
        @pl.when((pl.program_id(1) == 0) & (pl.program_id(2) == 0))
        def _():
            dkt_ref[...] = jnp.zeros_like(dkt_ref)
            dvt_ref[...] = jnp.zeros_like(dvt_ref)

        qv = q_ref[...]
        dov = do_ref[...]
        dsum = jnp.sum(dov.astype(F32) * o_ref[...], axis=-1, keepdims=True)
        p = jnp.exp2(_dot_nt(qv, k_ref[...]) * c2 - lse_ref[:, :1])
        ds = (p * (_dot_nt(dov, v_ref[...]) - dsum)).astype(BF)
        dq_ref[...] = _dot(ds, k_ref[...]) * scale
        dvt_ref[...] += _dot(dov.astype(F32).T.astype(BF), p.astype(BF))
        dkt_ref[...] += _dot(qv.astype(F32).T.astype(BF), ds) * scale

    qspec = pl.BlockSpec((tq, HW), lambda hk, g, i: (i, hk * group + g))
    kspec = pl.BlockSpec((t, HW), lambda hk, g, i: (0, hk))
    tspec = pl.BlockSpec((HW, t), lambda hk, g, i: (hk, 0))
    return pl.pallas_call(
        body, name=name, grid=(nkv, group, t // tq),
        in_specs=[qspec, qspec, qspec, qspec, kspec, kspec],
        out_specs=[qspec, tspec, tspec],
        out_shape=[jax.ShapeDtypeStruct((t, NH * HW), F32), jax.ShapeDtypeStruct((nkv * HW, t), F32),
                   jax.ShapeDtypeStruct((nkv * HW, t), F32)],
        compiler_params=_params(("parallel", "arbitrary", "arbitrary")),
    )(q, do, o, lse, k, v)


def _bwd_mid(zm, dqa, dkta, dvta, dqb, dktb, dvtb, gq, gk, gqa, gkva, wq, wk, wv, ca, sa, cb, sb):
    t = zm.shape[0]
    hw8 = NH * HW

    def body(zm_ref, dqa_ref, dkta_ref, dvta_ref, dqb_ref, dktb_ref, dvtb_ref, gq_ref, gk_ref, gqa_ref, gkva_ref,
             wq_ref, wk_ref, wv_ref, ca_ref, sa_ref, cb_ref, sb_ref,
             dzm_ref, dqbp_ref, dkb16_ref, dvb16_ref, dgq_ref, dgk_ref, dgqa_ref, dgkva_ref):
        ca_, sa_, cb_, sb_ = ca_ref[...], sa_ref[...], cb_ref[...], sb_ref[...]
        nega, negb = _neg_mask(TM, 16), _neg_mask(TM, 8)
        dgq = jnp.zeros((1, HW), F32)
        for h in range(NH):
            sl = slice(h * HW, (h + 1) * HW)
            dqn = _rope_bwd(dqa_ref[:, sl], ca_, sa_, 16, nega)
            dx, dg = _norm_bwd(zm_ref[:, O_QA + h * HW:O_QA + (h + 1) * HW], gq_ref[...], dqn, HEAD)
            dzm_ref[:, O_QA + h * HW:O_QA + (h + 1) * HW] = dx.astype(BF)
            dgq = dgq + dg
        _accumulate(dgq_ref, dgq)
        dgk = jnp.zeros((1, HW), F32)
        for j in range(GQA_KV):
            dk = dkta_ref[j * HW:(j + 1) * HW, :].T
            dv = dvta_ref[j * HW:(j + 1) * HW, :].T
            dkn = _rope_bwd(dk, ca_, sa_, 16, nega)
            dx, dg = _norm_bwd(zm_ref[:, O_KA + j * HW:O_KA + (j + 1) * HW], gk_ref[...], dkn, HEAD)
            dzm_ref[:, O_KA + j * HW:O_KA + (j + 1) * HW] = dx.astype(BF)
            dzm_ref[:, O_VA + j * HW:O_VA + (j + 1) * HW] = dv.astype(BF)
            dgk = dgk + dg
        _accumulate(dgk_ref, dgk)
        for h in range(NH):
            sl = slice(h * HW, (h + 1) * HW)
            dqbp_ref[:, sl] = _rope_bwd(dqb_ref[:, sl], cb_, sb_, 8, negb).astype(BF)
        dcqn = _dot_nt(dqbp_ref[...], wq_ref[...])
        dx, dg = _norm_bwd(zm_ref[:, O_CQ:O_CQ + Q_RANK], gqa_ref[...], dcqn)
        dzm_ref[:, O_CQ:O_CQ + Q_RANK] = dx.astype(BF)
        _accumulate(dgqa_ref, dg)
        dkr = jnp.zeros((TM, HW), F32)
        for h in range(NH):
            sl = slice(h * HW, (h + 1) * HW)
            dkh = dktb_ref[sl, :].T
            dkr = dkr + dkh
            dkb16_ref[:, sl] = dkh.astype(BF)
            dvb16_ref[:, sl] = dvtb_ref[sl, :].T.astype(BF)
        dkb16 = dkb16_ref[...]
        dvb16 = dvb16_ref[...]
        lane = lax.broadcasted_iota(jnp.int32, (TM, HW), 1)
        in_rope = (lane >= MLA_NOPE) & (lane < MLA_QK)
        dzm_ref[:, O_KR:O_KR + HW] = jnp.where(in_rope, _rope_bwd(dkr, cb_, sb_, 8, negb), 0.0).astype(BF)
        dckvn = _dot_nt(dkb16, wk_ref[...]) + _dot_nt(dvb16, wv_ref[...])
        dx, dg = _norm_bwd(zm_ref[:, O_CKV:O_CKV + KV_RANK], gkva_ref[...], dckvn)
        dzm_ref[:, O_CKV:O_CKV + KV_RANK] = dx.astype(BF)
        _accumulate(dgkva_ref, dg)

    return pl.pallas_call(
        body, name="bwd_mid", grid=(t // TM,),
        in_specs=[_rows(TM, ZM_W), _rows(TM, hw8), _cols(GQA_KV * HW, TM), _cols(GQA_KV * HW, TM), _rows(TM, hw8),
                  _cols(hw8, TM), _cols(hw8, TM), _whole((1, HW)), _whole((1, HW)), _whole((1, Q_RANK)), _whole((1, KV_RANK)),
                  _whole((Q_RANK, hw8)), _whole((KV_RANK, hw8)), _whole((KV_RANK, hw8)),
                  _rows(TM, HW), _rows(TM, HW), _rows(TM, HW), _rows(TM, HW)],
        out_specs=[_rows(TM, ZM_W), _rows(TM, hw8), _rows(TM, hw8), _rows(TM, hw8),
                   _whole((1, HW)), _whole((1, HW)), _whole((1, Q_RANK)), _whole((1, KV_RANK))],
        out_shape=[jax.ShapeDtypeStruct((t, ZM_W), BF), jax.ShapeDtypeStruct((t, hw8), BF), jax.ShapeDtypeStruct((t, hw8), BF),
                   jax.ShapeDtypeStruct((t, hw8), BF), jax.ShapeDtypeStruct((1, HW), F32), jax.ShapeDtypeStruct((1, HW), F32),
                   jax.ShapeDtypeStruct((1, Q_RANK), F32), jax.ShapeDtypeStruct((1, KV_RANK), F32)],
        compiler_params=_params(("arbitrary",)),
    )(zm, dqa, dkta, dvta, dqb, dktb, dvtb, gq, gk, gqa, gkva, wq, wk, wv, ca, sa, cb, sb)


def _bwd_inproj(dzm, dzg, w1, w2, x, g1, dx1):
    t = x.shape[0]

    def body(dzm_ref, dzg_ref, w1_ref, w2_ref, x_ref, g_ref, dx1_ref, dx_ref, dg_ref):
        du = _dot_nt(dzm_ref[...], w1_ref[...]) + _dot_nt(dzg_ref[...], w2_ref[...])
        dx, dg = _norm_bwd(x_ref[...], g_ref[...], du)
        _accumulate(dg_ref, dg)
        dx_ref[...] = dx1_ref[...] + dx

    return pl.pallas_call(
        body, name="bwd_inproj", grid=(t // TM,),
        in_specs=[_rows(TM, ZM_W), _rows(TM, ZG_W), _whole((D, ZM_W)), _whole((D, ZG_W)), _rows(TM, D), _whole((1, D)),
                  _rows(TM, D)],
        out_specs=[_rows(TM, D), _whole((1, D))],
        out_shape=[jax.ShapeDtypeStruct((t, D), F32), jax.ShapeDtypeStruct((1, D), F32)],
        compiler_params=_params(("arbitrary",)),
    )(dzm, dzg, w1, w2, x, g1, dx1)


def _matmul_tn(a, b, tn, name):
    t, kdim = a.shape
    n = b.shape[1]
    tm = min(512, t)
    nsteps = t // tm

    def body(a_ref, b_ref, o_ref, acc_ref):
        i = pl.program_id(1)

        @pl.when(i == 0)
        def _():
            acc_ref[...] = jnp.zeros_like(acc_ref)

        acc_ref[...] += _dot_tn(a_ref[...].astype(BF), b_ref[...])

        @pl.when(i == nsteps - 1)
        def _():
            o_ref[...] = acc_ref[...].astype(BF)

    return pl.pallas_call(
        body, name=name, grid=(n // tn, nsteps),
        in_specs=[pl.BlockSpec((tm, kdim), lambda j, i: (i, 0)), pl.BlockSpec((tm, tn), lambda j, i: (i, j))],
        out_specs=pl.BlockSpec((kdim, tn), lambda j, i: (0, j)),
        out_shape=jax.ShapeDtypeStruct((kdim, n), BF),
        scratch_shapes=[pltpu.VMEM((kdim, tn), F32)],
        compiler_params=_params(("parallel", "arbitrary")),
    )(a, b)


def _rope_tables(t):
    pos = jnp.arange(t, dtype=jnp.int32)
    row = (pos // GRID_W).astype(F32)
    col = (pos % GRID_W).astype(F32)

    def table(rot_dim):
        half = rot_dim // 2
        inv = ROPE_THETA ** (-jnp.arange(0, half, 2, dtype=F32) / half)
        ar = row[:, None] * inv[None, :]
        ac = col[:, None] * inv[None, :]
        ang = jnp.concatenate([ar, ar, ac, ac], axis=-1)
        return jnp.cos(ang), jnp.sin(ang)

    c64, s64 = table(HEAD)
    c32, s32 = table(MLA_ROPE)
    ones = lambda w: jnp.ones((t, w), F32)
    zeros = lambda w: jnp.zeros((t, w), F32)
    ca = jnp.concatenate([c64, ones(HW - HEAD)], axis=1)
    sa = jnp.concatenate([s64, zeros(HW - HEAD)], axis=1)
    cb = jnp.concatenate([ones(MLA_NOPE), c32, ones(HW - MLA_QK)], axis=1)
    sb = jnp.concatenate([zeros(MLA_NOPE), s32, zeros(HW - MLA_QK)], axis=1)
    return ca, sa, cb, sb


def _pad_heads_cols(w, heads, width):
    k = w.shape[0]
    w = w.reshape(k, heads, width)
    return jnp.pad(w, ((0, 0), (0, 0), (0, HW - width))).reshape(k, heads * HW)


def _pad_heads_rows(w, heads, width):
    n = w.shape[1]
    w = w.reshape(heads, width, n)
    return jnp.pad(w, ((0, 0), (0, HW - width), (0, 0))).reshape(heads * HW, n)


def _unpad_heads_cols(w, heads, width):
    k = w.shape[0]
    return w.reshape(k, heads, HW)[:, :, :width].reshape(k, heads * width)


def _unpad_heads_rows(w, heads, width):
    n = w.shape[1]
    return w.reshape(heads, HW, n)[:, :width, :].reshape(heads * width, n)


def _pad_layer_weights(w):
    w_in = w["w_in"]
    qa = _pad_heads_cols(w_in[:, 0:512], NH, HEAD)
    ka = _pad_heads_cols(w_in[:, 512:640], GQA_KV, HEAD)
    va = _pad_heads_cols(w_in[:, 640:768], GQA_KV, HEAD)
    cq = w_in[:, 768:1152]
    ckv = w_in[:, 1152:1408]
    kr = jnp.pad(w_in[:, 1408:1440], ((0, 0), (MLA_NOPE, HW - MLA_QK)))
    kvu = w["w_kv_up"].reshape(KV_RANK, NH, 2 * HEAD)
    return dict(
        w1=jnp.concatenate([qa, ka, va, cq, ckv, kr], axis=1),
        w2=w_in[:, 1440:],
        wq=_pad_heads_cols(w["w_q_up"], NH, MLA_QK),
        wk=jnp.pad(kvu[:, :, :HEAD], ((0, 0), (0, 0), (0, HEAD))).reshape(KV_RANK, NH * HW),
        wv=jnp.pad(kvu[:, :, HEAD:], ((0, 0), (0, 0), (0, HEAD))).reshape(KV_RANK, NH * HW),
        wa=_pad_heads_rows(w["w_branch_a"], NH, HEAD),
        wb=_pad_heads_rows(w["w_branch_b"], NH, HEAD),
        wo=w["w_o"], wup=w["w_ffn_up"], wdn=w["w_ffn_down"],
    )


def _unpad_layer_grads(g):
    d1 = g["w1"]
    w_in = jnp.concatenate([
        _unpad_heads_cols(d1[:, O_QA:O_KA], NH, HEAD), _unpad_heads_cols(d1[:, O_KA:O_VA], GQA_KV, HEAD),
        _unpad_heads_cols(d1[:, O_VA:O_CQ], GQA_KV, HEAD), d1[:, O_CQ:O_CKV], d1[:, O_CKV:O_KR],
        d1[:, O_KR + MLA_NOPE:O_KR + MLA_QK], g["w2"]], axis=1)
    dk = g["wk"].reshape(KV_RANK, NH, HW)[:, :, :HEAD]
    dv = g["wv"].reshape(KV_RANK, NH, HW)[:, :, :HEAD]
    return dict(
        w_in=w_in,
        w_q_up=_unpad_heads_cols(g["wq"], NH, MLA_QK),
        w_kv_up=jnp.concatenate([dk, dv], axis=2).reshape(KV_RANK, NH * 2 * HEAD),
        w_branch_a=_unpad_heads_rows(g["wa"], NH, HEAD),
        w_branch_b=_unpad_heads_rows(g["wb"], NH, HEAD),
        w_o=g["wo"], w_ffn_up=g["wup"], w_ffn_down=g["wdn"],
    )


def _pad_lanes(v, width):
    return jnp.pad(v, (0, HW - width)).reshape(1, HW)


def _local_step(x, target, layers, smalls):
    t = x.shape[0]
    ca, sa, cb, sb = _rope_tables(t)
    sc_a = 1.0 / math.sqrt(HEAD)
    sc_b = 1.0 / math.sqrt(MLA_QK)
    saved = []
    for w, s in zip(layers, smalls):
        gq, gk = _pad_lanes(s["q_norm_g"], HEAD), _pad_lanes(s["k_norm_g"], HEAD)
        gqa, gkva = s["q_a_norm_g"].reshape(1, -1), s["kv_a_norm_g"].reshape(1, -1)
        g1, g2, g3, g4 = (s[n].reshape(1, D) for n in ("pre_mix_g", "post_mix_g", "pre_ffn_g", "post_ffn_g"))
        b = s["b_gate"].reshape(1, ZG_W)
        u, zm, zg = _fwd_inproj(x, g1, w["w1"], w["w2"])
        qa, ka, va, qb, kb, vb, cqn, ckvn = _fwd_mid(zm, gq, gk, gqa, gkva, w["wq"], w["wk"], w["wv"], ca, sa, cb, sb)
        ya, lse_a = _flash_fwd(qa, ka, va, GQA_G, sc_a, "flash_fwd_gqa")
        yb, lse_b = _flash_fwd(qb, kb, vb, 1, sc_b, "flash_fwd_mla")
        pa, pb, mg, m, x1 = _fwd_merge(ya, yb, zg, b, w["wa"], w["wb"], w["wo"], x, g2)
        hn, h, a = _fwd_ffn_up(x1, g3, w["wup"])
        f, x2 = _fwd_ffn_down(a, w["wdn"], x1, g4)
        saved.append(dict(x=x, u=u, zm=zm, zg=zg, qa=qa, ka=ka, va=va, qb=qb, kb=kb, vb=vb, cqn=cqn, ckvn=ckvn,
                          ya=ya, lse_a=lse_a, yb=yb, lse_b=lse_b, pa=pa, pb=pb, mg=mg,
                          m=m, x1=x1, hn=hn, h=h, a=a, f=f, gq=gq, gk=gk, gqa=gqa, gkva=gkva, g1=g1, g2=g2, g3=g3,
                          g4=g4, b=b))
        x = x2

    dx, loss8 = _loss_head(x, target)
    loss = loss8[0, 0]

    grads = [None] * len(layers)
    for li in range(len(layers) - 1, -1, -1):
        w, r = layers[li], saved[li]
        df, dh, dg4 = _bwd_ffn_down(r["f"], dx, r["g4"], w["wdn"], r["h"])
        dx1, dg3 = _bwd_ffn_up(dh, w["wup"], r["x1"], r["g3"], dx)
        dm, dpa, dpb, dzg, dya, dyb, dg2, db = _bwd_merge(r["m"], dx1, r["g2"], w["wo"], r["zg"], r["b"], r["pa"], r["pb"],
                                                         w["wa"], w["wb"])
        dqa, dkta, dvta = _flash_bwd(r["qa"], dya, r["ya"], r["lse_a"], r["ka"], r["va"], GQA_G, sc_a, "flash_bwd_gqa")
        dqb, dktb, dvtb = _flash_bwd(r["qb"], dyb, r["yb"], r["lse_b"], r["kb"], r["vb"], 1, sc_b, "flash_bwd_mla")
        dzm, dqbp, dkb16, dvb16, dgq, dgk, dgqa, dgkva = _bwd_mid(
            r["zm"], dqa, dkta, dvta, dqb, dktb, dvtb, r["gq"], r["gk"], r["gqa"], r["gkva"], w["wq"], w["wk"], w["wv"],
            ca, sa, cb, sb)
        dx, dg1 = _bwd_inproj(dzm, dzg, w["w1"], w["w2"], r["x"], r["g1"], dx1)
        big = dict(
            w1=_matmul_tn(r["u"], dzm, 768, "dw_in_main"), w2=_matmul_tn(r["u"], dzg, 512, "dw_in_gate"),
            wq=_matmul_tn(r["cqn"], dqbp, 512, "dw_q_up"), wk=_matmul_tn(r["ckvn"], dkb16, 512, "dw_k_up"),
            wv=_matmul_tn(r["ckvn"], dvb16, 512, "dw_v_up"), wa=_matmul_tn(r["ya"], dpa, 512, "dw_branch_a"),
            wb=_matmul_tn(r["yb"], dpb, 512, "dw_branch_b"), wo=_matmul_tn(r["mg"], dm, 512, "dw_o"),
            wup=_matmul_tn(r["hn"], dh, 512, "dw_ffn_up"), wdn=_matmul_tn(r["a"], df, 512, "dw_ffn_down"))
        small = dict(b_gate=db[0], q_norm_g=dgq[0, :HEAD], k_norm_g=dgk[0, :HEAD], q_a_norm_g=dgqa[0],
                     kv_a_norm_g=dgkva[0], pre_mix_g=dg1[0], post_mix_g=dg2[0], pre_ffn_g=dg3[0], post_ffn_g=dg4[0])
        grads[li] = (big, small)
    return loss, dx, grads


def _chip_exchange(srcs, per_dest, name):
    n = len(srcs)
    shapes = [s.shape[1:] if per_dest else s.shape for s in srcs]

    def body(*refs):
        src_refs, out_refs = refs[:n], refs[n:2 * n]
        send_sems, recv_sems, local_sems = refs[2 * n:]
        x, y, c = lax.axis_index("x"), lax.axis_index("y"), lax.axis_index("c")
        me = 2 * x + y
        chips = [(1 - x, y), (x, 1 - y), (1 - x, 1 - y)]

        def piece(a, k):
            return src_refs[a].at[k] if per_dest else src_refs[a]

        def remote(a, j, src_chip, dst_slab):
            px, py = chips[j]
            return pltpu.make_async_remote_copy(
                src_ref=piece(a, src_chip), dst_ref=out_refs[a].at[dst_slab], send_sem=send_sems.at[3 * a + j],
                recv_sem=recv_sems.at[3 * a + j], device_id=(px, py, c), device_id_type=MESH)

        local = [pltpu.make_async_copy(piece(a, me), out_refs[a].at[me], local_sems.at[a]) for a in range(n)]
        for cp in local:
            cp.start()
        sends = [remote(a, j, 2 * chips[j][0] + chips[j][1], me) for a in range(n) for j in range(3)]
        for cp in sends:
            cp.start()
        for a in range(n):
            for j in range(3):
                remote(a, j, me, 2 * chips[j][0] + chips[j][1]).wait_recv()
        for cp in sends:
            cp.wait_send()
        for cp in local:
            cp.wait()

    return pl.pallas_call(
        body, name=name,
        in_specs=[pl.BlockSpec(memory_space=pl.ANY)] * n,
        out_specs=[pl.BlockSpec(memory_space=pl.ANY)] * n,
        out_shape=[jax.ShapeDtypeStruct((4,) + tuple(sh), s.dtype) for sh, s in zip(shapes, srcs)],
        scratch_shapes=[pltpu.SemaphoreType.DMA((3 * n,)), pltpu.SemaphoreType.DMA((3 * n,)), pltpu.SemaphoreType.DMA((n,))],
    )(*srcs)


def _sibling_exchange(srcs, name):
    n = len(srcs)

    def body(*refs):
        src_refs, out_refs = refs[:n], refs[n:2 * n]
        send_sems, recv_sems = refs[2 * n:]
        x, y, c = lax.axis_index("x"), lax.axis_index("y"), lax.axis_index("c")
        cps = [pltpu.make_async_remote_copy(src_ref=src_refs[a], dst_ref=out_refs[a], send_sem=send_sems.at[a],
                                            recv_sem=recv_sems.at[a], device_id=(x, y, 1 - c), device_id_type=MESH)
               for a in range(n)]
        for cp in cps:
            cp.start()
        for cp in cps:
            cp.wait()

    return pl.pallas_call(
        body, name=name,
        in_specs=[pl.BlockSpec(memory_space=pl.ANY)] * n,
        out_specs=[pl.BlockSpec(memory_space=pl.ANY)] * n,
        out_shape=[jax.ShapeDtypeStruct(s.shape, s.dtype) for s in srcs],
        scratch_shapes=[pltpu.SemaphoreType.DMA((n,)), pltpu.SemaphoreType.DMA((n,))],
    )(*srcs)


def _allgather_small(v):
    m_per, n = v.shape

    def body(x_ref, out_ref, send_sems, recv_sems, local_sem):
        x, y, c = lax.axis_index("x"), lax.axis_index("y"), lax.axis_index("c")
        me, sibling = (x, y, c), (x, y, 1 - c)
        chips = [(1 - x, y), (x, 1 - y), (1 - x, 1 - y)]

        def rows(px, py, pc):
            return out_ref.at[pl.ds((4 * px + 2 * py + pc) * m_per, m_per), :]

        def copy(k, block, to, src=None):
            return pltpu.make_async_remote_copy(
                src_ref=rows(*block) if src is None else src, dst_ref=rows(*block),
                send_sem=send_sems.at[k], recv_sem=recv_sems.at[k], device_id=to, device_id_type=MESH)

        mine = pltpu.make_async_copy(x_ref, rows(*me), local_sem)
        mine.start()
        first = [copy(0, me, sibling, src=x_ref)]
        first += [copy(1 + j, me, (*chip, c), src=x_ref) for j, chip in enumerate(chips)]
        for cp in first:
            cp.start()
        passed = [copy(4 + j, (*chip, c), sibling) for j, chip in enumerate(chips)]
        for j, chip in enumerate(chips):
            copy(1 + j, (*chip, c), me).wait_recv()
            passed[j].start()
        copy(0, sibling, me).wait_recv()
        for j, chip in enumerate(chips):
            copy(4 + j, (*chip, 1 - c), me).wait_recv()
        for cp in first + passed:
            cp.wait_send()
        mine.wait()

    return pl.pallas_call(
        body, name="allgather_small",
        out_shape=jax.ShapeDtypeStruct((8 * m_per, n), v.dtype),
        in_specs=[pl.BlockSpec(memory_space=pltpu.VMEM)],
        out_specs=pl.BlockSpec(memory_space=pltpu.VMEM),
        scratch_shapes=[pltpu.SemaphoreType.DMA((7,)), pltpu.SemaphoreType.DMA((7,)), pltpu.SemaphoreType.DMA],
    )(v)


def _pick_rows(r):
    return next(t for t in (512, 256, 128, 64, 32, 16, 8, r) if r % t == 0)


def _sum_slabs(a, name):
    s, r, c = a.shape
    tm = _pick_rows(r)

    def body(a_ref, o_ref):
        acc = a_ref[0].astype(F32)
        for k in range(1, s):
            acc = acc + a_ref[k].astype(F32)
        o_ref[...] = acc

    return pl.pallas_call(
        body, name=name, grid=(r // tm,),
        in_specs=[pl.BlockSpec((s, tm, c), lambda i: (0, i, 0))],
        out_specs=_rows(tm, c),
        out_shape=jax.ShapeDtypeStruct((r, c), F32),
        compiler_params=_params(("parallel",)),
    )(a)


def _adamw(w, ga, gb, m, v, name):
    r, c = w.shape
    tm = min(256, _pick_rows(r))

    def body(w_ref, ga_ref, gb_ref, m_ref, v_ref, g_ref, d_ref, nm_ref, nv_ref):
        gv = ga_ref[...] + gb_ref[...]
        g_ref[...] = gv
        nm = ADAM_B1 * m_ref[...] + (1.0 - ADAM_B1) * gv
        nv = ADAM_B2 * v_ref[...] + (1.0 - ADAM_B2) * (gv * gv)
        m_hat = nm / (1.0 - ADAM_B1 ** ADAM_STEP)
        v_hat = nv / (1.0 - ADAM_B2 ** ADAM_STEP)
        d_ref[...] = -ADAM_LR * (m_hat / (jnp.sqrt(v_hat) + ADAM_EPS) + ADAM_WD * w_ref[...])
        nm_ref[...] = nm
        nv_ref[...] = nv

    spec = _rows(tm, c)
    return pl.pallas_call(
        body, name=name, grid=(r // tm,),
        in_specs=[spec] * 5, out_specs=[spec] * 4,
        out_shape=[jax.ShapeDtypeStruct((r, c), F32)] * 4,
        compiler_params=_params(("parallel",)),
    )(w, ga, gb, m, v)


def _adamw_small(w, gparts, m, v):
    mrows, n = w.shape

    def body(w_ref, g_ref, m_ref, v_ref, go_ref, d_ref, nm_ref, nv_ref):
        gv = g_ref[0]
        for k in range(1, 8):
            gv = gv + g_ref[k]
        go_ref[...] = gv
        nm = ADAM_B1 * m_ref[...] + (1.0 - ADAM_B1) * gv
        nv = ADAM_B2 * v_ref[...] + (1.0 - ADAM_B2) * (gv * gv)
        m_hat = nm / (1.0 - ADAM_B1 ** ADAM_STEP)
        v_hat = nv / (1.0 - ADAM_B2 ** ADAM_STEP)
        d_ref[...] = -ADAM_LR * (m_hat / (jnp.sqrt(v_hat) + ADAM_EPS) + ADAM_WD * w_ref[...])
        nm_ref[...] = nm
        nv_ref[...] = nv

    return pl.pallas_call(
        body, name="adamw_small",
        out_shape=[jax.ShapeDtypeStruct((mrows, n), F32)] * 4,
    )(w, gparts, m, v)


def _flat(parts):
    nl = parts[0].shape[0]
    return jnp.concatenate([p.reshape(nl, -1) for p in parts], axis=1)


def _unflat(flat, shapes):
    out, off = [], 0
    nl = flat.shape[0]
    for s in shapes:
        n = math.prod(s)
        out.append(flat[:, off:off + n].reshape((nl,) + tuple(s)))
        off += n
    return out


SHARD_AXIS = dict(w_in=1, w_q_up=1, w_kv_up=1, w_branch_a=1, w_branch_b=1, w_o=0, w_ffn_up=1, w_ffn_down=0)


def kernel(x, w_in, b_gate, q_norm_g, k_norm_g, q_a_norm_g, kv_a_norm_g, w_q_up, w_kv_up, w_branch_a, w_branch_b, w_o, w_ffn_up, w_ffn_down, pre_mix_g, post_mix_g, pre_ffn_g, post_ffn_g, loss_target, m_w_in, m_b_gate, m_q_norm_g, m_k_norm_g, m_q_a_norm_g, m_kv_a_norm_g, m_w_q_up, m_w_kv_up, m_w_branch_a, m_w_branch_b, m_w_o, m_w_ffn_up, m_w_ffn_down, m_pre_mix_g, m_post_mix_g, m_pre_ffn_g, m_post_ffn_g, v_w_in, v_b_gate, v_q_norm_g, v_k_norm_g, v_q_a_norm_g, v_kv_a_norm_g, v_w_q_up, v_w_kv_up, v_w_branch_a, v_w_branch_b, v_w_o, v_w_ffn_up, v_w_ffn_down, v_pre_mix_g, v_post_mix_g, v_pre_ffn_g, v_post_ffn_g):
    wts = dict(w_in=w_in, b_gate=b_gate, q_norm_g=q_norm_g, k_norm_g=k_norm_g, q_a_norm_g=q_a_norm_g,
               kv_a_norm_g=kv_a_norm_g, w_q_up=w_q_up, w_kv_up=w_kv_up, w_branch_a=w_branch_a, w_branch_b=w_branch_b,
               w_o=w_o, w_ffn_up=w_ffn_up, w_ffn_down=w_ffn_down, pre_mix_g=pre_mix_g, post_mix_g=post_mix_g,
               pre_ffn_g=pre_ffn_g, post_ffn_g=post_ffn_g)
    mom = dict(w_in=m_w_in, b_gate=m_b_gate, q_norm_g=m_q_norm_g, k_norm_g=m_k_norm_g, q_a_norm_g=m_q_a_norm_g,
               kv_a_norm_g=m_kv_a_norm_g, w_q_up=m_w_q_up, w_kv_up=m_w_kv_up, w_branch_a=m_w_branch_a,
               w_branch_b=m_w_branch_b, w_o=m_w_o, w_ffn_up=m_w_ffn_up, w_ffn_down=m_w_ffn_down, pre_mix_g=m_pre_mix_g,
               post_mix_g=m_post_mix_g, pre_ffn_g=m_pre_ffn_g, post_ffn_g=m_post_ffn_g)
    var = dict(w_in=v_w_in, b_gate=v_b_gate, q_norm_g=v_q_norm_g, k_norm_g=v_k_norm_g, q_a_norm_g=v_q_a_norm_g,
               kv_a_norm_g=v_kv_a_norm_g, w_q_up=v_w_q_up, w_kv_up=v_w_kv_up, w_branch_a=v_w_branch_a,
               w_branch_b=v_w_branch_b, w_o=v_w_o, w_ffn_up=v_w_ffn_up, w_ffn_down=v_w_ffn_down, pre_mix_g=v_pre_mix_g,
               post_mix_g=v_post_mix_g, pre_ffn_g=v_pre_ffn_g, post_ffn_g=v_post_ffn_g)
    nl = w_in.shape[0]

    def join(parts, name):
        if SHARD_AXIS[name] == 0:
            return parts.reshape(-1, parts.shape[-1])
        return jnp.transpose(parts, (1, 0, 2)).reshape(parts.shape[1], -1)

    def split(full, name):
        if SHARD_AXIS[name] == 0:
            return full.reshape(4, -1, full.shape[-1])
        return jnp.transpose(full.reshape(full.shape[0], 4, -1), (1, 0, 2))

    gathered = dict(zip(BIG, _chip_exchange([wts[n].astype(BF) for n in BIG], False, "gather_weights")))
    layers = [_pad_layer_weights({n: join(gathered[n][:, li], n) for n in BIG}) for li in range(nl)]
    smalls = [{n: wts[n][li] for n in SMALL} for li in range(nl)]

    loss_local, dx, grads = _local_step(x[0], loss_target[0], layers, smalls)
    loss = lax.psum(loss_local, ("x", "y", "c"))

    nat = [_unpad_layer_grads(grads[li][0]) for li in range(nl)]
    send = [jnp.stack([split(nat[li][n], n) for li in range(nl)], axis=1) for n in BIG]
    got = _chip_exchange(send, True, "scatter_grads")
    part = [_sum_slabs(g.reshape(4, -1, g.shape[-1]), "sum_chips_" + n) for g, n in zip(got, BIG)]
    other = _sibling_exchange(part, "swap_cores")

    small_shapes = [wts[n].shape[1:] for n in SMALL]
    g_loc = _flat([jnp.stack([grads[li][1][n] for li in range(nl)]) for n in SMALL]).reshape(-1, 128)
    g_all = _allgather_small(g_loc).reshape(8, -1, 128)
    pack = lambda d: _flat([d[n] for n in SMALL]).reshape(-1, 128)
    gs, ds, ms, vs = _adamw_small(pack(wts), g_all, pack(mom), pack(var))
    unpack = lambda a: dict(zip(SMALL, _unflat(a.reshape(nl, -1), small_shapes)))
    g_small, d_small, m_small, v_small = unpack(gs), unpack(ds), unpack(ms), unpack(vs)

    out_g, out_d, out_m, out_v = dict(g_small), dict(d_small), dict(m_small), dict(v_small)
    for i, n in enumerate(BIG):
        shp = wts[n].shape
        two = lambda a: a.reshape(-1, shp[-1])
        g, d, nm, nv = _adamw(two(wts[n]), part[i], other[i], two(mom[n]), two(var[n]), "adamw_" + n)
        out_g[n], out_d[n], out_m[n], out_v[n] = g.reshape(shp), d.reshape(shp), nm.reshape(shp), nv.reshape(shp)

    return (loss, dx[None], *[out_g[n] for n in ORDER], *[out_d[n] for n in ORDER], *[out_m[n] for n in ORDER],
            *[out_v[n] for n in ORDER])
```

```python
import functools
import math

import jax
import jax.numpy as jnp
from jax import lax
from jax.experimental import pallas as pl
from jax.experimental.pallas import tpu as pltpu

F32 = jnp.float32
BF = jnp.bfloat16
MESH = pl.DeviceIdType.MESH

EPS = 1e-6
D = 1024
NH = 8
HW = 128
GQA_KV = 2
GQA_G = 4
HEAD = 64
MLA_NOPE = 64
MLA_ROPE = 32
MLA_QK = 96
Q_RANK = 384
KV_RANK = 256
DFF = 4096
GRID_W = 64
ROPE_THETA = 10000.0

O_QA, O_KA, O_VA, O_CQ, O_CKV, O_KR, ZM_W = 0, 1024, 1280, 1536, 1920, 2176, 2304
ZG_W = 2048

ADAM_LR, ADAM_B1, ADAM_B2, ADAM_EPS, ADAM_WD, ADAM_STEP = 0.001, 0.9, 0.999, 1e-08, 0.01, 10

TM = 256
TQ = 256
ONES_LANE = 64
LOG2E = 1.4426950408889634
LN2 = 0.6931471805599453
SCALE_GQA = 1.0 / math.sqrt(HEAD)
SCALE_MLA = 1.0 / math.sqrt(MLA_QK)
C2_GQA = SCALE_GQA * LOG2E
C2_MLA = SCALE_MLA * LOG2E
VMEM_LIMIT = 56 * 1024 * 1024
TN_ROWS_NARROW, TN_ROWS_WIDE = 2048, 1024

BIG = ("w_in", "w_q_up", "w_kv_up", "w_branch_a", "w_branch_b", "w_o", "w_ffn_up", "w_ffn_down")
SMALL = ("b_gate", "q_norm_g", "k_norm_g", "q_a_norm_g", "kv_a_norm_g", "pre_mix_g", "post_mix_g", "pre_ffn_g", "post_ffn_g")
ORDER = ("w_in", "b_gate", "q_norm_g", "k_norm_g", "q_a_norm_g", "kv_a_norm_g", "w_q_up", "w_kv_up", "w_branch_a",
         "w_branch_b", "w_o", "w_ffn_up", "w_ffn_down", "pre_mix_g", "post_mix_g", "pre_ffn_g", "post_ffn_g")


def _params(sem=None):
    return pltpu.CompilerParams(dimension_semantics=sem, vmem_limit_bytes=VMEM_LIMIT)


def _rows(tm, w):
    return pl.BlockSpec((tm, w), lambda i: (i, 0))


def _cols(h, tm):
    return pl.BlockSpec((h, tm), lambda i: (0, i))


def _whole(shape):
    return pl.BlockSpec(shape, lambda i: (0,) * len(shape))


def _dot(a, b):
    return jnp.dot(a, b, preferred_element_type=F32)


def _dot_nt(a, b):
    return lax.dot_general(a, b, (((1,), (1,)), ((), ())), preferred_element_type=F32)


def _dot_tn(a, b):
    return lax.dot_general(a, b, (((0,), (0,)), ((), ())), preferred_element_type=F32)


def _norm_fwd(xv, g, n=None):
    n = xv.shape[-1] if n is None else n
    r = lax.rsqrt(jnp.sum(xv * xv, axis=-1, keepdims=True) * (1.0 / n) + EPS)
    return (xv * r) * g


def _norm_bwd(xv, g, dy, n=None):
    n = xv.shape[-1] if n is None else n
    r = lax.rsqrt(jnp.sum(xv * xv, axis=-1, keepdims=True) * (1.0 / n) + EPS)
    xh = xv * r
    dxh = dy * g
    dg = jnp.sum(dy * xh, axis=0, keepdims=True)
    dx = r * (dxh - xh * (jnp.sum(dxh * xh, axis=-1, keepdims=True) * (1.0 / n)))
    return dx, dg


def _accumulate(ref, val):
    @pl.when(pl.program_id(0) == 0)
    def _():
        ref[...] = jnp.zeros_like(ref)

    ref[...] += val


def _rot(xv, q, neg):
    w = xv.shape[-1]
    return jnp.where(neg, -pltpu.roll(xv, w - q, 1), pltpu.roll(xv, q, 1))


def _rope(xv, c, s, q, neg):
    return xv * c + _rot(xv, q, neg) * s


def _rope_bwd(dy, c, s, q, neg):
    return dy * c - _rot(dy * s, q, neg)


def _neg_mask(tm, q):
    lane = lax.broadcasted_iota(jnp.int32, (tm, HW), 1)
    return (lane % (2 * q)) < q


def _sigmoid(z):
    return 1.0 / (1.0 + jnp.exp(-z))


def _fwd_inproj(x, g1, w1, w2):
    t = x.shape[0]

    def body(x_ref, g_ref, w1_ref, w2_ref, u_ref, zm_ref, zg_ref):
        u = _norm_fwd(x_ref[...], g_ref[...]).astype(BF)
        u_ref[...] = u
        for c in range(0, ZM_W, 768):
            zm_ref[:, c:c + 768] = _dot(u, w1_ref[:, c:c + 768])
        for c in range(0, ZG_W, 512):
            zg_ref[:, c:c + 512] = _dot(u, w2_ref[:, c:c + 512])

    return pl.pallas_call(
        body, name="fwd_inproj", grid=(t // TM,),
        in_specs=[_rows(TM, D), _whole((1, D)), _whole((D, ZM_W)), _whole((D, ZG_W))],
        out_specs=[_rows(TM, D), _rows(TM, ZM_W), _rows(TM, ZG_W)],
        out_shape=[jax.ShapeDtypeStruct((t, D), BF), jax.ShapeDtypeStruct((t, ZM_W), F32),
                   jax.ShapeDtypeStruct((t, ZG_W), F32)],
        compiler_params=_params(("parallel",)),
    )(x, g1, w1, w2)


def _fwd_mid(zm, gq, gk, gqa, gkva, wq, wk, wv, ca, sa, cb, sb):
    t = zm.shape[0]

    def body(zm_ref, gq_ref, gk_ref, gqa_ref, gkva_ref, wq_ref, wk_ref, wv_ref, ca_ref, sa_ref, cb_ref, sb_ref,
             qa_ref, ka_ref, va_ref, qb_ref, kb_ref, vb_ref, cqn_ref, ckvn_ref):
        ca_, sa_, cb_, sb_ = ca_ref[...], sa_ref[...], cb_ref[...], sb_ref[...]
        nega, negb = _neg_mask(TM, 16), _neg_mask(TM, 8)
        for h in range(NH):
            xv = zm_ref[:, O_QA + h * HW:O_QA + (h + 1) * HW]
            qa_ref[:, h * HW:(h + 1) * HW] = (_rope(_norm_fwd(xv, gq_ref[...], HEAD), ca_, sa_, 16, nega) * C2_GQA).astype(BF)
        for h in range(GQA_KV):
            xv = zm_ref[:, O_KA + h * HW:O_KA + (h + 1) * HW]
            ka_ref[:, h * HW:(h + 1) * HW] = _rope(_norm_fwd(xv, gk_ref[...], HEAD), ca_, sa_, 16, nega).astype(BF)
        ones = lax.broadcasted_iota(jnp.int32, (TM, HW), 1) == ONES_LANE
        for h in range(GQA_KV):
            va_ref[:, h * HW:(h + 1) * HW] = jnp.where(ones, 1.0, zm_ref[:, O_VA + h * HW:O_VA + (h + 1) * HW]).astype(BF)
        cqn = _norm_fwd(zm_ref[:, O_CQ:O_CQ + Q_RANK], gqa_ref[...]).astype(BF)
        ckvn = _norm_fwd(zm_ref[:, O_CKV:O_CKV + KV_RANK], gkva_ref[...]).astype(BF)
        cqn_ref[...] = cqn
        ckvn_ref[...] = ckvn
        qb = _dot(cqn, wq_ref[...])
        kpre = _dot(ckvn, wk_ref[...])
        kr = _rope(zm_ref[:, O_KR:O_KR + HW], cb_, sb_, 8, negb)
        for h in range(NH):
            sl = slice(h * HW, (h + 1) * HW)
            qb_ref[:, sl] = (_rope(qb[:, sl], cb_, sb_, 8, negb) * C2_MLA).astype(BF)
            kb_ref[:, sl] = (kpre[:, sl] + kr).astype(BF)
        vb = _dot(ckvn, wv_ref[...])
        for h in range(NH):
            vb_ref[:, h * HW:(h + 1) * HW] = jnp.where(ones, 1.0, vb[:, h * HW:(h + 1) * HW]).astype(BF)

    hw8 = NH * HW
    return pl.pallas_call(
        body, name="fwd_mid", grid=(t // TM,),
        in_specs=[_rows(TM, ZM_W), _whole((1, HW)), _whole((1, HW)), _whole((1, Q_RANK)), _whole((1, KV_RANK)),
                  _whole((Q_RANK, hw8)), _whole((KV_RANK, hw8)), _whole((KV_RANK, hw8)),
                  _rows(TM, HW), _rows(TM, HW), _rows(TM, HW), _rows(TM, HW)],
        out_specs=[_rows(TM, hw8), _rows(TM, GQA_KV * HW), _rows(TM, GQA_KV * HW), _rows(TM, hw8), _rows(TM, hw8),
                   _rows(TM, hw8), _rows(TM, Q_RANK), _rows(TM, KV_RANK)],
        out_shape=[jax.ShapeDtypeStruct((t, hw8), BF), jax.ShapeDtypeStruct((t, GQA_KV * HW), BF),
                   jax.ShapeDtypeStruct((t, GQA_KV * HW), BF), jax.ShapeDtypeStruct((t, hw8), BF),
                   jax.ShapeDtypeStruct((t, hw8), BF), jax.ShapeDtypeStruct((t, hw8), BF),
                   jax.ShapeDtypeStruct((t, Q_RANK), BF), jax.ShapeDtypeStruct((t, KV_RANK), BF)],
        compiler_params=_params(("parallel",)),
    )(zm, gq, gk, gqa, gkva, wq, wk, wv, ca, sa, cb, sb)


class _ChipExchange:
    def __init__(self, srcs, per_dest):
        self.srcs, self.per_dest, self.n = list(srcs), per_dest, len(srcs)
        self.out_shape = [jax.ShapeDtypeStruct((4,) + tuple(s.shape[1:] if per_dest else s.shape), s.dtype) for s in srcs]
        self.specs = [pl.BlockSpec(memory_space=pl.ANY)] * self.n
        self.scratch = [pltpu.SemaphoreType.DMA((3 * self.n,)), pltpu.SemaphoreType.DMA((3 * self.n,)),
                        pltpu.SemaphoreType.DMA((self.n,))]

    def _copies(self, src_refs, out_refs, send_sems, recv_sems, local_sems):
        x, y, c = lax.axis_index("x"), lax.axis_index("y"), lax.axis_index("c")
        me = 2 * x + y
        chips = [(1 - x, y), (x, 1 - y), (1 - x, 1 - y)]

        def piece(a, k):
            return src_refs[a].at[k] if self.per_dest else src_refs[a]

        def remote(a, j, src_chip, dst_slab):
            px, py = chips[j]
            return pltpu.make_async_remote_copy(
                src_ref=piece(a, src_chip), dst_ref=out_refs[a].at[dst_slab], send_sem=send_sems.at[3 * a + j],
                recv_sem=recv_sems.at[3 * a + j], device_id=(px, py, c), device_id_type=MESH)

        local = [pltpu.make_async_copy(piece(a, me), out_refs[a].at[me], local_sems.at[a]) for a in range(self.n)]
        sends = [remote(a, j, 2 * chips[j][0] + chips[j][1], me) for a in range(self.n) for j in range(3)]
        recvs = [remote(a, j, me, 2 * chips[j][0] + chips[j][1]) for a in range(self.n) for j in range(3)]
        return local, sends, recvs

    def start(self, *refs):
        local, sends, _ = self._copies(*refs)
        for cp in local + sends:
            cp.start()

    def finish(self, *refs):
        local, sends, recvs = self._copies(*refs)
        for cp in recvs:
            cp.wait_recv()
        for cp in sends:
            cp.wait_send()
        for cp in local:
            cp.wait()


def _hosted(exchange, refs, n_in, n_out):
    n = exchange.n if exchange else 0
    ins, srcs = refs[:n_in], refs[n_in:n_in + n]
    outs = refs[n_in + n:n_in + n + n_out]
    rest = refs[n_in + n + n_out:]
    return ins, outs, (tuple(srcs), tuple(rest[:n])) + tuple(rest[n:])


def _flash_fwd(q, k, v, group, name, exchange=None):
    t = q.shape[0]
    tq = min(TQ, t)
    nq = t // tq

    def body(*refs):
        (q_ref, k_ref, v_ref), (o_ref, lse_ref), ex_refs = _hosted(exchange, refs, 3, 2)
        if exchange:
            @pl.when((pl.program_id(0) == 0) & (pl.program_id(1) == 0))
            def _():
                exchange.start(*ex_refs)

        s = _dot_nt(q_ref[...], k_ref[...])
        m = jnp.max(s, axis=-1, keepdims=True)
        acc = _dot(jnp.exp2(s - m).astype(BF), v_ref[...])
        l = acc[:, ONES_LANE:ONES_LANE + 1]
        o_ref[...] = acc / l
        lse_ref[...] = jnp.broadcast_to(m + jnp.log2(l), (tq, HW))
        if exchange:
            @pl.when((pl.program_id(0) == NH - 1) & (pl.program_id(1) == nq - 1))
            def _():
                exchange.finish(*ex_refs)

    qspec = pl.BlockSpec((tq, HW), lambda h, i: (i, h))
    kspec = pl.BlockSpec((t, HW), lambda h, i: (0, h // group))
    out = pl.pallas_call(
        body, name=name, grid=(NH, nq),
        in_specs=[qspec, kspec, kspec] + (exchange.specs if exchange else []),
        out_specs=[qspec, qspec] + (exchange.specs if exchange else []),
        out_shape=[jax.ShapeDtypeStruct((t, NH * HW), F32), jax.ShapeDtypeStruct((t, NH * HW), F32)]
        + (exchange.out_shape if exchange else []),
        scratch_shapes=exchange.scratch if exchange else [],
        compiler_params=_params(("arbitrary", "arbitrary") if exchange else ("parallel", "parallel")),
    )(q, k, v, *(exchange.srcs if exchange else []))
    return out[0], out[1], out[2:]


def _fwd_merge(ya, yb, zg, b, wa, wb, wo, x, g2):
    t = x.shape[0]

    def body(ya_ref, yb_ref, zg_ref, b_ref, wa_ref, wb_ref, wo_ref, x_ref, g_ref, pa_ref, pb_ref, mg_ref, m_ref, x1_ref):
        pa = _dot(ya_ref[...].astype(BF), wa_ref[...])
        pb = _dot(yb_ref[...].astype(BF), wb_ref[...])
        pa_ref[...] = pa
        pb_ref[...] = pb
        ga = _sigmoid(zg_ref[:, :D] + b_ref[:, :D])
        gb = _sigmoid(zg_ref[:, D:] + b_ref[:, D:])
        mg = (ga * pa + gb * pb).astype(BF)
        mg_ref[...] = mg
        m = _dot(mg, wo_ref[...])
        m_ref[...] = m
        x1_ref[...] = x_ref[...] + _norm_fwd(m, g_ref[...])

    return pl.pallas_call(
        body, name="fwd_merge", grid=(t // TM,),
        in_specs=[_rows(TM, D), _rows(TM, D), _rows(TM, ZG_W), _whole((1, ZG_W)), _whole((D, D)), _whole((D, D)),
                  _whole((D, D)), _rows(TM, D), _whole((1, D))],
        out_specs=[_rows(TM, D), _rows(TM, D), _rows(TM, D), _rows(TM, D), _rows(TM, D)],
        out_shape=[jax.ShapeDtypeStruct((t, D), F32), jax.ShapeDtypeStruct((t, D), F32), jax.ShapeDtypeStruct((t, D), BF),
                   jax.ShapeDtypeStruct((t, D), F32), jax.ShapeDtypeStruct((t, D), F32)],
        compiler_params=_params(("parallel",)),
    )(ya, yb, zg, b, wa, wb, wo, x, g2)


def _fwd_ffn_up(x1, g3, wup):
    t = x1.shape[0]

    def body(x_ref, g_ref, w_ref, hn_ref, h_ref, a_ref):
        hn = _norm_fwd(x_ref[...], g_ref[...]).astype(BF)
        hn_ref[...] = hn
        for c in range(0, DFF, 1024):
            h = _dot(hn, w_ref[:, c:c + 1024])
            h_ref[:, c:c + 1024] = h
            r = jnp.maximum(h, 0.0)
            a_ref[:, c:c + 1024] = (r * r).astype(BF)

    return pl.pallas_call(
        body, name="fwd_ffn_up", grid=(t // TM,),
        in_specs=[_rows(TM, D), _whole((1, D)), _whole((D, DFF))],
        out_specs=[_rows(TM, D), _rows(TM, DFF), _rows(TM, DFF)],
        out_shape=[jax.ShapeDtypeStruct((t, D), BF), jax.ShapeDtypeStruct((t, DFF), F32), jax.ShapeDtypeStruct((t, DFF), BF)],
        compiler_params=_params(("parallel",)),
    )(x1, g3, wup)


def _fwd_ffn_down(a, wdn, x1, g4):
    t = x1.shape[0]

    def body(a_ref, w_ref, x_ref, g_ref, f_ref, x2_ref):
        f = _dot(a_ref[...], w_ref[...])
        f_ref[...] = f
        x2_ref[...] = x_ref[...] + _norm_fwd(f, g_ref[...])

    return pl.pallas_call(
        body, name="fwd_ffn_down", grid=(t // TM,),
        in_specs=[_rows(TM, DFF), _whole((DFF, D)), _rows(TM, D), _whole((1, D))],
        out_specs=[_rows(TM, D), _rows(TM, D)],
        out_shape=[jax.ShapeDtypeStruct((t, D), F32), jax.ShapeDtypeStruct((t, D), F32)],
        compiler_params=_params(("parallel",)),
    )(a, wdn, x1, g4)


def _loss_head(y, target):
    t = y.shape[0]

    def body(y_ref, t_ref, dy_ref, loss_ref):
        d = y_ref[...] - t_ref[...]
        dy_ref[...] = d * (1.0 / D)
        part = 0.5 * jnp.sum(jnp.sum(d * d, axis=-1, keepdims=True) * (1.0 / D), axis=0, keepdims=True)
        _accumulate(loss_ref, jnp.broadcast_to(part, (8, HW)))

    return pl.pallas_call(
        body, name="loss_head", grid=(t // TM,),
        in_specs=[_rows(TM, D), _rows(TM, D)],
        out_specs=[_rows(TM, D), _whole((8, HW))],
        out_shape=[jax.ShapeDtypeStruct((t, D), F32), jax.ShapeDtypeStruct((8, HW), F32)],
        compiler_params=_params(("arbitrary",)),
    )(y, target)


def _bwd_ffn_down(f, dx2, g4, wdn, h):
    t = f.shape[0]

    def body(f_ref, dx2_ref, g_ref, w_ref, h_ref, df_ref, dh_ref, dg_ref):
        df, dg = _norm_bwd(f_ref[...], g_ref[...], dx2_ref[...])
        _accumulate(dg_ref, dg)
        df16 = df.astype(BF)
        df_ref[...] = df16
        for c in range(0, DFF, 1024):
            da = _dot_nt(df16, w_ref[c:c + 1024, :])
            dh_ref[:, c:c + 1024] = (da * (2.0 * jnp.maximum(h_ref[:, c:c + 1024], 0.0))).astype(BF)

    return pl.pallas_call(
        body, name="bwd_ffn_down", grid=(t // TM,),
        in_specs=[_rows(TM, D), _rows(TM, D), _whole((1, D)), _whole((DFF, D)), _rows(TM, DFF)],
        out_specs=[_rows(TM, D), _rows(TM, DFF), _whole((1, D))],
        out_shape=[jax.ShapeDtypeStruct((t, D), BF), jax.ShapeDtypeStruct((t, DFF), BF), jax.ShapeDtypeStruct((1, D), F32)],
        compiler_params=_params(("arbitrary",)),
    )(f, dx2, g4, wdn, h)


def _bwd_ffn_up(dh, wup, x1, g3, dx2):
    t = x1.shape[0]

    def body(dh_ref, w_ref, x_ref, g_ref, dx2_ref, dx1_ref, dg_ref):
        dhn = _dot_nt(dh_ref[...], w_ref[...])
        dx, dg = _norm_bwd(x_ref[...], g_ref[...], dhn)
        _accumulate(dg_ref, dg)
        dx1_ref[...] = dx2_ref[...] + dx

    return pl.pallas_call(
        body, name="bwd_ffn_up", grid=(t // TM,),
        in_specs=[_rows(TM, DFF), _whole((D, DFF)), _rows(TM, D), _whole((1, D)), _rows(TM, D)],
        out_specs=[_rows(TM, D), _whole((1, D))],
        out_shape=[jax.ShapeDtypeStruct((t, D), F32), jax.ShapeDtypeStruct((1, D), F32)],
        compiler_params=_params(("arbitrary",)),
    )(dh, wup, x1, g3, dx2)


def _bwd_merge(m, dx1, g2, wo, zg, b, pa, pb, wa, wb):
    t = m.shape[0]

    def body(m_ref, dx1_ref, g_ref, wo_ref, zg_ref, b_ref, pa_ref, pb_ref, wa_ref, wb_ref,
             dm_ref, dpa_ref, dpb_ref, dzg_ref, dya_ref, dyb_ref, dg_ref, db_ref):
        dm, dg = _norm_bwd(m_ref[...], g_ref[...], dx1_ref[...])
        _accumulate(dg_ref, dg)
        dm16 = dm.astype(BF)
        dm_ref[...] = dm16
        dmg = _dot_nt(dm16, wo_ref[...])
        ga = _sigmoid(zg_ref[:, :D] + b_ref[:, :D])
        gb = _sigmoid(zg_ref[:, D:] + b_ref[:, D:])
        dpa = (dmg * ga).astype(BF)
        dpb = (dmg * gb).astype(BF)
        dpa_ref[...] = dpa
        dpb_ref[...] = dpb
        dza = (dmg * pa_ref[...]) * (ga * (1.0 - ga))
        dzb = (dmg * pb_ref[...]) * (gb * (1.0 - gb))
        dzg_ref[:, :D] = dza.astype(BF)
        dzg_ref[:, D:] = dzb.astype(BF)

        @pl.when(pl.program_id(0) == 0)
        def _():
            db_ref[...] = jnp.zeros_like(db_ref)

        db_ref[:, :D] += jnp.sum(dza, axis=0, keepdims=True)
        db_ref[:, D:] += jnp.sum(dzb, axis=0, keepdims=True)
        dya_ref[...] = _dot_nt(dpa, wa_ref[...]).astype(BF)
        dyb_ref[...] = _dot_nt(dpb, wb_ref[...]).astype(BF)

    return pl.pallas_call(
        body, name="bwd_merge", grid=(t // TM,),
        in_specs=[_rows(TM, D), _rows(TM, D), _whole((1, D)), _whole((D, D)), _rows(TM, ZG_W), _whole((1, ZG_W)),
                  _rows(TM, D), _rows(TM, D), _whole((D, D)), _whole((D, D))],
        out_specs=[_rows(TM, D), _rows(TM, D), _rows(TM, D), _rows(TM, ZG_W), _rows(TM, D), _rows(TM, D),
                   _whole((1, D)), _whole((1, ZG_W))],
        out_shape=[jax.ShapeDtypeStruct((t, D), BF), jax.ShapeDtypeStruct((t, D), BF), jax.ShapeDtypeStruct((t, D), BF),
                   jax.ShapeDtypeStruct((t, ZG_W), BF), jax.ShapeDtypeStruct((t, D), BF), jax.ShapeDtypeStruct((t, D), BF),
                   jax.ShapeDtypeStruct((1, D), F32), jax.ShapeDtypeStruct((1, ZG_W), F32)],
        compiler_params=_params(("arbitrary",)),
    )(m, dx1, g2, wo, zg, b, pa, pb, wa, wb)


def _flash_bwd(q, do, o, lse, k, v, group, scale, name, exchange=None):
    t = q.shape[0]
    tq = min(TQ, t)
    nq = t // tq
    nkv = NH // group

    def body(*refs):
        (q_ref, do_ref, o_ref, lse_ref, k_ref, v_ref), (dq_ref, dkt_ref, dvt_ref), ex_refs = _hosted(exchange, refs, 6, 3)
        first = (pl.program_id(1) == 0) & (pl.program_id(2) == 0)
        if exchange:
            @pl.when(first & (pl.program_id(0) == 0))
            def _():
                exchange.start(*ex_refs)

        @pl.when(first)
        def _():
            dkt_ref[...] = jnp.zeros_like(dkt_ref)
            dvt_ref[...] = jnp.zeros_like(dvt_ref)

        qv = q_ref[...]
        dov = do_ref[...]
        dsum = jnp.sum(dov.astype(F32) * o_ref[...], axis=-1, keepdims=True)
        p = jnp.exp2(_dot_nt(qv, k_ref[...]) - lse_ref[:, :1])
        ds = (p * (_dot_nt(dov, v_ref[...]) - dsum)).astype(BF)
        dq_ref[...] = _dot(ds, k_ref[...]) * scale
        dvt_ref[...] += _dot(dov.astype(F32).T.astype(BF), p.astype(BF))
        dkt_ref[...] += _dot(qv.astype(F32).T.astype(BF), ds) * LN2
        if exchange:
            @pl.when((pl.program_id(0) == nkv - 1) & (pl.program_id(1) == group - 1) & (pl.program_id(2) == nq - 1))
            def _():
                exchange.finish(*ex_refs)

    qspec = pl.BlockSpec((tq, HW), lambda hk, g, i: (i, hk * group + g))
    kspec = pl.BlockSpec((t, HW), lambda hk, g, i: (0, hk))
    tspec = pl.BlockSpec((HW, t), lambda hk, g, i: (hk, 0))
    out = pl.pallas_call(
        body, name=name, grid=(nkv, group, nq),
        in_specs=[qspec, qspec, qspec, qspec, kspec, kspec] + (exchange.specs if exchange else []),
        out_specs=[qspec, tspec, tspec] + (exchange.specs if exchange else []),
        out_shape=[jax.ShapeDtypeStruct((t, NH * HW), F32), jax.ShapeDtypeStruct((nkv * HW, t), F32),
                   jax.ShapeDtypeStruct((nkv * HW, t), F32)] + (exchange.out_shape if exchange else []),
        scratch_shapes=exchange.scratch if exchange else [],
        compiler_params=_params(("arbitrary" if exchange else "parallel", "arbitrary", "arbitrary")),
    )(q, do, o, lse, k, v, *(exchange.srcs if exchange else []))
    return out[0], out[1], out[2], out[3:]


def _bwd_mid(zm, dqa, dkta, dvta, dqb, dktb, dvtb, gq, gk, gqa, gkva, wq, wk, wv, ca, sa, cb, sb):
    t = zm.shape[0]
    hw8 = NH * HW

    def body(zm_ref, dqa_ref, dkta_ref, dvta_ref, dqb_ref, dktb_ref, dvtb_ref, gq_ref, gk_ref, gqa_ref, gkva_ref,
             wq_ref, wk_ref, wv_ref, ca_ref, sa_ref, cb_ref, sb_ref,
             dzm_ref, dqbp_ref, dkb16_ref, dvb16_ref, dgq_ref, dgk_ref, dgqa_ref, dgkva_ref):
        ca_, sa_, cb_, sb_ = ca_ref[...], sa_ref[...], cb_ref[...], sb_ref[...]
        nega, negb = _neg_mask(TM, 16), _neg_mask(TM, 8)
        dgq = jnp.zeros((1, HW), F32)
        for h in range(NH):
            sl = slice(h * HW, (h + 1) * HW)
            dqn = _rope_bwd(dqa_ref[:, sl], ca_, sa_, 16, nega)
            dx, dg = _norm_bwd(zm_ref[:, O_QA + h * HW:O_QA + (h + 1) * HW], gq_ref[...], dqn, HEAD)
            dzm_ref[:, O_QA + h * HW:O_QA + (h + 1) * HW] = dx.astype(BF)
            dgq = dgq + dg
        _accumulate(dgq_ref, dgq)
        dgk = jnp.zeros((1, HW), F32)
        for j in range(GQA_KV):
            dk = dkta_ref[j * HW:(j + 1) * HW, :].T
            dv = dvta_ref[j * HW:(j + 1) * HW, :].T
            dkn = _rope_bwd(dk, ca_, sa_, 16, nega)
            dx, dg = _norm_bwd(zm_ref[:, O_KA + j * HW:O_KA + (j + 1) * HW], gk_ref[...], dkn, HEAD)
            dzm_ref[:, O_KA + j * HW:O_KA + (j + 1) * HW] = dx.astype(BF)
            dzm_ref[:, O_VA + j * HW:O_VA + (j + 1) * HW] = dv.astype(BF)
            dgk = dgk + dg
        _accumulate(dgk_ref, dgk)
        for h in range(NH):
            sl = slice(h * HW, (h + 1) * HW)
            dqbp_ref[:, sl] = _rope_bwd(dqb_ref[:, sl], cb_, sb_, 8, negb).astype(BF)
        dcqn = _dot_nt(dqbp_ref[...], wq_ref[...])
        dx, dg = _norm_bwd(zm_ref[:, O_CQ:O_CQ + Q_RANK], gqa_ref[...], dcqn)
        dzm_ref[:, O_CQ:O_CQ + Q_RANK] = dx.astype(BF)
        _accumulate(dgqa_ref, dg)
        dkr = jnp.zeros((TM, HW), F32)
        for h in range(NH):
            sl = slice(h * HW, (h + 1) * HW)
            dkh = dktb_ref[sl, :].T
            dkr = dkr + dkh
            dkb16_ref[:, sl] = dkh.astype(BF)
            dvb16_ref[:, sl] = dvtb_ref[sl, :].T.astype(BF)
        dkb16 = dkb16_ref[...]
        dvb16 = dvb16_ref[...]
        lane = lax.broadcasted_iota(jnp.int32, (TM, HW), 1)
        in_rope = (lane >= MLA_NOPE) & (lane < MLA_QK)
        dzm_ref[:, O_KR:O_KR + HW] = jnp.where(in_rope, _rope_bwd(dkr, cb_, sb_, 8, negb), 0.0).astype(BF)
        dckvn = _dot_nt(dkb16, wk_ref[...]) + _dot_nt(dvb16, wv_ref[...])
        dx, dg = _norm_bwd(zm_ref[:, O_CKV:O_CKV + KV_RANK], gkva_ref[...], dckvn)
        dzm_ref[:, O_CKV:O_CKV + KV_RANK] = dx.astype(BF)
        _accumulate(dgkva_ref, dg)

    return pl.pallas_call(
        body, name="bwd_mid", grid=(t // TM,),
        in_specs=[_rows(TM, ZM_W), _rows(TM, hw8), _cols(GQA_KV * HW, TM), _cols(GQA_KV * HW, TM), _rows(TM, hw8),
                  _cols(hw8, TM), _cols(hw8, TM), _whole((1, HW)), _whole((1, HW)), _whole((1, Q_RANK)), _whole((1, KV_RANK)),
                  _whole((Q_RANK, hw8)), _whole((KV_RANK, hw8)), _whole((KV_RANK, hw8)),
                  _rows(TM, HW), _rows(TM, HW), _rows(TM, HW), _rows(TM, HW)],
        out_specs=[_rows(TM, ZM_W), _rows(TM, hw8), _rows(TM, hw8), _rows(TM, hw8),
                   _whole((1, HW)), _whole((1, HW)), _whole((1, Q_RANK)), _whole((1, KV_RANK))],
        out_shape=[jax.ShapeDtypeStruct((t, ZM_W), BF), jax.ShapeDtypeStruct((t, hw8), BF), jax.ShapeDtypeStruct((t, hw8), BF),
                   jax.ShapeDtypeStruct((t, hw8), BF), jax.ShapeDtypeStruct((1, HW), F32), jax.ShapeDtypeStruct((1, HW), F32),
                   jax.ShapeDtypeStruct((1, Q_RANK), F32), jax.ShapeDtypeStruct((1, KV_RANK), F32)],
        compiler_params=_params(("arbitrary",)),
    )(zm, dqa, dkta, dvta, dqb, dktb, dvtb, gq, gk, gqa, gkva, wq, wk, wv, ca, sa, cb, sb)


def _bwd_inproj(dzm, dzg, w1, w2, x, g1, dx1):
    t = x.shape[0]

    def body(dzm_ref, dzg_ref, w1_ref, w2_ref, x_ref, g_ref, dx1_ref, dx_ref, dg_ref):
        du = _dot_nt(dzm_ref[...], w1_ref[...]) + _dot_nt(dzg_ref[...], w2_ref[...])
        dx, dg = _norm_bwd(x_ref[...], g_ref[...], du)
        _accumulate(dg_ref, dg)
        dx_ref[...] = dx1_ref[...] + dx

    return pl.pallas_call(
        body, name="bwd_inproj", grid=(t // TM,),
        in_specs=[_rows(TM, ZM_W), _rows(TM, ZG_W), _whole((D, ZM_W)), _whole((D, ZG_W)), _rows(TM, D), _whole((1, D)),
                  _rows(TM, D)],
        out_specs=[_rows(TM, D), _whole((1, D))],
        out_shape=[jax.ShapeDtypeStruct((t, D), F32), jax.ShapeDtypeStruct((1, D), F32)],
        compiler_params=_params(("arbitrary",)),
    )(dzm, dzg, w1, w2, x, g1, dx1)


def _matmul_tn(a, b, tn, name):
    t, kdim = a.shape
    n = b.shape[1]
    tm = min(TN_ROWS_NARROW if kdim <= D else TN_ROWS_WIDE, t)
    nsteps = t // tm

    def body(a_ref, b_ref, o_ref, acc_ref):
        i = pl.program_id(1)

        @pl.when(i == 0)
        def _():
            acc_ref[...] = jnp.zeros_like(acc_ref)

        acc_ref[...] += _dot_tn(a_ref[...].astype(BF), b_ref[...])

        @pl.when(i == nsteps - 1)
        def _():
            o_ref[...] = acc_ref[...].astype(BF)

    return pl.pallas_call(
        body, name=name, grid=(n // tn, nsteps),
        in_specs=[pl.BlockSpec((tm, kdim), lambda j, i: (i, 0)), pl.BlockSpec((tm, tn), lambda j, i: (i, j))],
        out_specs=pl.BlockSpec((kdim, tn), lambda j, i: (0, j)),
        out_shape=jax.ShapeDtypeStruct((kdim, n), BF),
        scratch_shapes=[pltpu.VMEM((kdim, tn), F32)],
        compiler_params=_params(("parallel", "arbitrary")),
    )(a, b)


def _rope_tables(t):
    pos = jnp.arange(t, dtype=jnp.int32)
    row = (pos // GRID_W).astype(F32)
    col = (pos % GRID_W).astype(F32)

    def table(rot_dim):
        half = rot_dim // 2
        inv = ROPE_THETA ** (-jnp.arange(0, half, 2, dtype=F32) / half)
        ar = row[:, None] * inv[None, :]
        ac = col[:, None] * inv[None, :]
        ang = jnp.concatenate([ar, ar, ac, ac], axis=-1)
        return jnp.cos(ang), jnp.sin(ang)

    c64, s64 = table(HEAD)
    c32, s32 = table(MLA_ROPE)
    ones = lambda w: jnp.ones((t, w), F32)
    zeros = lambda w: jnp.zeros((t, w), F32)
    ca = jnp.concatenate([c64, ones(HW - HEAD)], axis=1)
    sa = jnp.concatenate([s64, zeros(HW - HEAD)], axis=1)
    cb = jnp.concatenate([ones(MLA_NOPE), c32, ones(HW - MLA_QK)], axis=1)
    sb = jnp.concatenate([zeros(MLA_NOPE), s32, zeros(HW - MLA_QK)], axis=1)
    return ca, sa, cb, sb


def _pad_heads_cols(w, heads, width):
    k = w.shape[0]
    w = w.reshape(k, heads, width)
    return jnp.pad(w, ((0, 0), (0, 0), (0, HW - width))).reshape(k, heads * HW)


def _pad_heads_rows(w, heads, width):
    n = w.shape[1]
    w = w.reshape(heads, width, n)
    return jnp.pad(w, ((0, 0), (0, HW - width), (0, 0))).reshape(heads * HW, n)


def _unpad_heads_cols(w, heads, width):
    k = w.shape[0]
    return w.reshape(k, heads, HW)[:, :, :width].reshape(k, heads * width)


def _unpad_heads_rows(w, heads, width):
    n = w.shape[1]
    return w.reshape(heads, HW, n)[:, :width, :].reshape(heads * width, n)


def _pad_layer_weights(w):
    w_in = w["w_in"]
    qa = _pad_heads_cols(w_in[:, 0:512], NH, HEAD)
    ka = _pad_heads_cols(w_in[:, 512:640], GQA_KV, HEAD)
    va = _pad_heads_cols(w_in[:, 640:768], GQA_KV, HEAD)
    cq = w_in[:, 768:1152]
    ckv = w_in[:, 1152:1408]
    kr = jnp.pad(w_in[:, 1408:1440], ((0, 0), (MLA_NOPE, HW - MLA_QK)))
    kvu = w["w_kv_up"].reshape(KV_RANK, NH, 2 * HEAD)
    return dict(
        w1=jnp.concatenate([qa, ka, va, cq, ckv, kr], axis=1),
        w2=w_in[:, 1440:],
        wq=_pad_heads_cols(w["w_q_up"], NH, MLA_QK),
        wk=jnp.pad(kvu[:, :, :HEAD], ((0, 0), (0, 0), (0, HEAD))).reshape(KV_RANK, NH * HW),
        wv=jnp.pad(kvu[:, :, HEAD:], ((0, 0), (0, 0), (0, HEAD))).reshape(KV_RANK, NH * HW),
        wa=_pad_heads_rows(w["w_branch_a"], NH, HEAD),
        wb=_pad_heads_rows(w["w_branch_b"], NH, HEAD),
        wo=w["w_o"], wup=w["w_ffn_up"], wdn=w["w_ffn_down"],
    )


def _unpad_layer_grads(g):
    d1 = g["w1"]
    w_in = jnp.concatenate([
        _unpad_heads_cols(d1[:, O_QA:O_KA], NH, HEAD), _unpad_heads_cols(d1[:, O_KA:O_VA], GQA_KV, HEAD),
        _unpad_heads_cols(d1[:, O_VA:O_CQ], GQA_KV, HEAD), d1[:, O_CQ:O_CKV], d1[:, O_CKV:O_KR],
        d1[:, O_KR + MLA_NOPE:O_KR + MLA_QK], g["w2"]], axis=1)
    dk = g["wk"].reshape(KV_RANK, NH, HW)[:, :, :HEAD]
    dv = g["wv"].reshape(KV_RANK, NH, HW)[:, :, :HEAD]
    return dict(
        w_in=w_in,
        w_q_up=_unpad_heads_cols(g["wq"], NH, MLA_QK),
        w_kv_up=jnp.concatenate([dk, dv], axis=2).reshape(KV_RANK, NH * 2 * HEAD),
        w_branch_a=_unpad_heads_rows(g["wa"], NH, HEAD),
        w_branch_b=_unpad_heads_rows(g["wb"], NH, HEAD),
        w_o=g["wo"], w_ffn_up=g["wup"], w_ffn_down=g["wdn"],
    )


def _pad_lanes(v, width):
    return jnp.pad(v, (0, HW - width)).reshape(1, HW)


SHARD_AXIS = dict(w_in=1, w_q_up=1, w_kv_up=1, w_branch_a=1, w_branch_b=1, w_o=0, w_ffn_up=1, w_ffn_down=0)


def _join(parts, name):
    if SHARD_AXIS[name] == 0:
        return parts.reshape(-1, parts.shape[-1])
    return jnp.transpose(parts, (1, 0, 2)).reshape(parts.shape[1], -1)


def _split(full, name):
    if SHARD_AXIS[name] == 0:
        return full.reshape(4, -1, full.shape[-1])
    return jnp.transpose(full.reshape(full.shape[0], 4, -1), (1, 0, 2))


def _local_step(x, target, shards, smalls):
    t = x.shape[0]
    nl = len(smalls)
    ca, sa, cb, sb = _rope_tables(t)

    def padded(got):
        return _pad_layer_weights({n: _join(g, n) for n, g in zip(BIG, got)})

    w = padded(_chip_exchange([shards[n][0] for n in BIG], False, "gather_weights"))
    layers, saved = [], []
    for li, s in enumerate(smalls):
        gq, gk = _pad_lanes(s["q_norm_g"], HEAD), _pad_lanes(s["k_norm_g"], HEAD)
        gqa, gkva = s["q_a_norm_g"].reshape(1, -1), s["kv_a_norm_g"].reshape(1, -1)
        g1, g2, g3, g4 = (s[n].reshape(1, D) for n in ("pre_mix_g", "post_mix_g", "pre_ffn_g", "post_ffn_g"))
        b = s["b_gate"].reshape(1, ZG_W)
        u, zm, zg = _fwd_inproj(x, g1, w["w1"], w["w2"])
        qa, ka, va, qb, kb, vb, cqn, ckvn = _fwd_mid(zm, gq, gk, gqa, gkva, w["wq"], w["wk"], w["wv"], ca, sa, cb, sb)
        nxt = _ChipExchange([shards[n][li + 1] for n in BIG], False) if li + 1 < nl else None
        ya, lse_a, got = _flash_fwd(qa, ka, va, GQA_G, "flash_fwd_gqa", nxt)
        yb, lse_b, _ = _flash_fwd(qb, kb, vb, 1, "flash_fwd_mla")
        pa, pb, mg, m, x1 = _fwd_merge(ya, yb, zg, b, w["wa"], w["wb"], w["wo"], x, g2)
        hn, h, a = _fwd_ffn_up(x1, g3, w["wup"])
        f, x2 = _fwd_ffn_down(a, w["wdn"], x1, g4)
        saved.append(dict(x=x, u=u, zm=zm, zg=zg, qa=qa, ka=ka, va=va, qb=qb, kb=kb, vb=vb, cqn=cqn, ckvn=ckvn,
                          ya=ya, lse_a=lse_a, yb=yb, lse_b=lse_b, pa=pa, pb=pb, mg=mg,
                          m=m, x1=x1, hn=hn, h=h, a=a, f=f, gq=gq, gk=gk, gqa=gqa, gkva=gkva, g1=g1, g2=g2, g3=g3,
                          g4=g4, b=b))
        layers.append(w)
        x = x2
        if nxt:
            w = padded(got)

    dx, loss8 = _loss_head(x, target)
    loss = loss8[0, 0]

    received, small_grads, send = [None] * nl, [None] * nl, None
    for li in range(nl - 1, -1, -1):
        w, r = layers[li], saved[li]
        df, dh, dg4 = _bwd_ffn_down(r["f"], dx, r["g4"], w["wdn"], r["h"])
        dx1, dg3 = _bwd_ffn_up(dh, w["wup"], r["x1"], r["g3"], dx)
        dm, dpa, dpb, dzg, dya, dyb, dg2, db = _bwd_merge(r["m"], dx1, r["g2"], w["wo"], r["zg"], r["b"], r["pa"], r["pb"],
                                                         w["wa"], w["wb"])
        prev = _ChipExchange(send, True) if send else None
        dqa, dkta, dvta, got = _flash_bwd(r["qa"], dya, r["ya"], r["lse_a"], r["ka"], r["va"], GQA_G, SCALE_GQA,
                                          "flash_bwd_gqa", prev)
        if prev:
            received[li + 1] = got
        dqb, dktb, dvtb, _ = _flash_bwd(r["qb"], dyb, r["yb"], r["lse_b"], r["kb"], r["vb"], 1, SCALE_MLA, "flash_bwd_mla")
        dzm, dqbp, dkb16, dvb16, dgq, dgk, dgqa, dgkva = _bwd_mid(
            r["zm"], dqa, dkta, dvta, dqb, dktb, dvtb, r["gq"], r["gk"], r["gqa"], r["gkva"], w["wq"], w["wk"], w["wv"],
            ca, sa, cb, sb)
        dx, dg1 = _bwd_inproj(dzm, dzg, w["w1"], w["w2"], r["x"], r["g1"], dx1)
        nat = _unpad_layer_grads(dict(
            w1=_matmul_tn(r["u"], dzm, 768, "dw_in_main"), w2=_matmul_tn(r["u"], dzg, 512, "dw_in_gate"),
            wq=_matmul_tn(r["cqn"], dqbp, 512, "dw_q_up"), wk=_matmul_tn(r["ckvn"], dkb16, 512, "dw_k_up"),
            wv=_matmul_tn(r["ckvn"], dvb16, 512, "dw_v_up"), wa=_matmul_tn(r["ya"], dpa, 512, "dw_branch_a"),
            wb=_matmul_tn(r["yb"], dpb, 512, "dw_branch_b"), wo=_matmul_tn(r["mg"], dm, 512, "dw_o"),
            wup=_matmul_tn(r["hn"], dh, 512, "dw_ffn_up"), wdn=_matmul_tn(r["a"], df, 512, "dw_ffn_down")))
        send = [_split(nat[n], n) for n in BIG]
        small_grads[li] = dict(b_gate=db[0], q_norm_g=dgq[0, :HEAD], k_norm_g=dgk[0, :HEAD], q_a_norm_g=dgqa[0],
                               kv_a_norm_g=dgkva[0], pre_mix_g=dg1[0], post_mix_g=dg2[0], pre_ffn_g=dg3[0],
                               post_ffn_g=dg4[0])
    received[0] = _chip_exchange(send, True, "scatter_grads")
    return loss, dx, received, small_grads


def _chip_exchange(srcs, per_dest, name):
    ex = _ChipExchange(srcs, per_dest)

    def body(*refs):
        _, _, ex_refs = _hosted(ex, refs, 0, 0)
        ex.start(*ex_refs)
        ex.finish(*ex_refs)

    return pl.pallas_call(
        body, name=name, in_specs=ex.specs, out_specs=ex.specs, out_shape=ex.out_shape, scratch_shapes=ex.scratch,
    )(*ex.srcs)


def _sibling_exchange(srcs, name):
    n = len(srcs)

    def body(*refs):
        src_refs, out_refs = refs[:n], refs[n:2 * n]
        send_sems, recv_sems = refs[2 * n:]
        x, y, c = lax.axis_index("x"), lax.axis_index("y"), lax.axis_index("c")
        cps = [pltpu.make_async_remote_copy(src_ref=src_refs[a], dst_ref=out_refs[a], send_sem=send_sems.at[a],
                                            recv_sem=recv_sems.at[a], device_id=(x, y, 1 - c), device_id_type=MESH)
               for a in range(n)]
        for cp in cps:
            cp.start()
        for cp in cps:
            cp.wait()

    return pl.pallas_call(
        body, name=name,
        in_specs=[pl.BlockSpec(memory_space=pl.ANY)] * n,
        out_specs=[pl.BlockSpec(memory_space=pl.ANY)] * n,
        out_shape=[jax.ShapeDtypeStruct(s.shape, s.dtype) for s in srcs],
        scratch_shapes=[pltpu.SemaphoreType.DMA((n,)), pltpu.SemaphoreType.DMA((n,))],
    )(*srcs)


def _allgather_small(v):
    m_per, n = v.shape

    def body(x_ref, out_ref, send_sems, recv_sems, local_sem):
        x, y, c = lax.axis_index("x"), lax.axis_index("y"), lax.axis_index("c")
        me, sibling = (x, y, c), (x, y, 1 - c)
        chips = [(1 - x, y), (x, 1 - y), (1 - x, 1 - y)]

        def rows(px, py, pc):
            return out_ref.at[pl.ds((4 * px + 2 * py + pc) * m_per, m_per), :]

        def copy(k, block, to, src=None):
            return pltpu.make_async_remote_copy(
                src_ref=rows(*block) if src is None else src, dst_ref=rows(*block),
                send_sem=send_sems.at[k], recv_sem=recv_sems.at[k], device_id=to, device_id_type=MESH)

        mine = pltpu.make_async_copy(x_ref, rows(*me), local_sem)
        mine.start()
        first = [copy(0, me, sibling, src=x_ref)]
        first += [copy(1 + j, me, (*chip, c), src=x_ref) for j, chip in enumerate(chips)]
        for cp in first:
            cp.start()
        passed = [copy(4 + j, (*chip, c), sibling) for j, chip in enumerate(chips)]
        for j, chip in enumerate(chips):
            copy(1 + j, (*chip, c), me).wait_recv()
            passed[j].start()
        copy(0, sibling, me).wait_recv()
        for j, chip in enumerate(chips):
            copy(4 + j, (*chip, 1 - c), me).wait_recv()
        for cp in first + passed:
            cp.wait_send()
        mine.wait()

    return pl.pallas_call(
        body, name="allgather_small",
        out_shape=jax.ShapeDtypeStruct((8 * m_per, n), v.dtype),
        in_specs=[pl.BlockSpec(memory_space=pltpu.VMEM)],
        out_specs=pl.BlockSpec(memory_space=pltpu.VMEM),
        scratch_shapes=[pltpu.SemaphoreType.DMA((7,)), pltpu.SemaphoreType.DMA((7,)), pltpu.SemaphoreType.DMA],
    )(v)


def _pick_rows(r):
    return next(t for t in (512, 256, 128, 64, 32, 16, 8, r) if r % t == 0)


def _sum_slabs(a, name):
    s, r, c = a.shape
    tm = _pick_rows(r)

    def body(a_ref, o_ref):
        acc = a_ref[0].astype(F32)
        for k in range(1, s):
            acc = acc + a_ref[k].astype(F32)
        o_ref[...] = acc

    return pl.pallas_call(
        body, name=name, grid=(r // tm,),
        in_specs=[pl.BlockSpec((s, tm, c), lambda i: (0, i, 0))],
        out_specs=_rows(tm, c),
        out_shape=jax.ShapeDtypeStruct((r, c), F32),
        compiler_params=_params(("parallel",)),
    )(a)


def _adamw(w, ga, gb, m, v, name):
    r, c = w.shape
    tm = min(256, _pick_rows(r))

    def body(w_ref, ga_ref, gb_ref, m_ref, v_ref, g_ref, d_ref, nm_ref, nv_ref):
        gv = ga_ref[...] + gb_ref[...]
        g_ref[...] = gv
        nm = ADAM_B1 * m_ref[...] + (1.0 - ADAM_B1) * gv
        nv = ADAM_B2 * v_ref[...] + (1.0 - ADAM_B2) * (gv * gv)
        m_hat = nm / (1.0 - ADAM_B1 ** ADAM_STEP)
        v_hat = nv / (1.0 - ADAM_B2 ** ADAM_STEP)
        d_ref[...] = -ADAM_LR * (m_hat / (jnp.sqrt(v_hat) + ADAM_EPS) + ADAM_WD * w_ref[...])
        nm_ref[...] = nm
        nv_ref[...] = nv

    spec = _rows(tm, c)
    return pl.pallas_call(
        body, name=name, grid=(r // tm,),
        in_specs=[spec] * 5, out_specs=[spec] * 4,
        out_shape=[jax.ShapeDtypeStruct((r, c), F32)] * 4,
        compiler_params=_params(("parallel",)),
    )(w, ga, gb, m, v)


def _adamw_small(w, gparts, m, v):
    mrows, n = w.shape

    def body(w_ref, g_ref, m_ref, v_ref, go_ref, d_ref, nm_ref, nv_ref):
        gv = g_ref[0]
        for k in range(1, 8):
            gv = gv + g_ref[k]
        go_ref[...] = gv
        nm = ADAM_B1 * m_ref[...] + (1.0 - ADAM_B1) * gv
        nv = ADAM_B2 * v_ref[...] + (1.0 - ADAM_B2) * (gv * gv)
        m_hat = nm / (1.0 - ADAM_B1 ** ADAM_STEP)
        v_hat = nv / (1.0 - ADAM_B2 ** ADAM_STEP)
        d_ref[...] = -ADAM_LR * (m_hat / (jnp.sqrt(v_hat) + ADAM_EPS) + ADAM_WD * w_ref[...])
        nm_ref[...] = nm
        nv_ref[...] = nv

    return pl.pallas_call(
        body, name="adamw_small",
        out_shape=[jax.ShapeDtypeStruct((mrows, n), F32)] * 4,
    )(w, gparts, m, v)


def _flat(parts):
    nl = parts[0].shape[0]
    return jnp.concatenate([p.reshape(nl, -1) for p in parts], axis=1)


def _unflat(flat, shapes):
    out, off = [], 0
    nl = flat.shape[0]
    for s in shapes:
        n = math.prod(s)
        out.append(flat[:, off:off + n].reshape((nl,) + tuple(s)))
        off += n
    return out


def kernel(x, w_in, b_gate, q_norm_g, k_norm_g, q_a_norm_g, kv_a_norm_g, w_q_up, w_kv_up, w_branch_a, w_branch_b, w_o, w_ffn_up, w_ffn_down, pre_mix_g, post_mix_g, pre_ffn_g, post_ffn_g, loss_target, m_w_in, m_b_gate, m_q_norm_g, m_k_norm_g, m_q_a_norm_g, m_kv_a_norm_g, m_w_q_up, m_w_kv_up, m_w_branch_a, m_w_branch_b, m_w_o, m_w_ffn_up, m_w_ffn_down, m_pre_mix_g, m_post_mix_g, m_pre_ffn_g, m_post_ffn_g, v_w_in, v_b_gate, v_q_norm_g, v_k_norm_g, v_q_a_norm_g, v_kv_a_norm_g, v_w_q_up, v_w_kv_up, v_w_branch_a, v_w_branch_b, v_w_o, v_w_ffn_up, v_w_ffn_down, v_pre_mix_g, v_post_mix_g, v_pre_ffn_g, v_post_ffn_g):
    wts = dict(w_in=w_in, b_gate=b_gate, q_norm_g=q_norm_g, k_norm_g=k_norm_g, q_a_norm_g=q_a_norm_g,
               kv_a_norm_g=kv_a_norm_g, w_q_up=w_q_up, w_kv_up=w_kv_up, w_branch_a=w_branch_a, w_branch_b=w_branch_b,
               w_o=w_o, w_ffn_up=w_ffn_up, w_ffn_down=w_ffn_down, pre_mix_g=pre_mix_g, post_mix_g=post_mix_g,
               pre_ffn_g=pre_ffn_g, post_ffn_g=post_ffn_g)
    mom = dict(w_in=m_w_in, b_gate=m_b_gate, q_norm_g=m_q_norm_g, k_norm_g=m_k_norm_g, q_a_norm_g=m_q_a_norm_g,
               kv_a_norm_g=m_kv_a_norm_g, w_q_up=m_w_q_up, w_kv_up=m_w_kv_up, w_branch_a=m_w_branch_a,
               w_branch_b=m_w_branch_b, w_o=m_w_o, w_ffn_up=m_w_ffn_up, w_ffn_down=m_w_ffn_down, pre_mix_g=m_pre_mix_g,
               post_mix_g=m_post_mix_g, pre_ffn_g=m_pre_ffn_g, post_ffn_g=m_post_ffn_g)
    var = dict(w_in=v_w_in, b_gate=v_b_gate, q_norm_g=v_q_norm_g, k_norm_g=v_k_norm_g, q_a_norm_g=v_q_a_norm_g,
               kv_a_norm_g=v_kv_a_norm_g, w_q_up=v_w_q_up, w_kv_up=v_w_kv_up, w_branch_a=v_w_branch_a,
               w_branch_b=v_w_branch_b, w_o=v_w_o, w_ffn_up=v_w_ffn_up, w_ffn_down=v_w_ffn_down, pre_mix_g=v_pre_mix_g,
               post_mix_g=v_post_mix_g, pre_ffn_g=v_pre_ffn_g, post_ffn_g=v_post_ffn_g)
    nl = w_in.shape[0]

    shards = {n: wts[n].astype(BF) for n in BIG}
    smalls = [{n: wts[n][li] for n in SMALL} for li in range(nl)]
    loss_local, dx, received, small_grads = _local_step(x[0], loss_target[0], shards, smalls)
    loss = lax.psum(loss_local, ("x", "y", "c"))
    part = []
    for i, n in enumerate(BIG):
        got = jnp.stack([received[li][i] for li in range(nl)], axis=1)
        part.append(_sum_slabs(got.reshape(4, -1, got.shape[-1]), "sum_chips_" + n))
    other = _sibling_exchange(part, "swap_cores")

    small_shapes = [wts[n].shape[1:] for n in SMALL]
    g_loc = _flat([jnp.stack([small_grads[li][n] for li in range(nl)]) for n in SMALL]).reshape(-1, 128)
    g_all = _allgather_small(g_loc).reshape(8, -1, 128)
    pack = lambda d: _flat([d[n] for n in SMALL]).reshape(-1, 128)
    gs, ds, ms, vs = _adamw_small(pack(wts), g_all, pack(mom), pack(var))
    unpack = lambda a: dict(zip(SMALL, _unflat(a.reshape(nl, -1), small_shapes)))
    g_small, d_small, m_small, v_small = unpack(gs), unpack(ds), unpack(ms), unpack(vs)

    out_g, out_d, out_m, out_v = dict(g_small), dict(d_small), dict(m_small), dict(v_small)
    for i, n in enumerate(BIG):
        shp = wts[n].shape
        two = lambda a: a.reshape(-1, shp[-1])
        g, d, nm, nv = _adamw(two(wts[n]), part[i], other[i], two(mom[n]), two(var[n]), "adamw_" + n)
        out_g[n], out_d[n], out_m[n], out_v[n] = g.reshape(shp), d.reshape(shp), nm.reshape(shp), nv.reshape(shp)

    return (loss, dx[None], *[out_g[n] for n in ORDER], *[out_d[n] for n in ORDER], *[out_m[n] for n in ORDER],
            *[out_v[n] for n in ORDER])
```

```python
import functools
import math

import jax
import jax.numpy as jnp
from jax import lax
from jax.experimental import pallas as pl
from jax.experimental.pallas import tpu as pltpu

F32 = jnp.float32
BF = jnp.bfloat16
MESH = pl.DeviceIdType.MESH

EPS = 1e-6
D = 1024
NH = 8
HW = 128
GQA_KV = 2
GQA_G = 4
HEAD = 64
MLA_NOPE = 64
MLA_ROPE = 32
MLA_QK = 96
Q_RANK = 384
KV_RANK = 256
DFF = 4096
GRID_W = 64
ROPE_THETA = 10000.0

O_QA, O_KA, O_VA, O_CQ, O_CKV, O_KR, ZM_W = 0, 1024, 1280, 1536, 1920, 2176, 2304
ZG_W = 2048

ADAM_LR, ADAM_B1, ADAM_B2, ADAM_EPS, ADAM_WD, ADAM_STEP = 0.001, 0.9, 0.999, 1e-08, 0.01, 10

TM = 256
TQ = 256
TQ_FWD_STEP = 1024
TQ_BWD_STEP = 512
ONES_LANE = 64
LOG2E = 1.4426950408889634
LN2 = 0.6931471805599453
SCALE_GQA = 1.0 / math.sqrt(HEAD)
SCALE_MLA = 1.0 / math.sqrt(MLA_QK)
C2_GQA = SCALE_GQA * LOG2E
C2_MLA = SCALE_MLA * LOG2E
VMEM_LIMIT = 56 * 1024 * 1024
TN_ROWS_NARROW, TN_ROWS_WIDE = 2048, 1024

BIG = ("w_in", "w_q_up", "w_kv_up", "w_branch_a", "w_branch_b", "w_o", "w_ffn_up", "w_ffn_down")
FIRST = BIG[:3]
REST = BIG[3:]
SMALL = ("b_gate", "q_norm_g", "k_norm_g", "q_a_norm_g", "kv_a_norm_g", "pre_mix_g", "post_mix_g", "pre_ffn_g", "post_ffn_g")
ORDER = ("w_in", "b_gate", "q_norm_g", "k_norm_g", "q_a_norm_g", "kv_a_norm_g", "w_q_up", "w_kv_up", "w_branch_a",
         "w_branch_b", "w_o", "w_ffn_up", "w_ffn_down", "pre_mix_g", "post_mix_g", "pre_ffn_g", "post_ffn_g")


def _params(sem=None):
    return pltpu.CompilerParams(dimension_semantics=sem, vmem_limit_bytes=VMEM_LIMIT)


def _rows(tm, w):
    return pl.BlockSpec((tm, w), lambda i: (i, 0))


def _cols(h, tm):
    return pl.BlockSpec((h, tm), lambda i: (0, i))


def _whole(shape):
    return pl.BlockSpec(shape, lambda i: (0,) * len(shape))


def _dot(a, b):
    return jnp.dot(a, b, preferred_element_type=F32)


def _dot_nt(a, b):
    return lax.dot_general(a, b, (((1,), (1,)), ((), ())), preferred_element_type=F32)


def _dot_tn(a, b):
    return lax.dot_general(a, b, (((0,), (0,)), ((), ())), preferred_element_type=F32)


def _norm_fwd(xv, g, n=None):
    n = xv.shape[-1] if n is None else n
    r = lax.rsqrt(jnp.sum(xv * xv, axis=-1, keepdims=True) * (1.0 / n) + EPS)
    return (xv * r) * g


def _norm_bwd(xv, g, dy, n=None):
    n = xv.shape[-1] if n is None else n
    r = lax.rsqrt(jnp.sum(xv * xv, axis=-1, keepdims=True) * (1.0 / n) + EPS)
    xh = xv * r
    dxh = dy * g
    dg = jnp.sum(dy * xh, axis=0, keepdims=True)
    dx = r * (dxh - xh * (jnp.sum(dxh * xh, axis=-1, keepdims=True) * (1.0 / n)))
    return dx, dg


def _accumulate(ref, val):
    @pl.when(pl.program_id(0) == 0)
    def _():
        ref[...] = jnp.zeros_like(ref)

    ref[...] += val


def _rot(xv, q, neg):
    w = xv.shape[-1]
    return jnp.where(neg, -pltpu.roll(xv, w - q, 1), pltpu.roll(xv, q, 1))


def _rope(xv, c, s, q, neg):
    return xv * c + _rot(xv, q, neg) * s


def _rope_bwd(dy, c, s, q, neg):
    return dy * c - _rot(dy * s, q, neg)


def _neg_mask(tm, q):
    lane = lax.broadcasted_iota(jnp.int32, (tm, HW), 1)
    return (lane % (2 * q)) < q


def _sigmoid(z):
    return 1.0 / (1.0 + jnp.exp(-z))


def _fwd_inproj(x, g1, w1, w2):
    t = x.shape[0]

    def body(x_ref, g_ref, w1_ref, w2_ref, u_ref, zm_ref, zg_ref):
        u = _norm_fwd(x_ref[...], g_ref[...]).astype(BF)
        u_ref[...] = u
        for c in range(0, ZM_W, 768):
            zm_ref[:, c:c + 768] = _dot(u, w1_ref[:, c:c + 768])
        for c in range(0, ZG_W, 512):
            zg_ref[:, c:c + 512] = _dot(u, w2_ref[:, c:c + 512])

    return pl.pallas_call(
        body, name="fwd_inproj", grid=(t // TM,),
        in_specs=[_rows(TM, D), _whole((1, D)), _whole((D, ZM_W)), _whole((D, ZG_W))],
        out_specs=[_rows(TM, D), _rows(TM, ZM_W), _rows(TM, ZG_W)],
        out_shape=[jax.ShapeDtypeStruct((t, D), BF), jax.ShapeDtypeStruct((t, ZM_W), F32),
                   jax.ShapeDtypeStruct((t, ZG_W), F32)],
        compiler_params=_params(("parallel",)),
    )(x, g1, w1, w2)


def _fwd_mid(zm, gq, gk, gqa, gkva, wq, wk, wv, ca, sa, cb, sb):
    t = zm.shape[0]

    def body(zm_ref, gq_ref, gk_ref, gqa_ref, gkva_ref, wq_ref, wk_ref, wv_ref, ca_ref, sa_ref, cb_ref, sb_ref,
             qa_ref, ka_ref, va_ref, qb_ref, kb_ref, vb_ref, cqn_ref, ckvn_ref):
        ca_, sa_, cb_, sb_ = ca_ref[...], sa_ref[...], cb_ref[...], sb_ref[...]
        nega, negb = _neg_mask(TM, 16), _neg_mask(TM, 8)
        for h in range(NH):
            xv = zm_ref[:, O_QA + h * HW:O_QA + (h + 1) * HW]
            qa_ref[:, h * HW:(h + 1) * HW] = (_rope(_norm_fwd(xv, gq_ref[...], HEAD), ca_, sa_, 16, nega) * C2_GQA).astype(BF)
        for h in range(GQA_KV):
            xv = zm_ref[:, O_KA + h * HW:O_KA + (h + 1) * HW]
            ka_ref[:, h * HW:(h + 1) * HW] = _rope(_norm_fwd(xv, gk_ref[...], HEAD), ca_, sa_, 16, nega).astype(BF)
        ones = lax.broadcasted_iota(jnp.int32, (TM, HW), 1) == ONES_LANE
        for h in range(GQA_KV):
            va_ref[:, h * HW:(h + 1) * HW] = jnp.where(ones, 1.0, zm_ref[:, O_VA + h * HW:O_VA + (h + 1) * HW]).astype(BF)
        cqn = _norm_fwd(zm_ref[:, O_CQ:O_CQ + Q_RANK], gqa_ref[...]).astype(BF)
        ckvn = _norm_fwd(zm_ref[:, O_CKV:O_CKV + KV_RANK], gkva_ref[...]).astype(BF)
        cqn_ref[...] = cqn
        ckvn_ref[...] = ckvn
        qb = _dot(cqn, wq_ref[...])
        kpre = _dot(ckvn, wk_ref[...])
        kr = _rope(zm_ref[:, O_KR:O_KR + HW], cb_, sb_, 8, negb)
        for h in range(NH):
            sl = slice(h * HW, (h + 1) * HW)
            qb_ref[:, sl] = (_rope(qb[:, sl], cb_, sb_, 8, negb) * C2_MLA).astype(BF)
            kb_ref[:, sl] = (kpre[:, sl] + kr).astype(BF)
        vb = _dot(ckvn, wv_ref[...])
        for h in range(NH):
            vb_ref[:, h * HW:(h + 1) * HW] = jnp.where(ones, 1.0, vb[:, h * HW:(h + 1) * HW]).astype(BF)

    hw8 = NH * HW
    return pl.pallas_call(
        body, name="fwd_mid", grid=(t // TM,),
        in_specs=[_rows(TM, ZM_W), _whole((1, HW)), _whole((1, HW)), _whole((1, Q_RANK)), _whole((1, KV_RANK)),
                  _whole((Q_RANK, hw8)), _whole((KV_RANK, hw8)), _whole((KV_RANK, hw8)),
                  _rows(TM, HW), _rows(TM, HW), _rows(TM, HW), _rows(TM, HW)],
        out_specs=[_rows(TM, hw8), _rows(TM, GQA_KV * HW), _rows(TM, GQA_KV * HW), _rows(TM, hw8), _rows(TM, hw8),
                   _rows(TM, hw8), _rows(TM, Q_RANK), _rows(TM, KV_RANK)],
        out_shape=[jax.ShapeDtypeStruct((t, hw8), BF), jax.ShapeDtypeStruct((t, GQA_KV * HW), BF),
                   jax.ShapeDtypeStruct((t, GQA_KV * HW), BF), jax.ShapeDtypeStruct((t, hw8), BF),
                   jax.ShapeDtypeStruct((t, hw8), BF), jax.ShapeDtypeStruct((t, hw8), BF),
                   jax.ShapeDtypeStruct((t, Q_RANK), BF), jax.ShapeDtypeStruct((t, KV_RANK), BF)],
        compiler_params=_params(("parallel",)),
    )(zm, gq, gk, gqa, gkva, wq, wk, wv, ca, sa, cb, sb)


class _ChipExchange:
    def __init__(self, srcs, per_dest):
        self.srcs, self.per_dest, self.n = list(srcs), per_dest, len(srcs)
        self.out_shape = [jax.ShapeDtypeStruct((4,) + tuple(s.shape[1:] if per_dest else s.shape), s.dtype) for s in srcs]
        self.specs = [pl.BlockSpec(memory_space=pl.ANY)] * self.n
        self.scratch = [pltpu.SemaphoreType.DMA((3 * self.n,)), pltpu.SemaphoreType.DMA((3 * self.n,)),
                        pltpu.SemaphoreType.DMA((self.n,))]

    def _copies(self, src_refs, out_refs, send_sems, recv_sems, local_sems):
        x, y, c = lax.axis_index("x"), lax.axis_index("y"), lax.axis_index("c")
        me = 2 * x + y
        chips = [(1 - x, y), (x, 1 - y), (1 - x, 1 - y)]

        def piece(a, k):
            return src_refs[a].at[k] if self.per_dest else src_refs[a]

        def remote(a, j, src_chip, dst_slab):
            px, py = chips[j]
            return pltpu.make_async_remote_copy(
                src_ref=piece(a, src_chip), dst_ref=out_refs[a].at[dst_slab], send_sem=send_sems.at[3 * a + j],
                recv_sem=recv_sems.at[3 * a + j], device_id=(px, py, c), device_id_type=MESH)

        local = [pltpu.make_async_copy(piece(a, me), out_refs[a].at[me], local_sems.at[a]) for a in range(self.n)]
        sends = [remote(a, j, 2 * chips[j][0] + chips[j][1], me) for a in range(self.n) for j in range(3)]
        recvs = [remote(a, j, me, 2 * chips[j][0] + chips[j][1]) for a in range(self.n) for j in range(3)]
        return local, sends, recvs

    def start(self, *refs):
        local, sends, _ = self._copies(*refs)
        for cp in local + sends:
            cp.start()

    def finish(self, *refs):
        local, sends, recvs = self._copies(*refs)
        for cp in recvs:
            cp.wait_recv()
        for cp in sends:
            cp.wait_send()
        for cp in local:
            cp.wait()


def _hosted(exchange, refs, n_in, n_out):
    n = exchange.n if exchange else 0
    ins, srcs = refs[:n_in], refs[n_in:n_in + n]
    outs = refs[n_in + n:n_in + n + n_out]
    rest = refs[n_in + n + n_out:]
    return ins, outs, (tuple(srcs), tuple(rest[:n])) + tuple(rest[n:])


def _flash_fwd(q, k, v, group, name, exchange=None):
    t = q.shape[0]
    tq = min(TQ_FWD_STEP, t)
    nq = t // tq

    def body(*refs):
        (q_ref, k_ref, v_ref), (o_ref, lse_ref), ex_refs = _hosted(exchange, refs, 3, 2)
        if exchange:
            @pl.when((pl.program_id(0) == 0) & (pl.program_id(1) == 0))
            def _():
                exchange.start(*ex_refs)

        for r0 in range(0, tq, TQ):
            rows = slice(r0, r0 + TQ)
            s = _dot_nt(q_ref[rows, :], k_ref[...])
            m = jnp.max(s, axis=-1, keepdims=True)
            acc = _dot(jnp.exp2(s - m).astype(BF), v_ref[...])
            l = acc[:, ONES_LANE:ONES_LANE + 1]
            o_ref[rows, :] = acc / l
            lse_ref[rows, :] = jnp.broadcast_to(m + jnp.log2(l), (TQ, HW))
        if exchange:
            @pl.when((pl.program_id(0) == NH - 1) & (pl.program_id(1) == nq - 1))
            def _():
                exchange.finish(*ex_refs)

    qspec = pl.BlockSpec((tq, HW), lambda h, i: (i, h))
    kspec = pl.BlockSpec((t, HW), lambda h, i: (0, h // group))
    out = pl.pallas_call(
        body, name=name, grid=(NH, nq),
        in_specs=[qspec, kspec, kspec] + (exchange.specs if exchange else []),
        out_specs=[qspec, qspec] + (exchange.specs if exchange else []),
        out_shape=[jax.ShapeDtypeStruct((t, NH * HW), F32), jax.ShapeDtypeStruct((t, NH * HW), F32)]
        + (exchange.out_shape if exchange else []),
        scratch_shapes=exchange.scratch if exchange else [],
        compiler_params=_params(("arbitrary", "arbitrary") if exchange else ("parallel", "parallel")),
    )(q, k, v, *(exchange.srcs if exchange else []))
    return out[0], out[1], out[2:]


def _fwd_merge(ya, yb, zg, b, wa, wb, wo, x, g2):
    t = x.shape[0]

    def body(ya_ref, yb_ref, zg_ref, b_ref, wa_ref, wb_ref, wo_ref, x_ref, g_ref, pa_ref, pb_ref, mg_ref, m_ref, x1_ref):
        pa = _dot(ya_ref[...].astype(BF), wa_ref[...])
        pb = _dot(yb_ref[...].astype(BF), wb_ref[...])
        pa_ref[...] = pa
        pb_ref[...] = pb
        ga = _sigmoid(zg_ref[:, :D] + b_ref[:, :D])
        gb = _sigmoid(zg_ref[:, D:] + b_ref[:, D:])
        mg = (ga * pa + gb * pb).astype(BF)
        mg_ref[...] = mg
        m = _dot(mg, wo_ref[...])
        m_ref[...] = m
        x1_ref[...] = x_ref[...] + _norm_fwd(m, g_ref[...])

    return pl.pallas_call(
        body, name="fwd_merge", grid=(t // TM,),
        in_specs=[_rows(TM, D), _rows(TM, D), _rows(TM, ZG_W), _whole((1, ZG_W)), _whole((D, D)), _whole((D, D)),
                  _whole((D, D)), _rows(TM, D), _whole((1, D))],
        out_specs=[_rows(TM, D), _rows(TM, D), _rows(TM, D), _rows(TM, D), _rows(TM, D)],
        out_shape=[jax.ShapeDtypeStruct((t, D), F32), jax.ShapeDtypeStruct((t, D), F32), jax.ShapeDtypeStruct((t, D), BF),
                   jax.ShapeDtypeStruct((t, D), F32), jax.ShapeDtypeStruct((t, D), F32)],
        compiler_params=_params(("parallel",)),
    )(ya, yb, zg, b, wa, wb, wo, x, g2)


def _fwd_ffn_up(x1, g3, wup):
    t = x1.shape[0]

    def body(x_ref, g_ref, w_ref, hn_ref, h_ref, a_ref):
        hn = _norm_fwd(x_ref[...], g_ref[...]).astype(BF)
        hn_ref[...] = hn
        for c in range(0, DFF, 1024):
            h = _dot(hn, w_ref[:, c:c + 1024])
            h_ref[:, c:c + 1024] = h
            r = jnp.maximum(h, 0.0)
            a_ref[:, c:c + 1024] = (r * r).astype(BF)

    return pl.pallas_call(
        body, name="fwd_ffn_up", grid=(t // TM,),
        in_specs=[_rows(TM, D), _whole((1, D)), _whole((D, DFF))],
        out_specs=[_rows(TM, D), _rows(TM, DFF), _rows(TM, DFF)],
        out_shape=[jax.ShapeDtypeStruct((t, D), BF), jax.ShapeDtypeStruct((t, DFF), F32), jax.ShapeDtypeStruct((t, DFF), BF)],
        compiler_params=_params(("parallel",)),
    )(x1, g3, wup)


def _fwd_ffn_down(a, wdn, x1, g4):
    t = x1.shape[0]

    def body(a_ref, w_ref, x_ref, g_ref, f_ref, x2_ref):
        f = _dot(a_ref[...], w_ref[...])
        f_ref[...] = f
        x2_ref[...] = x_ref[...] + _norm_fwd(f, g_ref[...])

    return pl.pallas_call(
        body, name="fwd_ffn_down", grid=(t // TM,),
        in_specs=[_rows(TM, DFF), _whole((DFF, D)), _rows(TM, D), _whole((1, D))],
        out_specs=[_rows(TM, D), _rows(TM, D)],
        out_shape=[jax.ShapeDtypeStruct((t, D), F32), jax.ShapeDtypeStruct((t, D), F32)],
        compiler_params=_params(("parallel",)),
    )(a, wdn, x1, g4)


def _loss_head(y, target):
    t = y.shape[0]

    def body(y_ref, t_ref, dy_ref, loss_ref):
        d = y_ref[...] - t_ref[...]
        dy_ref[...] = d * (1.0 / D)
        part = 0.5 * jnp.sum(jnp.sum(d * d, axis=-1, keepdims=True) * (1.0 / D), axis=0, keepdims=True)
        _accumulate(loss_ref, jnp.broadcast_to(part, (8, HW)))

    return pl.pallas_call(
        body, name="loss_head", grid=(t // TM,),
        in_specs=[_rows(TM, D), _rows(TM, D)],
        out_specs=[_rows(TM, D), _whole((8, HW))],
        out_shape=[jax.ShapeDtypeStruct((t, D), F32), jax.ShapeDtypeStruct((8, HW), F32)],
        compiler_params=_params(("arbitrary",)),
    )(y, target)


def _bwd_ffn_down(f, dx2, g4, wdn, h):
    t = f.shape[0]

    def body(f_ref, dx2_ref, g_ref, w_ref, h_ref, df_ref, dh_ref, dg_ref):
        df, dg = _norm_bwd(f_ref[...], g_ref[...], dx2_ref[...])
        _accumulate(dg_ref, dg)
        df16 = df.astype(BF)
        df_ref[...] = df16
        for c in range(0, DFF, 1024):
            da = _dot_nt(df16, w_ref[c:c + 1024, :])
            dh_ref[:, c:c + 1024] = (da * (2.0 * jnp.maximum(h_ref[:, c:c + 1024], 0.0))).astype(BF)

    return pl.pallas_call(
        body, name="bwd_ffn_down", grid=(t // TM,),
        in_specs=[_rows(TM, D), _rows(TM, D), _whole((1, D)), _whole((DFF, D)), _rows(TM, DFF)],
        out_specs=[_rows(TM, D), _rows(TM, DFF), _whole((1, D))],
        out_shape=[jax.ShapeDtypeStruct((t, D), BF), jax.ShapeDtypeStruct((t, DFF), BF), jax.ShapeDtypeStruct((1, D), F32)],
        compiler_params=_params(("arbitrary",)),
    )(f, dx2, g4, wdn, h)


def _bwd_ffn_up(dh, wup, x1, g3, dx2):
    t = x1.shape[0]

    def body(dh_ref, w_ref, x_ref, g_ref, dx2_ref, dx1_ref, dg_ref):
        dhn = _dot_nt(dh_ref[...], w_ref[...])
        dx, dg = _norm_bwd(x_ref[...], g_ref[...], dhn)
        _accumulate(dg_ref, dg)
        dx1_ref[...] = dx2_ref[...] + dx

    return pl.pallas_call(
        body, name="bwd_ffn_up", grid=(t // TM,),
        in_specs=[_rows(TM, DFF), _whole((D, DFF)), _rows(TM, D), _whole((1, D)), _rows(TM, D)],
        out_specs=[_rows(TM, D), _whole((1, D))],
        out_shape=[jax.ShapeDtypeStruct((t, D), F32), jax.ShapeDtypeStruct((1, D), F32)],
        compiler_params=_params(("arbitrary",)),
    )(dh, wup, x1, g3, dx2)


def _bwd_merge(m, dx1, g2, wo, zg, b, pa, pb, wa, wb):
    t = m.shape[0]

    def body(m_ref, dx1_ref, g_ref, wo_ref, zg_ref, b_ref, pa_ref, pb_ref, wa_ref, wb_ref,
             dm_ref, dpa_ref, dpb_ref, dzg_ref, dya_ref, dyb_ref, dg_ref, db_ref):
        dm, dg = _norm_bwd(m_ref[...], g_ref[...], dx1_ref[...])
        _accumulate(dg_ref, dg)
        dm16 = dm.astype(BF)
        dm_ref[...] = dm16
        dmg = _dot_nt(dm16, wo_ref[...])
        ga = _sigmoid(zg_ref[:, :D] + b_ref[:, :D])
        gb = _sigmoid(zg_ref[:, D:] + b_ref[:, D:])
        dpa = (dmg * ga).astype(BF)
        dpb = (dmg * gb).astype(BF)
        dpa_ref[...] = dpa
        dpb_ref[...] = dpb
        dza = (dmg * pa_ref[...]) * (ga * (1.0 - ga))
        dzb = (dmg * pb_ref[...]) * (gb * (1.0 - gb))
        dzg_ref[:, :D] = dza.astype(BF)
        dzg_ref[:, D:] = dzb.astype(BF)

        @pl.when(pl.program_id(0) == 0)
        def _():
            db_ref[...] = jnp.zeros_like(db_ref)

        db_ref[:, :D] += jnp.sum(dza, axis=0, keepdims=True)
        db_ref[:, D:] += jnp.sum(dzb, axis=0, keepdims=True)
        dya_ref[...] = _dot_nt(dpa, wa_ref[...]).astype(BF)
        dyb_ref[...] = _dot_nt(dpb, wb_ref[...]).astype(BF)

    return pl.pallas_call(
        body, name="bwd_merge", grid=(t // TM,),
        in_specs=[_rows(TM, D), _rows(TM, D), _whole((1, D)), _whole((D, D)), _rows(TM, ZG_W), _whole((1, ZG_W)),
                  _rows(TM, D), _rows(TM, D), _whole((D, D)), _whole((D, D))],
        out_specs=[_rows(TM, D), _rows(TM, D), _rows(TM, D), _rows(TM, ZG_W), _rows(TM, D), _rows(TM, D),
                   _whole((1, D)), _whole((1, ZG_W))],
        out_shape=[jax.ShapeDtypeStruct((t, D), BF), jax.ShapeDtypeStruct((t, D), BF), jax.ShapeDtypeStruct((t, D), BF),
                   jax.ShapeDtypeStruct((t, ZG_W), BF), jax.ShapeDtypeStruct((t, D), BF), jax.ShapeDtypeStruct((t, D), BF),
                   jax.ShapeDtypeStruct((1, D), F32), jax.ShapeDtypeStruct((1, ZG_W), F32)],
        compiler_params=_params(("arbitrary",)),
    )(m, dx1, g2, wo, zg, b, pa, pb, wa, wb)


def _flash_bwd(q, do, o, lse, k, v, group, scale, name, exchange=None):
    t = q.shape[0]
    tq = min(TQ_BWD_STEP, t)
    nq = t // tq
    nkv = NH // group

    def body(*refs):
        (q_ref, do_ref, o_ref, lse_ref, k_ref, v_ref), (dq_ref, dkt_ref, dvt_ref), ex_refs = _hosted(exchange, refs, 6, 3)
        first = (pl.program_id(1) == 0) & (pl.program_id(2) == 0)
        if exchange:
            @pl.when(first & (pl.program_id(0) == 0))
            def _():
                exchange.start(*ex_refs)

        @pl.when(first)
        def _():
            dkt_ref[...] = jnp.zeros_like(dkt_ref)
            dvt_ref[...] = jnp.zeros_like(dvt_ref)

        dvt = dkt = None
        for r0 in range(0, tq, TQ):
            rows = slice(r0, r0 + TQ)
            qv = q_ref[rows, :]
            dov = do_ref[rows, :]
            dsum = jnp.sum(dov.astype(F32) * o_ref[rows, :], axis=-1, keepdims=True)
            p = jnp.exp2(_dot_nt(qv, k_ref[...]) - lse_ref[rows, :1])
            ds = (p * (_dot_nt(dov, v_ref[...]) - dsum)).astype(BF)
            dq_ref[rows, :] = _dot(ds, k_ref[...]) * scale
            dvt_tile = _dot(dov.astype(F32).T.astype(BF), p.astype(BF))
            dkt_tile = _dot(qv.astype(F32).T.astype(BF), ds)
            dvt = dvt_tile if dvt is None else dvt + dvt_tile
            dkt = dkt_tile if dkt is None else dkt + dkt_tile
        dvt_ref[...] += dvt
        dkt_ref[...] += dkt * LN2
        if exchange:
            @pl.when((pl.program_id(0) == nkv - 1) & (pl.program_id(1) == group - 1) & (pl.program_id(2) == nq - 1))
            def _():
                exchange.finish(*ex_refs)

    qspec = pl.BlockSpec((tq, HW), lambda hk, g, i: (i, hk * group + g))
    kspec = pl.BlockSpec((t, HW), lambda hk, g, i: (0, hk))
    tspec = pl.BlockSpec((HW, t), lambda hk, g, i: (hk, 0))
    out = pl.pallas_call(
        body, name=name, grid=(nkv, group, nq),
        in_specs=[qspec, qspec, qspec, qspec, kspec, kspec] + (exchange.specs if exchange else []),
        out_specs=[qspec, tspec, tspec] + (exchange.specs if exchange else []),
        out_shape=[jax.ShapeDtypeStruct((t, NH * HW), F32), jax.ShapeDtypeStruct((nkv * HW, t), F32),
                   jax.ShapeDtypeStruct((nkv * HW, t), F32)] + (exchange.out_shape if exchange else []),
        scratch_shapes=exchange.scratch if exchange else [],
        compiler_params=_params(("arbitrary" if exchange else "parallel", "arbitrary", "arbitrary")),
    )(q, do, o, lse, k, v, *(exchange.srcs if exchange else []))
    return out[0], out[1], out[2], out[3:]


def _bwd_mid(zm, dqa, dkta, dvta, dqb, dktb, dvtb, gq, gk, gqa, gkva, wq, wk, wv, ca, sa, cb, sb):
    t = zm.shape[0]
    hw8 = NH * HW

    def body(zm_ref, dqa_ref, dkta_ref, dvta_ref, dqb_ref, dktb_ref, dvtb_ref, gq_ref, gk_ref, gqa_ref, gkva_ref,
             wq_ref, wk_ref, wv_ref, ca_ref, sa_ref, cb_ref, sb_ref,
             dzm_ref, dqbp_ref, dkb16_ref, dvb16_ref, dgq_ref, dgk_ref, dgqa_ref, dgkva_ref):
        ca_, sa_, cb_, sb_ = ca_ref[...], sa_ref[...], cb_ref[...], sb_ref[...]
        nega, negb = _neg_mask(TM, 16), _neg_mask(TM, 8)
        dgq = jnp.zeros((1, HW), F32)
        for h in range(NH):
            sl = slice(h * HW, (h + 1) * HW)
            dqn = _rope_bwd(dqa_ref[:, sl], ca_, sa_, 16, nega)
            dx, dg = _norm_bwd(zm_ref[:, O_QA + h * HW:O_QA + (h + 1) * HW], gq_ref[...], dqn, HEAD)
            dzm_ref[:, O_QA + h * HW:O_QA + (h + 1) * HW] = dx.astype(BF)
            dgq = dgq + dg
        _accumulate(dgq_ref, dgq)
        dgk = jnp.zeros((1, HW), F32)
        for j in range(GQA_KV):
            dk = dkta_ref[j * HW:(j + 1) * HW, :].T
            dv = dvta_ref[j * HW:(j + 1) * HW, :].T
            dkn = _rope_bwd(dk, ca_, sa_, 16, nega)
            dx, dg = _norm_bwd(zm_ref[:, O_KA + j * HW:O_KA + (j + 1) * HW], gk_ref[...], dkn, HEAD)
            dzm_ref[:, O_KA + j * HW:O_KA + (j + 1) * HW] = dx.astype(BF)
            dzm_ref[:, O_VA + j * HW:O_VA + (j + 1) * HW] = dv.astype(BF)
            dgk = dgk + dg
        _accumulate(dgk_ref, dgk)
        for h in range(NH):
            sl = slice(h * HW, (h + 1) * HW)
            dqbp_ref[:, sl] = _rope_bwd(dqb_ref[:, sl], cb_, sb_, 8, negb).astype(BF)
        dcqn = _dot_nt(dqbp_ref[...], wq_ref[...])
        dx, dg = _norm_bwd(zm_ref[:, O_CQ:O_CQ + Q_RANK], gqa_ref[...], dcqn)
        dzm_ref[:, O_CQ:O_CQ + Q_RANK] = dx.astype(BF)
        _accumulate(dgqa_ref, dg)
        dkr = jnp.zeros((TM, HW), F32)
        for h in range(NH):
            sl = slice(h * HW, (h + 1) * HW)
            dkh = dktb_ref[sl, :].T
            dkr = dkr + dkh
            dkb16_ref[:, sl] = dkh.astype(BF)
            dvb16_ref[:, sl] = dvtb_ref[sl, :].T.astype(BF)
        dkb16 = dkb16_ref[...]
        dvb16 = dvb16_ref[...]
        lane = lax.broadcasted_iota(jnp.int32, (TM, HW), 1)
        in_rope = (lane >= MLA_NOPE) & (lane < MLA_QK)
        dzm_ref[:, O_KR:O_KR + HW] = jnp.where(in_rope, _rope_bwd(dkr, cb_, sb_, 8, negb), 0.0).astype(BF)
        dckvn = _dot_nt(dkb16, wk_ref[...]) + _dot_nt(dvb16, wv_ref[...])
        dx, dg = _norm_bwd(zm_ref[:, O_CKV:O_CKV + KV_RANK], gkva_ref[...], dckvn)
        dzm_ref[:, O_CKV:O_CKV + KV_RANK] = dx.astype(BF)
        _accumulate(dgkva_ref, dg)

    return pl.pallas_call(
        body, name="bwd_mid", grid=(t // TM,),
        in_specs=[_rows(TM, ZM_W), _rows(TM, hw8), _cols(GQA_KV * HW, TM), _cols(GQA_KV * HW, TM), _rows(TM, hw8),
                  _cols(hw8, TM), _cols(hw8, TM), _whole((1, HW)), _whole((1, HW)), _whole((1, Q_RANK)), _whole((1, KV_RANK)),
                  _whole((Q_RANK, hw8)), _whole((KV_RANK, hw8)), _whole((KV_RANK, hw8)),
                  _rows(TM, HW), _rows(TM, HW), _rows(TM, HW), _rows(TM, HW)],
        out_specs=[_rows(TM, ZM_W), _rows(TM, hw8), _rows(TM, hw8), _rows(TM, hw8),
                   _whole((1, HW)), _whole((1, HW)), _whole((1, Q_RANK)), _whole((1, KV_RANK))],
        out_shape=[jax.ShapeDtypeStruct((t, ZM_W), BF), jax.ShapeDtypeStruct((t, hw8), BF), jax.ShapeDtypeStruct((t, hw8), BF),
                   jax.ShapeDtypeStruct((t, hw8), BF), jax.ShapeDtypeStruct((1, HW), F32), jax.ShapeDtypeStruct((1, HW), F32),
                   jax.ShapeDtypeStruct((1, Q_RANK), F32), jax.ShapeDtypeStruct((1, KV_RANK), F32)],
        compiler_params=_params(("arbitrary",)),
    )(zm, dqa, dkta, dvta, dqb, dktb, dvtb, gq, gk, gqa, gkva, wq, wk, wv, ca, sa, cb, sb)


def _bwd_inproj(dzm, dzg, w1, w2, x, g1, dx1):
    t = x.shape[0]

    def body(dzm_ref, dzg_ref, w1_ref, w2_ref, x_ref, g_ref, dx1_ref, dx_ref, dg_ref):
        du = _dot_nt(dzm_ref[...], w1_ref[...]) + _dot_nt(dzg_ref[...], w2_ref[...])
        dx, dg = _norm_bwd(x_ref[...], g_ref[...], du)
        _accumulate(dg_ref, dg)
        dx_ref[...] = dx1_ref[...] + dx

    return pl.pallas_call(
        body, name="bwd_inproj", grid=(t // TM,),
        in_specs=[_rows(TM, ZM_W), _rows(TM, ZG_W), _whole((D, ZM_W)), _whole((D, ZG_W)), _rows(TM, D), _whole((1, D)),
                  _rows(TM, D)],
        out_specs=[_rows(TM, D), _whole((1, D))],
        out_shape=[jax.ShapeDtypeStruct((t, D), F32), jax.ShapeDtypeStruct((1, D), F32)],
        compiler_params=_params(("arbitrary",)),
    )(dzm, dzg, w1, w2, x, g1, dx1)


def _matmul_tn(a, b, tn, name):
    t, kdim = a.shape
    n = b.shape[1]
    tm = min(TN_ROWS_NARROW if kdim <= D else TN_ROWS_WIDE, t)
    nsteps = t // tm

    def body(a_ref, b_ref, o_ref, acc_ref):
        i = pl.program_id(1)

        @pl.when(i == 0)
        def _():
            acc_ref[...] = jnp.zeros_like(acc_ref)

        acc_ref[...] += _dot_tn(a_ref[...].astype(BF), b_ref[...])

        @pl.when(i == nsteps - 1)
        def _():
            o_ref[...] = acc_ref[...].astype(BF)

    return pl.pallas_call(
        body, name=name, grid=(n // tn, nsteps),
        in_specs=[pl.BlockSpec((tm, kdim), lambda j, i: (i, 0)), pl.BlockSpec((tm, tn), lambda j, i: (i, j))],
        out_specs=pl.BlockSpec((kdim, tn), lambda j, i: (0, j)),
        out_shape=jax.ShapeDtypeStruct((kdim, n), BF),
        scratch_shapes=[pltpu.VMEM((kdim, tn), F32)],
        compiler_params=_params(("parallel", "arbitrary")),
    )(a, b)


def _rope_tables(t):
    pos = jnp.arange(t, dtype=jnp.int32)
    row = (pos // GRID_W).astype(F32)
    col = (pos % GRID_W).astype(F32)

    def table(rot_dim):
        half = rot_dim // 2
        inv = ROPE_THETA ** (-jnp.arange(0, half, 2, dtype=F32) / half)
        ar = row[:, None] * inv[None, :]
        ac = col[:, None] * inv[None, :]
        ang = jnp.concatenate([ar, ar, ac, ac], axis=-1)
        return jnp.cos(ang), jnp.sin(ang)

    c64, s64 = table(HEAD)
    c32, s32 = table(MLA_ROPE)
    ones = lambda w: jnp.ones((t, w), F32)
    zeros = lambda w: jnp.zeros((t, w), F32)
    ca = jnp.concatenate([c64, ones(HW - HEAD)], axis=1)
    sa = jnp.concatenate([s64, zeros(HW - HEAD)], axis=1)
    cb = jnp.concatenate([ones(MLA_NOPE), c32, ones(HW - MLA_QK)], axis=1)
    sb = jnp.concatenate([zeros(MLA_NOPE), s32, zeros(HW - MLA_QK)], axis=1)
    return ca, sa, cb, sb


def _pad_heads_cols(w, heads, width):
    k = w.shape[0]
    w = w.reshape(k, heads, width)
    return jnp.pad(w, ((0, 0), (0, 0), (0, HW - width))).reshape(k, heads * HW)


def _pad_heads_rows(w, heads, width):
    n = w.shape[1]
    w = w.reshape(heads, width, n)
    return jnp.pad(w, ((0, 0), (0, HW - width), (0, 0))).reshape(heads * HW, n)


def _unpad_heads_cols(w, heads, width):
    k = w.shape[0]
    return w.reshape(k, heads, HW)[:, :, :width].reshape(k, heads * width)


def _unpad_heads_rows(w, heads, width):
    n = w.shape[1]
    return w.reshape(heads, HW, n)[:, :width, :].reshape(heads * width, n)


def _pad_first(w):
    w_in = w["w_in"]
    qa = _pad_heads_cols(w_in[:, 0:512], NH, HEAD)
    ka = _pad_heads_cols(w_in[:, 512:640], GQA_KV, HEAD)
    va = _pad_heads_cols(w_in[:, 640:768], GQA_KV, HEAD)
    cq = w_in[:, 768:1152]
    ckv = w_in[:, 1152:1408]
    kr = jnp.pad(w_in[:, 1408:1440], ((0, 0), (MLA_NOPE, HW - MLA_QK)))
    kvu = w["w_kv_up"].reshape(KV_RANK, NH, 2 * HEAD)
    return dict(
        w1=jnp.concatenate([qa, ka, va, cq, ckv, kr], axis=1),
        w2=w_in[:, 1440:],
        wq=_pad_heads_cols(w["w_q_up"], NH, MLA_QK),
        wk=jnp.pad(kvu[:, :, :HEAD], ((0, 0), (0, 0), (0, HEAD))).reshape(KV_RANK, NH * HW),
        wv=jnp.pad(kvu[:, :, HEAD:], ((0, 0), (0, 0), (0, HEAD))).reshape(KV_RANK, NH * HW),
    )


def _pad_rest(w):
    return dict(
        wa=_pad_heads_rows(w["w_branch_a"], NH, HEAD),
        wb=_pad_heads_rows(w["w_branch_b"], NH, HEAD),
        wo=w["w_o"], wup=w["w_ffn_up"], wdn=w["w_ffn_down"],
    )


def _unpad_first(g):
    d1 = g["w1"]
    w_in = jnp.concatenate([
        _unpad_heads_cols(d1[:, O_QA:O_KA], NH, HEAD), _unpad_heads_cols(d1[:, O_KA:O_VA], GQA_KV, HEAD),
        _unpad_heads_cols(d1[:, O_VA:O_CQ], GQA_KV, HEAD), d1[:, O_CQ:O_CKV], d1[:, O_CKV:O_KR],
        d1[:, O_KR + MLA_NOPE:O_KR + MLA_QK], g["w2"]], axis=1)
    dk = g["wk"].reshape(KV_RANK, NH, HW)[:, :, :HEAD]
    dv = g["wv"].reshape(KV_RANK, NH, HW)[:, :, :HEAD]
    return dict(
        w_in=w_in,
        w_q_up=_unpad_heads_cols(g["wq"], NH, MLA_QK),
        w_kv_up=jnp.concatenate([dk, dv], axis=2).reshape(KV_RANK, NH * 2 * HEAD),
    )


def _unpad_rest(g):
    return dict(
        w_branch_a=_unpad_heads_rows(g["wa"], NH, HEAD),
        w_branch_b=_unpad_heads_rows(g["wb"], NH, HEAD),
        w_o=g["wo"], w_ffn_up=g["wup"], w_ffn_down=g["wdn"],
    )


def _pad_lanes(v, width):
    return jnp.pad(v, (0, HW - width)).reshape(1, HW)


SHARD_AXIS = dict(w_in=1, w_q_up=1, w_kv_up=1, w_branch_a=1, w_branch_b=1, w_o=0, w_ffn_up=1, w_ffn_down=0)


def _join(parts, name):
    if SHARD_AXIS[name] == 0:
        return parts.reshape(-1, parts.shape[-1])
    return jnp.transpose(parts, (1, 0, 2)).reshape(parts.shape[1], -1)


def _split(full, name):
    if SHARD_AXIS[name] == 0:
        return full.reshape(4, -1, full.shape[-1])
    return jnp.transpose(full.reshape(full.shape[0], 4, -1), (1, 0, 2))


def _local_step(x, target, shards, smalls):
    t = x.shape[0]
    nl = len(smalls)
    ca, sa, cb, sb = _rope_tables(t)

    def joined(names, got):
        return {n: _join(g, n) for n, g in zip(names, got)}

    w = _pad_first(joined(FIRST, _chip_exchange([shards[n][0] for n in FIRST], False, "gather_weights")))
    layers, saved = [], []
    for li, s in enumerate(smalls):
        gq, gk = _pad_lanes(s["q_norm_g"], HEAD), _pad_lanes(s["k_norm_g"], HEAD)
        gqa, gkva = s["q_a_norm_g"].reshape(1, -1), s["kv_a_norm_g"].reshape(1, -1)
        g1, g2, g3, g4 = (s[n].reshape(1, D) for n in ("pre_mix_g", "post_mix_g", "pre_ffn_g", "post_ffn_g"))
        b = s["b_gate"].reshape(1, ZG_W)
        u, zm, zg = _fwd_inproj(x, g1, w["w1"], w["w2"])
        qa, ka, va, qb, kb, vb, cqn, ckvn = _fwd_mid(zm, gq, gk, gqa, gkva, w["wq"], w["wk"], w["wv"], ca, sa, cb, sb)
        last = li + 1 == nl
        gather = _ChipExchange([shards[n][li] for n in REST] + ([] if last else [shards[n][li + 1] for n in FIRST]), False)
        ya, lse_a, got = _flash_fwd(qa, ka, va, GQA_G, "flash_fwd_gqa", gather)
        yb, lse_b, _ = _flash_fwd(qb, kb, vb, 1, "flash_fwd_mla")
        w = {**w, **_pad_rest(joined(REST, got[:len(REST)]))}
        pa, pb, mg, m, x1 = _fwd_merge(ya, yb, zg, b, w["wa"], w["wb"], w["wo"], x, g2)
        hn, h, a = _fwd_ffn_up(x1, g3, w["wup"])
        f, x2 = _fwd_ffn_down(a, w["wdn"], x1, g4)
        saved.append(dict(x=x, u=u, zm=zm, zg=zg, qa=qa, ka=ka, va=va, qb=qb, kb=kb, vb=vb, cqn=cqn, ckvn=ckvn,
                          ya=ya, lse_a=lse_a, yb=yb, lse_b=lse_b, pa=pa, pb=pb, mg=mg,
                          m=m, x1=x1, hn=hn, h=h, a=a, f=f, gq=gq, gk=gk, gqa=gqa, gkva=gkva, g1=g1, g2=g2, g3=g3,
                          g4=g4, b=b))
        layers.append(w)
        x = x2
        if not last:
            w = _pad_first(joined(FIRST, got[len(REST):]))

    dx, loss8 = _loss_head(x, target)
    loss = loss8[0, 0]

    received, small_grads, send_first = [dict() for _ in range(nl)], [None] * nl, None
    for li in range(nl - 1, -1, -1):
        w, r = layers[li], saved[li]
        df, dh, dg4 = _bwd_ffn_down(r["f"], dx, r["g4"], w["wdn"], r["h"])
        dx1, dg3 = _bwd_ffn_up(dh, w["wup"], r["x1"], r["g3"], dx)
        dm, dpa, dpb, dzg, dya, dyb, dg2, db = _bwd_merge(r["m"], dx1, r["g2"], w["wo"], r["zg"], r["b"], r["pa"], r["pb"],
                                                         w["wa"], w["wb"])
        nat = _unpad_rest(dict(
            wa=_matmul_tn(r["ya"], dpa, 512, "dw_branch_a"), wb=_matmul_tn(r["yb"], dpb, 512, "dw_branch_b"),
            wo=_matmul_tn(r["mg"], dm, 512, "dw_o"), wup=_matmul_tn(r["hn"], dh, 512, "dw_ffn_up"),
            wdn=_matmul_tn(r["a"], df, 512, "dw_ffn_down")))
        scatter = _ChipExchange([_split(nat[n], n) for n in REST] + (send_first or []), True)
        dqa, dkta, dvta, got = _flash_bwd(r["qa"], dya, r["ya"], r["lse_a"], r["ka"], r["va"], GQA_G, SCALE_GQA,
                                          "flash_bwd_gqa", scatter)
        received[li].update(zip(REST, got[:len(REST)]))
        if send_first:
            received[li + 1].update(zip(FIRST, got[len(REST):]))
        dqb, dktb, dvtb, _ = _flash_bwd(r["qb"], dyb, r["yb"], r["lse_b"], r["kb"], r["vb"], 1, SCALE_MLA, "flash_bwd_mla")
        dzm, dqbp, dkb16, dvb16, dgq, dgk, dgqa, dgkva = _bwd_mid(
            r["zm"], dqa, dkta, dvta, dqb, dktb, dvtb, r["gq"], r["gk"], r["gqa"], r["gkva"], w["wq"], w["wk"], w["wv"],
            ca, sa, cb, sb)
        dx, dg1 = _bwd_inproj(dzm, dzg, w["w1"], w["w2"], r["x"], r["g1"], dx1)
        nat = _unpad_first(dict(
            w1=_matmul_tn(r["u"], dzm, 768, "dw_in_main"), w2=_matmul_tn(r["u"], dzg, 512, "dw_in_gate"),
            wq=_matmul_tn(r["cqn"], dqbp, 512, "dw_q_up"), wk=_matmul_tn(r["ckvn"], dkb16, 512, "dw_k_up"),
            wv=_matmul_tn(r["ckvn"], dvb16, 512, "dw_v_up")))
        send_first = [_split(nat[n], n) for n in FIRST]
        small_grads[li] = dict(b_gate=db[0], q_norm_g=dgq[0, :HEAD], k_norm_g=dgk[0, :HEAD], q_a_norm_g=dgqa[0],
                               kv_a_norm_g=dgkva[0], pre_mix_g=dg1[0], post_mix_g=dg2[0], pre_ffn_g=dg3[0],
                               post_ffn_g=dg4[0])
    received[0].update(zip(FIRST, _chip_exchange(send_first, True, "scatter_grads")))
    return loss, dx, received, small_grads


def _chip_exchange(srcs, per_dest, name):
    ex = _ChipExchange(srcs, per_dest)

    def body(*refs):
        _, _, ex_refs = _hosted(ex, refs, 0, 0)
        ex.start(*ex_refs)
        ex.finish(*ex_refs)

    return pl.pallas_call(
        body, name=name, in_specs=ex.specs, out_specs=ex.specs, out_shape=ex.out_shape, scratch_shapes=ex.scratch,
    )(*ex.srcs)


def _sibling_exchange(srcs, name):
    n = len(srcs)

    def body(*refs):
        src_refs, out_refs = refs[:n], refs[n:2 * n]
        send_sems, recv_sems = refs[2 * n:]
        x, y, c = lax.axis_index("x"), lax.axis_index("y"), lax.axis_index("c")
        cps = [pltpu.make_async_remote_copy(src_ref=src_refs[a], dst_ref=out_refs[a], send_sem=send_sems.at[a],
                                            recv_sem=recv_sems.at[a], device_id=(x, y, 1 - c), device_id_type=MESH)
               for a in range(n)]
        for cp in cps:
            cp.start()
        for cp in cps:
            cp.wait()

    return pl.pallas_call(
        body, name=name,
        in_specs=[pl.BlockSpec(memory_space=pl.ANY)] * n,
        out_specs=[pl.BlockSpec(memory_space=pl.ANY)] * n,
        out_shape=[jax.ShapeDtypeStruct(s.shape, s.dtype) for s in srcs],
        scratch_shapes=[pltpu.SemaphoreType.DMA((n,)), pltpu.SemaphoreType.DMA((n,))],
    )(*srcs)


def _allgather_small(v):
    m_per, n = v.shape

    def body(x_ref, out_ref, send_sems, recv_sems, local_sem):
        x, y, c = lax.axis_index("x"), lax.axis_index("y"), lax.axis_index("c")
        me, sibling = (x, y, c), (x, y, 1 - c)
        chips = [(1 - x, y), (x, 1 - y), (1 - x, 1 - y)]

        def rows(px, py, pc):
            return out_ref.at[pl.ds((4 * px + 2 * py + pc) * m_per, m_per), :]

        def copy(k, block, to, src=None):
            return pltpu.make_async_remote_copy(
                src_ref=rows(*block) if src is None else src, dst_ref=rows(*block),
                send_sem=send_sems.at[k], recv_sem=recv_sems.at[k], device_id=to, device_id_type=MESH)

        mine = pltpu.make_async_copy(x_ref, rows(*me), local_sem)
        mine.start()
        first = [copy(0, me, sibling, src=x_ref)]
        first += [copy(1 + j, me, (*chip, c), src=x_ref) for j, chip in enumerate(chips)]
        for cp in first:
            cp.start()
        passed = [copy(4 + j, (*chip, c), sibling) for j, chip in enumerate(chips)]
        for j, chip in enumerate(chips):
            copy(1 + j, (*chip, c), me).wait_recv()
            passed[j].start()
        copy(0, sibling, me).wait_recv()
        for j, chip in enumerate(chips):
            copy(4 + j, (*chip, 1 - c), me).wait_recv()
        for cp in first + passed:
            cp.wait_send()
        mine.wait()

    return pl.pallas_call(
        body, name="allgather_small",
        out_shape=jax.ShapeDtypeStruct((8 * m_per, n), v.dtype),
        in_specs=[pl.BlockSpec(memory_space=pltpu.VMEM)],
        out_specs=pl.BlockSpec(memory_space=pltpu.VMEM),
        scratch_shapes=[pltpu.SemaphoreType.DMA((7,)), pltpu.SemaphoreType.DMA((7,)), pltpu.SemaphoreType.DMA],
    )(v)


def _pick_rows(r):
    return next(t for t in (512, 256, 128, 64, 32, 16, 8, r) if r % t == 0)


def _sum_slabs(a, name):
    s, r, c = a.shape
    tm = _pick_rows(r)

    def body(a_ref, o_ref):
        acc = a_ref[0].astype(F32)
        for k in range(1, s):
            acc = acc + a_ref[k].astype(F32)
        o_ref[...] = acc

    return pl.pallas_call(
        body, name=name, grid=(r // tm,),
        in_specs=[pl.BlockSpec((s, tm, c), lambda i: (0, i, 0))],
        out_specs=_rows(tm, c),
        out_shape=jax.ShapeDtypeStruct((r, c), F32),
        compiler_params=_params(("parallel",)),
    )(a)


def _adamw(w, ga, gb, m, v, name):
    r, c = w.shape
    tm = min(256, _pick_rows(r))

    def body(w_ref, ga_ref, gb_ref, m_ref, v_ref, g_ref, d_ref, nm_ref, nv_ref):
        gv = ga_ref[...] + gb_ref[...]
        g_ref[...] = gv
        nm = ADAM_B1 * m_ref[...] + (1.0 - ADAM_B1) * gv
        nv = ADAM_B2 * v_ref[...] + (1.0 - ADAM_B2) * (gv * gv)
        m_hat = nm / (1.0 - ADAM_B1 ** ADAM_STEP)
        v_hat = nv / (1.0 - ADAM_B2 ** ADAM_STEP)
        d_ref[...] = -ADAM_LR * (m_hat / (jnp.sqrt(v_hat) + ADAM_EPS) + ADAM_WD * w_ref[...])
        nm_ref[...] = nm
        nv_ref[...] = nv

    spec = _rows(tm, c)
    return pl.pallas_call(
        body, name=name, grid=(r // tm,),
        in_specs=[spec] * 5, out_specs=[spec] * 4,
        out_shape=[jax.ShapeDtypeStruct((r, c), F32)] * 4,
        compiler_params=_params(("parallel",)),
    )(w, ga, gb, m, v)


def _adamw_small(w, gparts, m, v):
    mrows, n = w.shape

    def body(w_ref, g_ref, m_ref, v_ref, go_ref, d_ref, nm_ref, nv_ref):
        gv = g_ref[0]
        for k in range(1, 8):
            gv = gv + g_ref[k]
        go_ref[...] = gv
        nm = ADAM_B1 * m_ref[...] + (1.0 - ADAM_B1) * gv
        nv = ADAM_B2 * v_ref[...] + (1.0 - ADAM_B2) * (gv * gv)
        m_hat = nm / (1.0 - ADAM_B1 ** ADAM_STEP)
        v_hat = nv / (1.0 - ADAM_B2 ** ADAM_STEP)
        d_ref[...] = -ADAM_LR * (m_hat / (jnp.sqrt(v_hat) + ADAM_EPS) + ADAM_WD * w_ref[...])
        nm_ref[...] = nm
        nv_ref[...] = nv

    return pl.pallas_call(
        body, name="adamw_small",
        out_shape=[jax.ShapeDtypeStruct((mrows, n), F32)] * 4,
    )(w, gparts, m, v)


def _flat(parts):
    nl = parts[0].shape[0]
    return jnp.concatenate([p.reshape(nl, -1) for p in parts], axis=1)


def _unflat(flat, shapes):
    out, off = [], 0
    nl = flat.shape[0]
    for s in shapes:
        n = math.prod(s)
        out.append(flat[:, off:off + n].reshape((nl,) + tuple(s)))
        off += n
    return out


def kernel(x, w_in, b_gate, q_norm_g, k_norm_g, q_a_norm_g, kv_a_norm_g, w_q_up, w_kv_up, w_branch_a, w_branch_b, w_o, w_ffn_up, w_ffn_down, pre_mix_g, post_mix_g, pre_ffn_g, post_ffn_g, loss_target, m_w_in, m_b_gate, m_q_norm_g, m_k_norm_g, m_q_a_norm_g, m_kv_a_norm_g, m_w_q_up, m_w_kv_up, m_w_branch_a, m_w_branch_b, m_w_o, m_w_ffn_up, m_w_ffn_down, m_pre_mix_g, m_post_mix_g, m_pre_ffn_g, m_post_ffn_g, v_w_in, v_b_gate, v_q_norm_g, v_k_norm_g, v_q_a_norm_g, v_kv_a_norm_g, v_w_q_up, v_w_kv_up, v_w_branch_a, v_w_branch_b, v_w_o, v_w_ffn_up, v_w_ffn_down, v_pre_mix_g, v_post_mix_g, v_pre_ffn_g, v_post_ffn_g):
    wts = dict(w_in=w_in, b_gate=b_gate, q_norm_g=q_norm_g, k_norm_g=k_norm_g, q_a_norm_g=q_a_norm_g,
               kv_a_norm_g=kv_a_norm_g, w_q_up=w_q_up, w_kv_up=w_kv_up, w_branch_a=w_branch_a, w_branch_b=w_branch_b,
               w_o=w_o, w_ffn_up=w_ffn_up, w_ffn_down=w_ffn_down, pre_mix_g=pre_mix_g, post_mix_g=post_mix_g,
               pre_ffn_g=pre_ffn_g, post_ffn_g=post_ffn_g)
    mom = dict(w_in=m_w_in, b_gate=m_b_gate, q_norm_g=m_q_norm_g, k_norm_g=m_k_norm_g, q_a_norm_g=m_q_a_norm_g,
               kv_a_norm_g=m_kv_a_norm_g, w_q_up=m_w_q_up, w_kv_up=m_w_kv_up, w_branch_a=m_w_branch_a,
               w_branch_b=m_w_branch_b, w_o=m_w_o, w_ffn_up=m_w_ffn_up, w_ffn_down=m_w_ffn_down, pre_mix_g=m_pre_mix_g,
               post_mix_g=m_post_mix_g, pre_ffn_g=m_pre_ffn_g, post_ffn_g=m_post_ffn_g)
    var = dict(w_in=v_w_in, b_gate=v_b_gate, q_norm_g=v_q_norm_g, k_norm_g=v_k_norm_g, q_a_norm_g=v_q_a_norm_g,
               kv_a_norm_g=v_kv_a_norm_g, w_q_up=v_w_q_up, w_kv_up=v_w_kv_up, w_branch_a=v_w_branch_a,
               w_branch_b=v_w_branch_b, w_o=v_w_o, w_ffn_up=v_w_ffn_up, w_ffn_down=v_w_ffn_down, pre_mix_g=v_pre_mix_g,
               post_mix_g=v_post_mix_g, pre_ffn_g=v_pre_ffn_g, post_ffn_g=v_post_ffn_g)
    nl = w_in.shape[0]

    shards = {n: wts[n].astype(BF) for n in BIG}
    smalls = [{n: wts[n][li] for n in SMALL} for li in range(nl)]
    loss_local, dx, received, small_grads = _local_step(x[0], loss_target[0], shards, smalls)
    loss = lax.psum(loss_local, ("x", "y", "c"))
    part = []
    for i, n in enumerate(BIG):
        got = jnp.stack([received[li][n] for li in range(nl)], axis=1)
        part.append(_sum_slabs(got.reshape(4, -1, got.shape[-1]), "sum_chips_" + n))
    other = _sibling_exchange(part, "swap_cores")

    small_shapes = [wts[n].shape[1:] for n in SMALL]
    g_loc = _flat([jnp.stack([small_grads[li][n] for li in range(nl)]) for n in SMALL]).reshape(-1, 128)
    g_all = _allgather_small(g_loc).reshape(8, -1, 128)
    pack = lambda d: _flat([d[n] for n in SMALL]).reshape(-1, 128)
    gs, ds, ms, vs = _adamw_small(pack(wts), g_all, pack(mom), pack(var))
    unpack = lambda a: dict(zip(SMALL, _unflat(a.reshape(nl, -1), small_shapes)))
    g_small, d_small, m_small, v_small = unpack(gs), unpack(ds), unpack(ms), unpack(vs)

    out_g, out_d, out_m, out_v = dict(g_small), dict(d_small), dict(m_small), dict(v_small)
    for i, n in enumerate(BIG):
        shp = wts[n].shape
        two = lambda a: a.reshape(-1, shp[-1])
        g, d, nm, nv = _adamw(two(wts[n]), part[i], other[i], two(mom[n]), two(var[n]), "adamw_" + n)
        out_g[n], out_d[n], out_m[n], out_v[n] = g.reshape(shp), d.reshape(shp), nm.reshape(shp), nv.reshape(shp)

    return (loss, dx[None], *[out_g[n] for n in ORDER], *[out_d[n] for n in ORDER], *[out_m[n] for n in ORDER],
            *[out_v[n] for n in ORDER])
```

```python
import functools
import math

import jax
import jax.numpy as jnp
from jax import lax
from jax.experimental import pallas as pl
from jax.experimental.pallas import tpu as pltpu

F32 = jnp.float32
BF = jnp.bfloat16
MESH = pl.DeviceIdType.MESH

EPS = 1e-6
D = 1024
NH = 8
HW = 128
GQA_KV = 2
GQA_G = 4
HEAD = 64
MLA_NOPE = 64
MLA_ROPE = 32
MLA_QK = 96
Q_RANK = 384
KV_RANK = 256
DFF = 4096
GRID_W = 64
ROPE_THETA = 10000.0

O_QA, O_KA, O_VA, O_CQ, O_CKV, O_KR, ZM_W = 0, 1024, 1280, 1536, 1920, 2176, 2304
ZG_W = 2048

ADAM_LR, ADAM_B1, ADAM_B2, ADAM_EPS, ADAM_WD, ADAM_STEP = 0.001, 0.9, 0.999, 1e-08, 0.01, 10

TM = 256
TQ = 256
TQ_FWD_STEP = 1024
TQ_BWD_STEP = 512
ONES_LANE = 64
LOG2E = 1.4426950408889634
LN2 = 0.6931471805599453
SCALE_GQA = 1.0 / math.sqrt(HEAD)
SCALE_MLA = 1.0 / math.sqrt(MLA_QK)
C2_GQA = SCALE_GQA * LOG2E
C2_MLA = SCALE_MLA * LOG2E
VMEM_LIMIT = 56 * 1024 * 1024
TN_ROWS_NARROW, TN_ROWS_WIDE = 2048, 1024

BIG = ("w_in", "w_q_up", "w_kv_up", "w_branch_a", "w_branch_b", "w_o", "w_ffn_up", "w_ffn_down")
FIRST = BIG[:3]
REST = BIG[3:]
SMALL = ("b_gate", "q_norm_g", "k_norm_g", "q_a_norm_g", "kv_a_norm_g", "pre_mix_g", "post_mix_g", "pre_ffn_g", "post_ffn_g")
ORDER = ("w_in", "b_gate", "q_norm_g", "k_norm_g", "q_a_norm_g", "kv_a_norm_g", "w_q_up", "w_kv_up", "w_branch_a",
         "w_branch_b", "w_o", "w_ffn_up", "w_ffn_down", "pre_mix_g", "post_mix_g", "pre_ffn_g", "post_ffn_g")


def _params(sem=None):
    return pltpu.CompilerParams(dimension_semantics=sem, vmem_limit_bytes=VMEM_LIMIT)


def _rows(tm, w):
    return pl.BlockSpec((tm, w), lambda i: (i, 0))


def _cols(h, tm):
    return pl.BlockSpec((h, tm), lambda i: (0, i))


def _whole(shape):
    return pl.BlockSpec(shape, lambda i: (0,) * len(shape))


def _dot(a, b):
    return jnp.dot(a, b, preferred_element_type=F32)


def _dot_nt(a, b):
    return lax.dot_general(a, b, (((1,), (1,)), ((), ())), preferred_element_type=F32)


def _dot_tn(a, b):
    return lax.dot_general(a, b, (((0,), (0,)), ((), ())), preferred_element_type=F32)


def _norm_fwd(xv, g, n=None):
    n = xv.shape[-1] if n is None else n
    r = lax.rsqrt(jnp.sum(xv * xv, axis=-1, keepdims=True) * (1.0 / n) + EPS)
    return (xv * r) * g


def _norm_bwd(xv, g, dy, n=None):
    n = xv.shape[-1] if n is None else n
    r = lax.rsqrt(jnp.sum(xv * xv, axis=-1, keepdims=True) * (1.0 / n) + EPS)
    xh = xv * r
    dxh = dy * g
    dg = jnp.sum(dy * xh, axis=0, keepdims=True)
    dx = r * (dxh - xh * (jnp.sum(dxh * xh, axis=-1, keepdims=True) * (1.0 / n)))
    return dx, dg


def _accumulate(ref, val):
    @pl.when(pl.program_id(0) == 0)
    def _():
        ref[...] = jnp.zeros_like(ref)

    ref[...] += val


def _rot(xv, q, neg):
    w = xv.shape[-1]
    return jnp.where(neg, -pltpu.roll(xv, w - q, 1), pltpu.roll(xv, q, 1))


def _rope(xv, c, s, q, neg):
    return xv * c + _rot(xv, q, neg) * s


def _rope_bwd(dy, c, s, q, neg):
    return dy * c - _rot(dy * s, q, neg)


def _neg_mask(tm, q):
    lane = lax.broadcasted_iota(jnp.int32, (tm, HW), 1)
    return (lane % (2 * q)) < q


def _sigmoid(z):
    return 1.0 / (1.0 + jnp.exp(-z))


def _fwd_inproj(x, g1, w1, w2, b):
    t = x.shape[0]

    def body(x_ref, g_ref, w1_ref, w2_ref, b_ref, u_ref, zm_ref, gate_ref):
        u = _norm_fwd(x_ref[...], g_ref[...]).astype(BF)
        u_ref[...] = u
        for c in range(0, ZM_W, 768):
            zm_ref[:, c:c + 768] = _dot(u, w1_ref[:, c:c + 768])
        for c in range(0, ZG_W, 512):
            gate_ref[:, c:c + 512] = _sigmoid(_dot(u, w2_ref[:, c:c + 512]) + b_ref[:, c:c + 512]).astype(BF)

    return pl.pallas_call(
        body, name="fwd_inproj", grid=(t // TM,),
        in_specs=[_rows(TM, D), _whole((1, D)), _whole((D, ZM_W)), _whole((D, ZG_W)), _whole((1, ZG_W))],
        out_specs=[_rows(TM, D), _rows(TM, ZM_W), _rows(TM, ZG_W)],
        out_shape=[jax.ShapeDtypeStruct((t, D), BF), jax.ShapeDtypeStruct((t, ZM_W), F32),
                   jax.ShapeDtypeStruct((t, ZG_W), BF)],
        compiler_params=_params(("parallel",)),
    )(x, g1, w1, w2, b)


def _fwd_mid(zm, gq, gk, gqa, gkva, wq, wk, wv, ca, sa, cb, sb):
    t = zm.shape[0]

    def body(zm_ref, gq_ref, gk_ref, gqa_ref, gkva_ref, wq_ref, wk_ref, wv_ref, ca_ref, sa_ref, cb_ref, sb_ref,
             qa_ref, ka_ref, va_ref, qb_ref, kb_ref, vb_ref, cqn_ref, ckvn_ref):
        ca_, sa_, cb_, sb_ = ca_ref[...], sa_ref[...], cb_ref[...], sb_ref[...]
        nega, negb = _neg_mask(TM, 16), _neg_mask(TM, 8)
        for h in range(NH):
            xv = zm_ref[:, O_QA + h * HW:O_QA + (h + 1) * HW]
            qa_ref[:, h * HW:(h + 1) * HW] = (_rope(_norm_fwd(xv, gq_ref[...], HEAD), ca_, sa_, 16, nega) * C2_GQA).astype(BF)
        for h in range(GQA_KV):
            xv = zm_ref[:, O_KA + h * HW:O_KA + (h + 1) * HW]
            ka_ref[:, h * HW:(h + 1) * HW] = _rope(_norm_fwd(xv, gk_ref[...], HEAD), ca_, sa_, 16, nega).astype(BF)
        ones = lax.broadcasted_iota(jnp.int32, (TM, HW), 1) == ONES_LANE
        for h in range(GQA_KV):
            va_ref[:, h * HW:(h + 1) * HW] = jnp.where(ones, 1.0, zm_ref[:, O_VA + h * HW:O_VA + (h + 1) * HW]).astype(BF)
        cqn = _norm_fwd(zm_ref[:, O_CQ:O_CQ + Q_RANK], gqa_ref[...]).astype(BF)
        ckvn = _norm_fwd(zm_ref[:, O_CKV:O_CKV + KV_RANK], gkva_ref[...]).astype(BF)
        cqn_ref[...] = cqn
        ckvn_ref[...] = ckvn
        qb = _dot(cqn, wq_ref[...])
        kpre = _dot(ckvn, wk_ref[...])
        kr = _rope(zm_ref[:, O_KR:O_KR + HW], cb_, sb_, 8, negb)
        for h in range(NH):
            sl = slice(h * HW, (h + 1) * HW)
            qb_ref[:, sl] = (_rope(qb[:, sl], cb_, sb_, 8, negb) * C2_MLA).astype(BF)
            kb_ref[:, sl] = (kpre[:, sl] + kr).astype(BF)
        vb = _dot(ckvn, wv_ref[...])
        for h in range(NH):
            vb_ref[:, h * HW:(h + 1) * HW] = jnp.where(ones, 1.0, vb[:, h * HW:(h + 1) * HW]).astype(BF)

    hw8 = NH * HW
    return pl.pallas_call(
        body, name="fwd_mid", grid=(t // TM,),
        in_specs=[_rows(TM, ZM_W), _whole((1, HW)), _whole((1, HW)), _whole((1, Q_RANK)), _whole((1, KV_RANK)),
                  _whole((Q_RANK, hw8)), _whole((KV_RANK, hw8)), _whole((KV_RANK, hw8)),
                  _rows(TM, HW), _rows(TM, HW), _rows(TM, HW), _rows(TM, HW)],
        out_specs=[_rows(TM, hw8), _rows(TM, GQA_KV * HW), _rows(TM, GQA_KV * HW), _rows(TM, hw8), _rows(TM, hw8),
                   _rows(TM, hw8), _rows(TM, Q_RANK), _rows(TM, KV_RANK)],
        out_shape=[jax.ShapeDtypeStruct((t, hw8), BF), jax.ShapeDtypeStruct((t, GQA_KV * HW), BF),
                   jax.ShapeDtypeStruct((t, GQA_KV * HW), BF), jax.ShapeDtypeStruct((t, hw8), BF),
                   jax.ShapeDtypeStruct((t, hw8), BF), jax.ShapeDtypeStruct((t, hw8), BF),
                   jax.ShapeDtypeStruct((t, Q_RANK), BF), jax.ShapeDtypeStruct((t, KV_RANK), BF)],
        compiler_params=_params(("parallel",)),
    )(zm, gq, gk, gqa, gkva, wq, wk, wv, ca, sa, cb, sb)


class _ChipExchange:
    def __init__(self, srcs, per_dest):
        self.srcs, self.per_dest, self.n = list(srcs), per_dest, len(srcs)
        self.out_shape = [jax.ShapeDtypeStruct((4,) + tuple(s.shape[1:] if per_dest else s.shape), s.dtype) for s in srcs]
        self.specs = [pl.BlockSpec(memory_space=pl.ANY)] * self.n
        self.scratch = [pltpu.SemaphoreType.DMA((3 * self.n,)), pltpu.SemaphoreType.DMA((3 * self.n,)),
                        pltpu.SemaphoreType.DMA((self.n,))]

    def _copies(self, src_refs, out_refs, send_sems, recv_sems, local_sems):
        x, y, c = lax.axis_index("x"), lax.axis_index("y"), lax.axis_index("c")
        me = 2 * x + y
        chips = [(1 - x, y), (x, 1 - y), (1 - x, 1 - y)]

        def piece(a, k):
            return src_refs[a].at[k] if self.per_dest else src_refs[a]

        def remote(a, j, src_chip, dst_slab):
            px, py = chips[j]
            return pltpu.make_async_remote_copy(
                src_ref=piece(a, src_chip), dst_ref=out_refs[a].at[dst_slab], send_sem=send_sems.at[3 * a + j],
                recv_sem=recv_sems.at[3 * a + j], device_id=(px, py, c), device_id_type=MESH)

        local = [pltpu.make_async_copy(piece(a, me), out_refs[a].at[me], local_sems.at[a]) for a in range(self.n)]
        sends = [remote(a, j, 2 * chips[j][0] + chips[j][1], me) for a in range(self.n) for j in range(3)]
        recvs = [remote(a, j, me, 2 * chips[j][0] + chips[j][1]) for a in range(self.n) for j in range(3)]
        return local, sends, recvs

    def start(self, *refs):
        local, sends, _ = self._copies(*refs)
        for cp in local + sends:
            cp.start()

    def finish(self, *refs):
        local, sends, recvs = self._copies(*refs)
        for cp in recvs:
            cp.wait_recv()
        for cp in sends:
            cp.wait_send()
        for cp in local:
            cp.wait()


def _hosted(exchange, refs, n_in, n_out):
    n = exchange.n if exchange else 0
    ins, srcs = refs[:n_in], refs[n_in:n_in + n]
    outs = refs[n_in + n:n_in + n + n_out]
    rest = refs[n_in + n + n_out:]
    return ins, outs, (tuple(srcs), tuple(rest[:n])) + tuple(rest[n:])


def _flash_fwd(q, k, v, group, name, exchange=None):
    t = q.shape[0]
    tq = min(TQ_FWD_STEP, t)
    nq = t // tq

    def body(*refs):
        (q_ref, k_ref, v_ref), (o_ref, lse_ref), ex_refs = _hosted(exchange, refs, 3, 2)
        if exchange:
            @pl.when((pl.program_id(0) == 0) & (pl.program_id(1) == 0))
            def _():
                exchange.start(*ex_refs)

        for r0 in range(0, tq, TQ):
            rows = slice(r0, r0 + TQ)
            s = _dot_nt(q_ref[rows, :], k_ref[...])
            m = jnp.max(s, axis=-1, keepdims=True)
            acc = _dot(jnp.exp2(s - m).astype(BF), v_ref[...])
            l = acc[:, ONES_LANE:ONES_LANE + 1]
            o_ref[rows, :] = (acc / l).astype(BF)
            lse_ref[rows, :] = jnp.broadcast_to(m + jnp.log2(l), (TQ, HW))
        if exchange:
            @pl.when((pl.program_id(0) == NH - 1) & (pl.program_id(1) == nq - 1))
            def _():
                exchange.finish(*ex_refs)

    qspec = pl.BlockSpec((tq, HW), lambda h, i: (i, h))
    kspec = pl.BlockSpec((t, HW), lambda h, i: (0, h // group))
    out = pl.pallas_call(
        body, name=name, grid=(NH, nq),
        in_specs=[qspec, kspec, kspec] + (exchange.specs if exchange else []),
        out_specs=[qspec, qspec] + (exchange.specs if exchange else []),
        out_shape=[jax.ShapeDtypeStruct((t, NH * HW), BF), jax.ShapeDtypeStruct((t, NH * HW), F32)]
        + (exchange.out_shape if exchange else []),
        scratch_shapes=exchange.scratch if exchange else [],
        compiler_params=_params(("arbitrary", "arbitrary") if exchange else ("parallel", "parallel")),
    )(q, k, v, *(exchange.srcs if exchange else []))
    return out[0], out[1], out[2:]


def _fwd_merge(ya, yb, gate, wa, wb, wo, x, g2):
    t = x.shape[0]

    def body(ya_ref, yb_ref, gate_ref, wa_ref, wb_ref, wo_ref, x_ref, g_ref, pa_ref, pb_ref, mg_ref, m_ref, x1_ref):
        pa = _dot(ya_ref[...], wa_ref[...])
        pb = _dot(yb_ref[...], wb_ref[...])
        pa_ref[...] = pa.astype(BF)
        pb_ref[...] = pb.astype(BF)
        mg = (gate_ref[:, :D].astype(F32) * pa + gate_ref[:, D:].astype(F32) * pb).astype(BF)
        mg_ref[...] = mg
        m = _dot(mg, wo_ref[...])
        m_ref[...] = m
        x1_ref[...] = x_ref[...] + _norm_fwd(m, g_ref[...])

    return pl.pallas_call(
        body, name="fwd_merge", grid=(t // TM,),
        in_specs=[_rows(TM, D), _rows(TM, D), _rows(TM, ZG_W), _whole((D, D)), _whole((D, D)),
                  _whole((D, D)), _rows(TM, D), _whole((1, D))],
        out_specs=[_rows(TM, D), _rows(TM, D), _rows(TM, D), _rows(TM, D), _rows(TM, D)],
        out_shape=[jax.ShapeDtypeStruct((t, D), BF), jax.ShapeDtypeStruct((t, D), BF), jax.ShapeDtypeStruct((t, D), BF),
                   jax.ShapeDtypeStruct((t, D), F32), jax.ShapeDtypeStruct((t, D), F32)],
        compiler_params=_params(("parallel",)),
    )(ya, yb, gate, wa, wb, wo, x, g2)


def _fwd_ffn_up(x1, g3, wup):
    t = x1.shape[0]

    def body(x_ref, g_ref, w_ref, hn_ref, a_ref):
        hn = _norm_fwd(x_ref[...], g_ref[...]).astype(BF)
        hn_ref[...] = hn
        for c in range(0, DFF, 1024):
            h = _dot(hn, w_ref[:, c:c + 1024])
            r = jnp.maximum(h, 0.0)
            a_ref[:, c:c + 1024] = (r * r).astype(BF)

    return pl.pallas_call(
        body, name="fwd_ffn_up", grid=(t // TM,),
        in_specs=[_rows(TM, D), _whole((1, D)), _whole((D, DFF))],
        out_specs=[_rows(TM, D), _rows(TM, DFF)],
        out_shape=[jax.ShapeDtypeStruct((t, D), BF), jax.ShapeDtypeStruct((t, DFF), BF)],
        compiler_params=_params(("parallel",)),
    )(x1, g3, wup)


def _fwd_ffn_down(a, wdn, x1, g4):
    t = x1.shape[0]

    def body(a_ref, w_ref, x_ref, g_ref, f_ref, x2_ref):
        f = _dot(a_ref[...], w_ref[...])
        f_ref[...] = f
        x2_ref[...] = x_ref[...] + _norm_fwd(f, g_ref[...])

    return pl.pallas_call(
        body, name="fwd_ffn_down", grid=(t // TM,),
        in_specs=[_rows(TM, DFF), _whole((DFF, D)), _rows(TM, D), _whole((1, D))],
        out_specs=[_rows(TM, D), _rows(TM, D)],
        out_shape=[jax.ShapeDtypeStruct((t, D), F32), jax.ShapeDtypeStruct((t, D), F32)],
        compiler_params=_params(("parallel",)),
    )(a, wdn, x1, g4)


def _loss_head(y, target):
    t = y.shape[0]

    def body(y_ref, t_ref, dy_ref, loss_ref):
        d = y_ref[...] - t_ref[...]
        dy_ref[...] = d * (1.0 / D)
        part = 0.5 * jnp.sum(jnp.sum(d * d, axis=-1, keepdims=True) * (1.0 / D), axis=0, keepdims=True)
        _accumulate(loss_ref, jnp.broadcast_to(part, (8, HW)))

    return pl.pallas_call(
        body, name="loss_head", grid=(t // TM,),
        in_specs=[_rows(TM, D), _rows(TM, D)],
        out_specs=[_rows(TM, D), _whole((8, HW))],
        out_shape=[jax.ShapeDtypeStruct((t, D), F32), jax.ShapeDtypeStruct((8, HW), F32)],
        compiler_params=_params(("arbitrary",)),
    )(y, target)


def _bwd_ffn_down(f, dx2, g4, wdn, a):
    t = f.shape[0]

    def body(f_ref, dx2_ref, g_ref, w_ref, a_ref, df_ref, dh_ref, dg_ref):
        df, dg = _norm_bwd(f_ref[...], g_ref[...], dx2_ref[...])
        _accumulate(dg_ref, dg)
        df16 = df.astype(BF)
        df_ref[...] = df16
        for c in range(0, DFF, 1024):
            da = _dot_nt(df16, w_ref[c:c + 1024, :])
            dh_ref[:, c:c + 1024] = (da * (2.0 * jnp.sqrt(a_ref[:, c:c + 1024].astype(F32)))).astype(BF)

    return pl.pallas_call(
        body, name="bwd_ffn_down", grid=(t // TM,),
        in_specs=[_rows(TM, D), _rows(TM, D), _whole((1, D)), _whole((DFF, D)), _rows(TM, DFF)],
        out_specs=[_rows(TM, D), _rows(TM, DFF), _whole((1, D))],
        out_shape=[jax.ShapeDtypeStruct((t, D), BF), jax.ShapeDtypeStruct((t, DFF), BF), jax.ShapeDtypeStruct((1, D), F32)],
        compiler_params=_params(("arbitrary",)),
    )(f, dx2, g4, wdn, a)


def _bwd_ffn_up(dh, wup, x1, g3, dx2):
    t = x1.shape[0]

    def body(dh_ref, w_ref, x_ref, g_ref, dx2_ref, dx1_ref, dg_ref):
        dhn = _dot_nt(dh_ref[...], w_ref[...])
        dx, dg = _norm_bwd(x_ref[...], g_ref[...], dhn)
        _accumulate(dg_ref, dg)
        dx1_ref[...] = dx2_ref[...] + dx

    return pl.pallas_call(
        body, name="bwd_ffn_up", grid=(t // TM,),
        in_specs=[_rows(TM, DFF), _whole((D, DFF)), _rows(TM, D), _whole((1, D)), _rows(TM, D)],
        out_specs=[_rows(TM, D), _whole((1, D))],
        out_shape=[jax.ShapeDtypeStruct((t, D), F32), jax.ShapeDtypeStruct((1, D), F32)],
        compiler_params=_params(("arbitrary",)),
    )(dh, wup, x1, g3, dx2)


def _bwd_merge(m, dx1, g2, wo, gate, pa, pb, wa, wb):
    t = m.shape[0]

    def body(m_ref, dx1_ref, g_ref, wo_ref, gate_ref, pa_ref, pb_ref, wa_ref, wb_ref,
             dm_ref, dpa_ref, dpb_ref, dzg_ref, dya_ref, dyb_ref, dg_ref, db_ref):
        dm, dg = _norm_bwd(m_ref[...], g_ref[...], dx1_ref[...])
        _accumulate(dg_ref, dg)
        dm16 = dm.astype(BF)
        dm_ref[...] = dm16
        dmg = _dot_nt(dm16, wo_ref[...])
        ga = gate_ref[:, :D].astype(F32)
        gb = gate_ref[:, D:].astype(F32)
        dpa = (dmg * ga).astype(BF)
        dpb = (dmg * gb).astype(BF)
        dpa_ref[...] = dpa
        dpb_ref[...] = dpb
        dza = (dmg * pa_ref[...].astype(F32)) * (ga * (1.0 - ga))
        dzb = (dmg * pb_ref[...].astype(F32)) * (gb * (1.0 - gb))
        dzg_ref[:, :D] = dza.astype(BF)
        dzg_ref[:, D:] = dzb.astype(BF)

        @pl.when(pl.program_id(0) == 0)
        def _():
            db_ref[...] = jnp.zeros_like(db_ref)

        db_ref[:, :D] += jnp.sum(dza, axis=0, keepdims=True)
        db_ref[:, D:] += jnp.sum(dzb, axis=0, keepdims=True)
        dya_ref[...] = _dot_nt(dpa, wa_ref[...]).astype(BF)
        dyb_ref[...] = _dot_nt(dpb, wb_ref[...]).astype(BF)

    return pl.pallas_call(
        body, name="bwd_merge", grid=(t // TM,),
        in_specs=[_rows(TM, D), _rows(TM, D), _whole((1, D)), _whole((D, D)), _rows(TM, ZG_W),
                  _rows(TM, D), _rows(TM, D), _whole((D, D)), _whole((D, D))],
        out_specs=[_rows(TM, D), _rows(TM, D), _rows(TM, D), _rows(TM, ZG_W), _rows(TM, D), _rows(TM, D),
                   _whole((1, D)), _whole((1, ZG_W))],
        out_shape=[jax.ShapeDtypeStruct((t, D), BF), jax.ShapeDtypeStruct((t, D), BF), jax.ShapeDtypeStruct((t, D), BF),
                   jax.ShapeDtypeStruct((t, ZG_W), BF), jax.ShapeDtypeStruct((t, D), BF), jax.ShapeDtypeStruct((t, D), BF),
                   jax.ShapeDtypeStruct((1, D), F32), jax.ShapeDtypeStruct((1, ZG_W), F32)],
        compiler_params=_params(("arbitrary",)),
    )(m, dx1, g2, wo, gate, pa, pb, wa, wb)


def _flash_bwd(q, do, o, lse, k, v, group, scale, name, exchange=None):
    t = q.shape[0]
    tq = min(TQ_BWD_STEP, t)
    nq = t // tq
    nkv = NH // group

    def body(*refs):
        (q_ref, do_ref, o_ref, lse_ref, k_ref, v_ref), (dq_ref, dkt_ref, dvt_ref), ex_refs = _hosted(exchange, refs, 6, 3)
        first = (pl.program_id(1) == 0) & (pl.program_id(2) == 0)
        if exchange:
            @pl.when(first & (pl.program_id(0) == 0))
            def _():
                exchange.start(*ex_refs)

        @pl.when(first)
        def _():
            dkt_ref[...] = jnp.zeros_like(dkt_ref)
            dvt_ref[...] = jnp.zeros_like(dvt_ref)

        dvt = dkt = None
        for r0 in range(0, tq, TQ):
            rows = slice(r0, r0 + TQ)
            qv = q_ref[rows, :]
            dov = do_ref[rows, :]
            dsum = jnp.sum(dov.astype(F32) * o_ref[rows, :].astype(F32), axis=-1, keepdims=True)
            p = jnp.exp2(_dot_nt(qv, k_ref[...]) - lse_ref[rows, :1])
            ds = (p * (_dot_nt(dov, v_ref[...]) - dsum)).astype(BF)
            dq_ref[rows, :] = _dot(ds, k_ref[...]) * scale
            dvt_tile = _dot(dov.astype(F32).T.astype(BF), p.astype(BF))
            dkt_tile = _dot(qv.astype(F32).T.astype(BF), ds)
            dvt = dvt_tile if dvt is None else dvt + dvt_tile
            dkt = dkt_tile if dkt is None else dkt + dkt_tile
        dvt_ref[...] += dvt
        dkt_ref[...] += dkt * LN2
        if exchange:
            @pl.when((pl.program_id(0) == nkv - 1) & (pl.program_id(1) == group - 1) & (pl.program_id(2) == nq - 1))
            def _():
                exchange.finish(*ex_refs)

    qspec = pl.BlockSpec((tq, HW), lambda hk, g, i: (i, hk * group + g))
    kspec = pl.BlockSpec((t, HW), lambda hk, g, i: (0, hk))
    tspec = pl.BlockSpec((HW, t), lambda hk, g, i: (hk, 0))
    out = pl.pallas_call(
        body, name=name, grid=(nkv, group, nq),
        in_specs=[qspec, qspec, qspec, qspec, kspec, kspec] + (exchange.specs if exchange else []),
        out_specs=[qspec, tspec, tspec] + (exchange.specs if exchange else []),
        out_shape=[jax.ShapeDtypeStruct((t, NH * HW), F32), jax.ShapeDtypeStruct((nkv * HW, t), F32),
                   jax.ShapeDtypeStruct((nkv * HW, t), F32)] + (exchange.out_shape if exchange else []),
        scratch_shapes=exchange.scratch if exchange else [],
        compiler_params=_params(("arbitrary" if exchange else "parallel", "arbitrary", "arbitrary")),
    )(q, do, o, lse, k, v, *(exchange.srcs if exchange else []))
    return out[0], out[1], out[2], out[3:]


def _bwd_mid(zm, dqa, dkta, dvta, dqb, dktb, dvtb, gq, gk, gqa, gkva, wq, wk, wv, ca, sa, cb, sb):
    t = zm.shape[0]
    hw8 = NH * HW

    def body(zm_ref, dqa_ref, dkta_ref, dvta_ref, dqb_ref, dktb_ref, dvtb_ref, gq_ref, gk_ref, gqa_ref, gkva_ref,
             wq_ref, wk_ref, wv_ref, ca_ref, sa_ref, cb_ref, sb_ref,
             dzm_ref, dqbp_ref, dkb16_ref, dvb16_ref, dgq_ref, dgk_ref, dgqa_ref, dgkva_ref):
        ca_, sa_, cb_, sb_ = ca_ref[...], sa_ref[...], cb_ref[...], sb_ref[...]
        nega, negb = _neg_mask(TM, 16), _neg_mask(TM, 8)
        dgq = jnp.zeros((1, HW), F32)
        for h in range(NH):
            sl = slice(h * HW, (h + 1) * HW)
            dqn = _rope_bwd(dqa_ref[:, sl], ca_, sa_, 16, nega)
            dx, dg = _norm_bwd(zm_ref[:, O_QA + h * HW:O_QA + (h + 1) * HW], gq_ref[...], dqn, HEAD)
            dzm_ref[:, O_QA + h * HW:O_QA + (h + 1) * HW] = dx.astype(BF)
            dgq = dgq + dg
        _accumulate(dgq_ref, dgq)
        dgk = jnp.zeros((1, HW), F32)
        for j in range(GQA_KV):
            dk = dkta_ref[j * HW:(j + 1) * HW, :].T
            dv = dvta_ref[j * HW:(j + 1) * HW, :].T
            dkn = _rope_bwd(dk, ca_, sa_, 16, nega)
            dx, dg = _norm_bwd(zm_ref[:, O_KA + j * HW:O_KA + (j + 1) * HW], gk_ref[...], dkn, HEAD)
            dzm_ref[:, O_KA + j * HW:O_KA + (j + 1) * HW] = dx.astype(BF)
            dzm_ref[:, O_VA + j * HW:O_VA + (j + 1) * HW] = dv.astype(BF)
            dgk = dgk + dg
        _accumulate(dgk_ref, dgk)
        for h in range(NH):
            sl = slice(h * HW, (h + 1) * HW)
            dqbp_ref[:, sl] = _rope_bwd(dqb_ref[:, sl], cb_, sb_, 8, negb).astype(BF)
        dcqn = _dot_nt(dqbp_ref[...], wq_ref[...])
        dx, dg = _norm_bwd(zm_ref[:, O_CQ:O_CQ + Q_RANK], gqa_ref[...], dcqn)
        dzm_ref[:, O_CQ:O_CQ + Q_RANK] = dx.astype(BF)
        _accumulate(dgqa_ref, dg)
        dkr = jnp.zeros((TM, HW), F32)
        for h in range(NH):
            sl = slice(h * HW, (h + 1) * HW)
            dkh = dktb_ref[sl, :].T
            dkr = dkr + dkh
            dkb16_ref[:, sl] = dkh.astype(BF)
            dvb16_ref[:, sl] = dvtb_ref[sl, :].T.astype(BF)
        dkb16 = dkb16_ref[...]
        dvb16 = dvb16_ref[...]
        lane = lax.broadcasted_iota(jnp.int32, (TM, HW), 1)
        in_rope = (lane >= MLA_NOPE) & (lane < MLA_QK)
        dzm_ref[:, O_KR:O_KR + HW] = jnp.where(in_rope, _rope_bwd(dkr, cb_, sb_, 8, negb), 0.0).astype(BF)
        dckvn = _dot_nt(dkb16, wk_ref[...]) + _dot_nt(dvb16, wv_ref[...])
        dx, dg = _norm_bwd(zm_ref[:, O_CKV:O_CKV + KV_RANK], gkva_ref[...], dckvn)
        dzm_ref[:, O_CKV:O_CKV + KV_RANK] = dx.astype(BF)
        _accumulate(dgkva_ref, dg)

    return pl.pallas_call(
        body, name="bwd_mid", grid=(t // TM,),
        in_specs=[_rows(TM, ZM_W), _rows(TM, hw8), _cols(GQA_KV * HW, TM), _cols(GQA_KV * HW, TM), _rows(TM, hw8),
                  _cols(hw8, TM), _cols(hw8, TM), _whole((1, HW)), _whole((1, HW)), _whole((1, Q_RANK)), _whole((1, KV_RANK)),
                  _whole((Q_RANK, hw8)), _whole((KV_RANK, hw8)), _whole((KV_RANK, hw8)),
                  _rows(TM, HW), _rows(TM, HW), _rows(TM, HW), _rows(TM, HW)],
        out_specs=[_rows(TM, ZM_W), _rows(TM, hw8), _rows(TM, hw8), _rows(TM, hw8),
                   _whole((1, HW)), _whole((1, HW)), _whole((1, Q_RANK)), _whole((1, KV_RANK))],
        out_shape=[jax.ShapeDtypeStruct((t, ZM_W), BF), jax.ShapeDtypeStruct((t, hw8), BF), jax.ShapeDtypeStruct((t, hw8), BF),
                   jax.ShapeDtypeStruct((t, hw8), BF), jax.ShapeDtypeStruct((1, HW), F32), jax.ShapeDtypeStruct((1, HW), F32),
                   jax.ShapeDtypeStruct((1, Q_RANK), F32), jax.ShapeDtypeStruct((1, KV_RANK), F32)],
        compiler_params=_params(("arbitrary",)),
    )(zm, dqa, dkta, dvta, dqb, dktb, dvtb, gq, gk, gqa, gkva, wq, wk, wv, ca, sa, cb, sb)


def _bwd_inproj(dzm, dzg, w1, w2, x, g1, dx1):
    t = x.shape[0]

    def body(dzm_ref, dzg_ref, w1_ref, w2_ref, x_ref, g_ref, dx1_ref, dx_ref, dg_ref):
        du = _dot_nt(dzm_ref[...], w1_ref[...]) + _dot_nt(dzg_ref[...], w2_ref[...])
        dx, dg = _norm_bwd(x_ref[...], g_ref[...], du)
        _accumulate(dg_ref, dg)
        dx_ref[...] = dx1_ref[...] + dx

    return pl.pallas_call(
        body, name="bwd_inproj", grid=(t // TM,),
        in_specs=[_rows(TM, ZM_W), _rows(TM, ZG_W), _whole((D, ZM_W)), _whole((D, ZG_W)), _rows(TM, D), _whole((1, D)),
                  _rows(TM, D)],
        out_specs=[_rows(TM, D), _whole((1, D))],
        out_shape=[jax.ShapeDtypeStruct((t, D), F32), jax.ShapeDtypeStruct((1, D), F32)],
        compiler_params=_params(("arbitrary",)),
    )(dzm, dzg, w1, w2, x, g1, dx1)


def _matmul_tn(a, b, tn, name):
    t, kdim = a.shape
    n = b.shape[1]
    tm = min(TN_ROWS_NARROW if kdim <= D else TN_ROWS_WIDE, t)
    nsteps = t // tm

    def body(a_ref, b_ref, o_ref, acc_ref):
        i = pl.program_id(1)

        @pl.when(i == 0)
        def _():
            acc_ref[...] = jnp.zeros_like(acc_ref)

        acc_ref[...] += _dot_tn(a_ref[...].astype(BF), b_ref[...])

        @pl.when(i == nsteps - 1)
        def _():
            o_ref[...] = acc_ref[...].astype(BF)

    return pl.pallas_call(
        body, name=name, grid=(n // tn, nsteps),
        in_specs=[pl.BlockSpec((tm, kdim), lambda j, i: (i, 0)), pl.BlockSpec((tm, tn), lambda j, i: (i, j))],
        out_specs=pl.BlockSpec((kdim, tn), lambda j, i: (0, j)),
        out_shape=jax.ShapeDtypeStruct((kdim, n), BF),
        scratch_shapes=[pltpu.VMEM((kdim, tn), F32)],
        compiler_params=_params(("parallel", "arbitrary")),
    )(a, b)


def _rope_tables(t):
    pos = jnp.arange(t, dtype=jnp.int32)
    row = (pos // GRID_W).astype(F32)
    col = (pos % GRID_W).astype(F32)

    def table(rot_dim):
        half = rot_dim // 2
        inv = ROPE_THETA ** (-jnp.arange(0, half, 2, dtype=F32) / half)
        ar = row[:, None] * inv[None, :]
        ac = col[:, None] * inv[None, :]
        ang = jnp.concatenate([ar, ar, ac, ac], axis=-1)
        return jnp.cos(ang), jnp.sin(ang)

    c64, s64 = table(HEAD)
    c32, s32 = table(MLA_ROPE)
    ones = lambda w: jnp.ones((t, w), F32)
    zeros = lambda w: jnp.zeros((t, w), F32)
    ca = jnp.concatenate([c64, ones(HW - HEAD)], axis=1)
    sa = jnp.concatenate([s64, zeros(HW - HEAD)], axis=1)
    cb = jnp.concatenate([ones(MLA_NOPE), c32, ones(HW - MLA_QK)], axis=1)
    sb = jnp.concatenate([zeros(MLA_NOPE), s32, zeros(HW - MLA_QK)], axis=1)
    return ca, sa, cb, sb


def _pad_heads_cols(w, heads, width):
    k = w.shape[0]
    w = w.reshape(k, heads, width)
    return jnp.pad(w, ((0, 0), (0, 0), (0, HW - width))).reshape(k, heads * HW)


def _pad_heads_rows(w, heads, width):
    n = w.shape[1]
    w = w.reshape(heads, width, n)
    return jnp.pad(w, ((0, 0), (0, HW - width), (0, 0))).reshape(heads * HW, n)


def _unpad_heads_cols(w, heads, width):
    k = w.shape[0]
    return w.reshape(k, heads, HW)[:, :, :width].reshape(k, heads * width)


def _unpad_heads_rows(w, heads, width):
    n = w.shape[1]
    return w.reshape(heads, HW, n)[:, :width, :].reshape(heads * width, n)


def _pad_first(w):
    w_in = w["w_in"]
    qa = _pad_heads_cols(w_in[:, 0:512], NH, HEAD)
    ka = _pad_heads_cols(w_in[:, 512:640], GQA_KV, HEAD)
    va = _pad_heads_cols(w_in[:, 640:768], GQA_KV, HEAD)
    cq = w_in[:, 768:1152]
    ckv = w_in[:, 1152:1408]
    kr = jnp.pad(w_in[:, 1408:1440], ((0, 0), (MLA_NOPE, HW - MLA_QK)))
    kvu = w["w_kv_up"].reshape(KV_RANK, NH, 2 * HEAD)
    return dict(
        w1=jnp.concatenate([qa, ka, va, cq, ckv, kr], axis=1),
        w2=w_in[:, 1440:],
        wq=_pad_heads_cols(w["w_q_up"], NH, MLA_QK),
        wk=jnp.pad(kvu[:, :, :HEAD], ((0, 0), (0, 0), (0, HEAD))).reshape(KV_RANK, NH * HW),
        wv=jnp.pad(kvu[:, :, HEAD:], ((0, 0), (0, 0), (0, HEAD))).reshape(KV_RANK, NH * HW),
    )


def _pad_rest(w):
    return dict(
        wa=_pad_heads_rows(w["w_branch_a"], NH, HEAD),
        wb=_pad_heads_rows(w["w_branch_b"], NH, HEAD),
        wo=w["w_o"], wup=w["w_ffn_up"], wdn=w["w_ffn_down"],
    )


def _unpad_first(g):
    d1 = g["w1"]
    w_in = jnp.concatenate([
        _unpad_heads_cols(d1[:, O_QA:O_KA], NH, HEAD), _unpad_heads_cols(d1[:, O_KA:O_VA], GQA_KV, HEAD),
        _unpad_heads_cols(d1[:, O_VA:O_CQ], GQA_KV, HEAD), d1[:, O_CQ:O_CKV], d1[:, O_CKV:O_KR],
        d1[:, O_KR + MLA_NOPE:O_KR + MLA_QK], g["w2"]], axis=1)
    dk = g["wk"].reshape(KV_RANK, NH, HW)[:, :, :HEAD]
    dv = g["wv"].reshape(KV_RANK, NH, HW)[:, :, :HEAD]
    return dict(
        w_in=w_in,
        w_q_up=_unpad_heads_cols(g["wq"], NH, MLA_QK),
        w_kv_up=jnp.concatenate([dk, dv], axis=2).reshape(KV_RANK, NH * 2 * HEAD),
    )


def _unpad_rest(g):
    return dict(
        w_branch_a=_unpad_heads_rows(g["wa"], NH, HEAD),
        w_branch_b=_unpad_heads_rows(g["wb"], NH, HEAD),
        w_o=g["wo"], w_ffn_up=g["wup"], w_ffn_down=g["wdn"],
    )


def _pad_lanes(v, width):
    return jnp.pad(v, (0, HW - width)).reshape(1, HW)


SHARD_AXIS = dict(w_in=1, w_q_up=1, w_kv_up=1, w_branch_a=1, w_branch_b=1, w_o=0, w_ffn_up=1, w_ffn_down=0)


def _join(parts, name):
    if SHARD_AXIS[name] == 0:
        return parts.reshape(-1, parts.shape[-1])
    return jnp.transpose(parts, (1, 0, 2)).reshape(parts.shape[1], -1)


def _split(full, name):
    if SHARD_AXIS[name] == 0:
        return full.reshape(4, -1, full.shape[-1])
    return jnp.transpose(full.reshape(full.shape[0], 4, -1), (1, 0, 2))


def _local_step(x, target, shards, smalls):
    t = x.shape[0]
    nl = len(smalls)
    ca, sa, cb, sb = _rope_tables(t)

    def joined(names, got):
        return {n: _join(g, n) for n, g in zip(names, got)}

    w = _pad_first(joined(FIRST, _chip_exchange([shards[n][0] for n in FIRST], False, "gather_weights")))
    layers, saved = [], []
    for li, s in enumerate(smalls):
        gq, gk = _pad_lanes(s["q_norm_g"], HEAD), _pad_lanes(s["k_norm_g"], HEAD)
        gqa, gkva = s["q_a_norm_g"].reshape(1, -1), s["kv_a_norm_g"].reshape(1, -1)
        g1, g2, g3, g4 = (s[n].reshape(1, D) for n in ("pre_mix_g", "post_mix_g", "pre_ffn_g", "post_ffn_g"))
        b = s["b_gate"].reshape(1, ZG_W)
        u, zm, gate = _fwd_inproj(x, g1, w["w1"], w["w2"], b)
        qa, ka, va, qb, kb, vb, cqn, ckvn = _fwd_mid(zm, gq, gk, gqa, gkva, w["wq"], w["wk"], w["wv"], ca, sa, cb, sb)
        last = li + 1 == nl
        gather = _ChipExchange([shards[n][li] for n in REST] + ([] if last else [shards[n][li + 1] for n in FIRST]), False)
        ya, lse_a, got = _flash_fwd(qa, ka, va, GQA_G, "flash_fwd_gqa", gather)
        yb, lse_b, _ = _flash_fwd(qb, kb, vb, 1, "flash_fwd_mla")
        w = {**w, **_pad_rest(joined(REST, got[:len(REST)]))}
        pa, pb, mg, m, x1 = _fwd_merge(ya, yb, gate, w["wa"], w["wb"], w["wo"], x, g2)
        hn, a = _fwd_ffn_up(x1, g3, w["wup"])
        f, x2 = _fwd_ffn_down(a, w["wdn"], x1, g4)
        saved.append(dict(x=x, u=u, zm=zm, gate=gate, qa=qa, ka=ka, va=va, qb=qb, kb=kb, vb=vb, cqn=cqn, ckvn=ckvn,
                          ya=ya, lse_a=lse_a, yb=yb, lse_b=lse_b, pa=pa, pb=pb, mg=mg,
                          m=m, x1=x1, hn=hn, a=a, f=f, gq=gq, gk=gk, gqa=gqa, gkva=gkva, g1=g1, g2=g2, g3=g3,
                          g4=g4))
        layers.append(w)
        x = x2
        if not last:
            w = _pad_first(joined(FIRST, got[len(REST):]))

    dx, loss8 = _loss_head(x, target)
    loss = loss8[0, 0]

    received, small_grads, send_first = [dict() for _ in range(nl)], [None] * nl, None
    for li in range(nl - 1, -1, -1):
        w, r = layers[li], saved[li]
        df, dh, dg4 = _bwd_ffn_down(r["f"], dx, r["g4"], w["wdn"], r["a"])
        dx1, dg3 = _bwd_ffn_up(dh, w["wup"], r["x1"], r["g3"], dx)
        dm, dpa, dpb, dzg, dya, dyb, dg2, db = _bwd_merge(r["m"], dx1, r["g2"], w["wo"], r["gate"], r["pa"], r["pb"],
                                                         w["wa"], w["wb"])
        nat = _unpad_rest(dict(
            wa=_matmul_tn(r["ya"], dpa, 512, "dw_branch_a"), wb=_matmul_tn(r["yb"], dpb, 512, "dw_branch_b"),
            wo=_matmul_tn(r["mg"], dm, 512, "dw_o"), wup=_matmul_tn(r["hn"], dh, 512, "dw_ffn_up"),
            wdn=_matmul_tn(r["a"], df, 512, "dw_ffn_down")))
        scatter = _ChipExchange([_split(nat[n], n) for n in REST] + (send_first or []), True)
        dqa, dkta, dvta, got = _flash_bwd(r["qa"], dya, r["ya"], r["lse_a"], r["ka"], r["va"], GQA_G, SCALE_GQA,
                                          "flash_bwd_gqa", scatter)
        received[li].update(zip(REST, got[:len(REST)]))
        if send_first:
            received[li + 1].update(zip(FIRST, got[len(REST):]))
        dqb, dktb, dvtb, _ = _flash_bwd(r["qb"], dyb, r["yb"], r["lse_b"], r["kb"], r["vb"], 1, SCALE_MLA, "flash_bwd_mla")
        dzm, dqbp, dkb16, dvb16, dgq, dgk, dgqa, dgkva = _bwd_mid(
            r["zm"], dqa, dkta, dvta, dqb, dktb, dvtb, r["gq"], r["gk"], r["gqa"], r["gkva"], w["wq"], w["wk"], w["wv"],
            ca, sa, cb, sb)
        dx, dg1 = _bwd_inproj(dzm, dzg, w["w1"], w["w2"], r["x"], r["g1"], dx1)
        nat = _unpad_first(dict(
            w1=_matmul_tn(r["u"], dzm, 768, "dw_in_main"), w2=_matmul_tn(r["u"], dzg, 512, "dw_in_gate"),
            wq=_matmul_tn(r["cqn"], dqbp, 512, "dw_q_up"), wk=_matmul_tn(r["ckvn"], dkb16, 512, "dw_k_up"),
            wv=_matmul_tn(r["ckvn"], dvb16, 512, "dw_v_up")))
        send_first = [_split(nat[n], n) for n in FIRST]
        small_grads[li] = dict(b_gate=db[0], q_norm_g=dgq[0, :HEAD], k_norm_g=dgk[0, :HEAD], q_a_norm_g=dgqa[0],
                               kv_a_norm_g=dgkva[0], pre_mix_g=dg1[0], post_mix_g=dg2[0], pre_ffn_g=dg3[0],
                               post_ffn_g=dg4[0])
    received[0].update(zip(FIRST, _chip_exchange(send_first, True, "scatter_grads")))
    return loss, dx, received, small_grads


def _chip_exchange(srcs, per_dest, name):
    ex = _ChipExchange(srcs, per_dest)

    def body(*refs):
        _, _, ex_refs = _hosted(ex, refs, 0, 0)
        ex.start(*ex_refs)
        ex.finish(*ex_refs)

    return pl.pallas_call(
        body, name=name, in_specs=ex.specs, out_specs=ex.specs, out_shape=ex.out_shape, scratch_shapes=ex.scratch,
    )(*ex.srcs)


def _sibling_exchange(srcs, name):
    n = len(srcs)

    def body(*refs):
        src_refs, out_refs = refs[:n], refs[n:2 * n]
        send_sems, recv_sems = refs[2 * n:]
        x, y, c = lax.axis_index("x"), lax.axis_index("y"), lax.axis_index("c")
        cps = [pltpu.make_async_remote_copy(src_ref=src_refs[a], dst_ref=out_refs[a], send_sem=send_sems.at[a],
                                            recv_sem=recv_sems.at[a], device_id=(x, y, 1 - c), device_id_type=MESH)
               for a in range(n)]
        for cp in cps:
            cp.start()
        for cp in cps:
            cp.wait()

    return pl.pallas_call(
        body, name=name,
        in_specs=[pl.BlockSpec(memory_space=pl.ANY)] * n,
        out_specs=[pl.BlockSpec(memory_space=pl.ANY)] * n,
        out_shape=[jax.ShapeDtypeStruct(s.shape, s.dtype) for s in srcs],
        scratch_shapes=[pltpu.SemaphoreType.DMA((n,)), pltpu.SemaphoreType.DMA((n,))],
    )(*srcs)


def _allgather_small(v):
    m_per, n = v.shape

    def body(x_ref, out_ref, send_sems, recv_sems, local_sem):
        x, y, c = lax.axis_index("x"), lax.axis_index("y"), lax.axis_index("c")
        me, sibling = (x, y, c), (x, y, 1 - c)
        chips = [(1 - x, y), (x, 1 - y), (1 - x, 1 - y)]

        def rows(px, py, pc):
            return out_ref.at[pl.ds((4 * px + 2 * py + pc) * m_per, m_per), :]

        def copy(k, block, to, src=None):
            return pltpu.make_async_remote_copy(
                src_ref=rows(*block) if src is None else src, dst_ref=rows(*block),
                send_sem=send_sems.at[k], recv_sem=recv_sems.at[k], device_id=to, device_id_type=MESH)

        mine = pltpu.make_async_copy(x_ref, rows(*me), local_sem)
        mine.start()
        first = [copy(0, me, sibling, src=x_ref)]
        first += [copy(1 + j, me, (*chip, c), src=x_ref) for j, chip in enumerate(chips)]
        for cp in first:
            cp.start()
        passed = [copy(4 + j, (*chip, c), sibling) for j, chip in enumerate(chips)]
        for j, chip in enumerate(chips):
            copy(1 + j, (*chip, c), me).wait_recv()
            passed[j].start()
        copy(0, sibling, me).wait_recv()
        for j, chip in enumerate(chips):
            copy(4 + j, (*chip, 1 - c), me).wait_recv()
        for cp in first + passed:
            cp.wait_send()
        mine.wait()

    return pl.pallas_call(
        body, name="allgather_small",
        out_shape=jax.ShapeDtypeStruct((8 * m_per, n), v.dtype),
        in_specs=[pl.BlockSpec(memory_space=pltpu.VMEM)],
        out_specs=pl.BlockSpec(memory_space=pltpu.VMEM),
        scratch_shapes=[pltpu.SemaphoreType.DMA((7,)), pltpu.SemaphoreType.DMA((7,)), pltpu.SemaphoreType.DMA],
    )(v)


def _pick_rows(r):
    return next(t for t in (512, 256, 128, 64, 32, 16, 8, r) if r % t == 0)


def _sum_slabs(a, name):
    s, r, c = a.shape
    tm = _pick_rows(r)

    def body(a_ref, o_ref):
        acc = a_ref[0].astype(F32)
        for k in range(1, s):
            acc = acc + a_ref[k].astype(F32)
        o_ref[...] = acc

    return pl.pallas_call(
        body, name=name, grid=(r // tm,),
        in_specs=[pl.BlockSpec((s, tm, c), lambda i: (0, i, 0))],
        out_specs=_rows(tm, c),
        out_shape=jax.ShapeDtypeStruct((r, c), F32),
        compiler_params=_params(("parallel",)),
    )(a)


def _adamw(w, ga, gb, m, v, name):
    r, c = w.shape
    tm = min(256, _pick_rows(r))

    def body(w_ref, ga_ref, gb_ref, m_ref, v_ref, g_ref, d_ref, nm_ref, nv_ref):
        gv = ga_ref[...] + gb_ref[...]
        g_ref[...] = gv
        nm = ADAM_B1 * m_ref[...] + (1.0 - ADAM_B1) * gv
        nv = ADAM_B2 * v_ref[...] + (1.0 - ADAM_B2) * (gv * gv)
        m_hat = nm / (1.0 - ADAM_B1 ** ADAM_STEP)
        v_hat = nv / (1.0 - ADAM_B2 ** ADAM_STEP)
        d_ref[...] = -ADAM_LR * (m_hat / (jnp.sqrt(v_hat) + ADAM_EPS) + ADAM_WD * w_ref[...])
        nm_ref[...] = nm
        nv_ref[...] = nv

    spec = _rows(tm, c)
    return pl.pallas_call(
        body, name=name, grid=(r // tm,),
        in_specs=[spec] * 5, out_specs=[spec] * 4,
        out_shape=[jax.ShapeDtypeStruct((r, c), F32)] * 4,
        compiler_params=_params(("parallel",)),
    )(w, ga, gb, m, v)


def _adamw_small(w, gparts, m, v):
    mrows, n = w.shape

    def body(w_ref, g_ref, m_ref, v_ref, go_ref, d_ref, nm_ref, nv_ref):
        gv = g_ref[0]
        for k in range(1, 8):
            gv = gv + g_ref[k]
        go_ref[...] = gv
        nm = ADAM_B1 * m_ref[...] + (1.0 - ADAM_B1) * gv
        nv = ADAM_B2 * v_ref[...] + (1.0 - ADAM_B2) * (gv * gv)
        m_hat = nm / (1.0 - ADAM_B1 ** ADAM_STEP)
        v_hat = nv / (1.0 - ADAM_B2 ** ADAM_STEP)
        d_ref[...] = -ADAM_LR * (m_hat / (jnp.sqrt(v_hat) + ADAM_EPS) + ADAM_WD * w_ref[...])
        nm_ref[...] = nm
        nv_ref[...] = nv

    return pl.pallas_call(
        body, name="adamw_small",
        out_shape=[jax.ShapeDtypeStruct((mrows, n), F32)] * 4,
    )(w, gparts, m, v)


def _flat(parts):
    nl = parts[0].shape[0]
    return jnp.concatenate([p.reshape(nl, -1) for p in parts], axis=1)


def _unflat(flat, shapes):
    out, off = [], 0
    nl = flat.shape[0]
    for s in shapes:
        n = math.prod(s)
        out.append(flat[:, off:off + n].reshape((nl,) + tuple(s)))
        off += n
    return out


def kernel(x, w_in, b_gate, q_norm_g, k_norm_g, q_a_norm_g, kv_a_norm_g, w_q_up, w_kv_up, w_branch_a, w_branch_b, w_o, w_ffn_up, w_ffn_down, pre_mix_g, post_mix_g, pre_ffn_g, post_ffn_g, loss_target, m_w_in, m_b_gate, m_q_norm_g, m_k_norm_g, m_q_a_norm_g, m_kv_a_norm_g, m_w_q_up, m_w_kv_up, m_w_branch_a, m_w_branch_b, m_w_o, m_w_ffn_up, m_w_ffn_down, m_pre_mix_g, m_post_mix_g, m_pre_ffn_g, m_post_ffn_g, v_w_in, v_b_gate, v_q_norm_g, v_k_norm_g, v_q_a_norm_g, v_kv_a_norm_g, v_w_q_up, v_w_kv_up, v_w_branch_a, v_w_branch_b, v_w_o, v_w_ffn_up, v_w_ffn_down, v_pre_mix_g, v_post_mix_g, v_pre_ffn_g, v_post_ffn_g):
    wts = dict(w_in=w_in, b_gate=b_gate, q_norm_g=q_norm_g, k_norm_g=k_norm_g, q_a_norm_g=q_a_norm_g,
               kv_a_norm_g=kv_a_norm_g, w_q_up=w_q_up, w_kv_up=w_kv_up, w_branch_a=w_branch_a, w_branch_b=w_branch_b,
               w_o=w_o, w_ffn_up=w_ffn_up, w_ffn_down=w_ffn_down, pre_mix_g=pre_mix_g, post_mix_g=post_mix_g,
               pre_ffn_g=pre_ffn_g, post_ffn_g=post_ffn_g)
    mom = dict(w_in=m_w_in, b_gate=m_b_gate, q_norm_g=m_q_norm_g, k_norm_g=m_k_norm_g, q_a_norm_g=m_q_a_norm_g,
               kv_a_norm_g=m_kv_a_norm_g, w_q_up=m_w_q_up, w_kv_up=m_w_kv_up, w_branch_a=m_w_branch_a,
               w_branch_b=m_w_branch_b, w_o=m_w_o, w_ffn_up=m_w_ffn_up, w_ffn_down=m_w_ffn_down, pre_mix_g=m_pre_mix_g,
               post_mix_g=m_post_mix_g, pre_ffn_g=m_pre_ffn_g, post_ffn_g=m_post_ffn_g)
    var = dict(w_in=v_w_in, b_gate=v_b_gate, q_norm_g=v_q_norm_g, k_norm_g=v_k_norm_g, q_a_norm_g=v_q_a_norm_g,
               kv_a_norm_g=v_kv_a_norm_g, w_q_up=v_w_q_up, w_kv_up=v_w_kv_up, w_branch_a=v_w_branch_a,
               w_branch_b=v_w_branch_b, w_o=v_w_o, w_ffn_up=v_w_ffn_up, w_ffn_down=v_w_ffn_down, pre_mix_g=v_pre_mix_g,
               post_mix_g=v_post_mix_g, pre_ffn_g=v_pre_ffn_g, post_ffn_g=v_post_ffn_g)
    nl = w_in.shape[0]

    shards = {n: wts[n].astype(BF) for n in BIG}
    smalls = [{n: wts[n][li] for n in SMALL} for li in range(nl)]
    loss_local, dx, received, small_grads = _local_step(x[0], loss_target[0], shards, smalls)
    loss = lax.psum(loss_local, ("x", "y", "c"))
    part = []
    for i, n in enumerate(BIG):
        got = jnp.stack([received[li][n] for li in range(nl)], axis=1)
        part.append(_sum_slabs(got.reshape(4, -1, got.shape[-1]), "sum_chips_" + n))
    other = _sibling_exchange(part, "swap_cores")

    small_shapes = [wts[n].shape[1:] for n in SMALL]
    g_loc = _flat([jnp.stack([small_grads[li][n] for li in range(nl)]) for n in SMALL]).reshape(-1, 128)
    g_all = _allgather_small(g_loc).reshape(8, -1, 128)
    pack = lambda d: _flat([d[n] for n in SMALL]).reshape(-1, 128)
    gs, ds, ms, vs = _adamw_small(pack(wts), g_all, pack(mom), pack(var))
    unpack = lambda a: dict(zip(SMALL, _unflat(a.reshape(nl, -1), small_shapes)))
    g_small, d_small, m_small, v_small = unpack(gs), unpack(ds), unpack(ms), unpack(vs)

    out_g, out_d, out_m, out_v = dict(g_small), dict(d_small), dict(m_small), dict(v_small)
    for i, n in enumerate(BIG):
        shp = wts[n].shape
        two = lambda a: a.reshape(-1, shp[-1])
        g, d, nm, nv = _adamw(two(wts[n]), part[i], other[i], two(mom[n]), two(var[n]), "adamw_" + n)
        out_g[n], out_d[n], out_m[n], out_v[n] = g.reshape(shp), d.reshape(shp), nm.reshape(shp), nv.reshape(shp)

    return (loss, dx[None], *[out_g[n] for n in ORDER], *[out_d[n] for n in ORDER], *[out_m[n] for n in ORDER],
            *[out_v[n] for n in ORDER])
```

```python
import functools
import math

import jax
import jax.numpy as jnp
from jax import lax
from jax.experimental import pallas as pl
from jax.experimental.pallas import tpu as pltpu

F32 = jnp.float32
BF = jnp.bfloat16
MESH = pl.DeviceIdType.MESH

EPS = 1e-6
D = 1024
NH = 8
HW = 128
GQA_KV = 2
GQA_G = 4
HEAD = 64
MLA_NOPE = 64
MLA_ROPE = 32
MLA_QK = 96
Q_RANK = 384
KV_RANK = 256
DFF = 4096
GRID_W = 64
ROPE_THETA = 10000.0

O_QA, O_KA, O_VA, O_CQ, O_CKV, O_KR, ZM_W = 0, 1024, 1280, 1536, 1920, 2176, 2304
ZG_W = 2048

ADAM_LR, ADAM_B1, ADAM_B2, ADAM_EPS, ADAM_WD, ADAM_STEP = 0.001, 0.9, 0.999, 1e-08, 0.01, 10

TM = 256
TQ = 256
TQ_FWD_STEP = 1024
TQ_BWD_STEP = 512
ONES_LANE = 64
LOG2E = 1.4426950408889634
LN2 = 0.6931471805599453
SCALE_GQA = 1.0 / math.sqrt(HEAD)
SCALE_MLA = 1.0 / math.sqrt(MLA_QK)
C2_GQA = SCALE_GQA * LOG2E
C2_MLA = SCALE_MLA * LOG2E
VMEM_LIMIT = 56 * 1024 * 1024
TN_ROWS_NARROW, TN_ROWS_WIDE = 2048, 1024

BIG = ("w_in", "w_q_up", "w_kv_up", "w_branch_a", "w_branch_b", "w_o", "w_ffn_up", "w_ffn_down")
FIRST = BIG[:3]
REST = BIG[3:]
SMALL = ("b_gate", "q_norm_g", "k_norm_g", "q_a_norm_g", "kv_a_norm_g", "pre_mix_g", "post_mix_g", "pre_ffn_g", "post_ffn_g")
ORDER = ("w_in", "b_gate", "q_norm_g", "k_norm_g", "q_a_norm_g", "kv_a_norm_g", "w_q_up", "w_kv_up", "w_branch_a",
         "w_branch_b", "w_o", "w_ffn_up", "w_ffn_down", "pre_mix_g", "post_mix_g", "pre_ffn_g", "post_ffn_g")


def _params(sem=None):
    return pltpu.CompilerParams(dimension_semantics=sem, vmem_limit_bytes=VMEM_LIMIT)


def _rows(tm, w):
    return pl.BlockSpec((tm, w), lambda i: (i, 0))


def _cols(h, tm):
    return pl.BlockSpec((h, tm), lambda i: (0, i))


def _whole(shape):
    return pl.BlockSpec(shape, lambda i: (0,) * len(shape))


def _dot(a, b):
    return jnp.dot(a, b, preferred_element_type=F32)


def _dot_nt(a, b):
    return lax.dot_general(a, b, (((1,), (1,)), ((), ())), preferred_element_type=F32)


def _dot_tn(a, b):
    return lax.dot_general(a, b, (((0,), (0,)), ((), ())), preferred_element_type=F32)


def _norm_fwd(xv, g, n=None):
    n = xv.shape[-1] if n is None else n
    r = lax.rsqrt(jnp.sum(xv * xv, axis=-1, keepdims=True) * (1.0 / n) + EPS)
    return (xv * r) * g


def _norm_bwd(xv, g, dy, n=None):
    n = xv.shape[-1] if n is None else n
    r = lax.rsqrt(jnp.sum(xv * xv, axis=-1, keepdims=True) * (1.0 / n) + EPS)
    xh = xv * r
    dxh = dy * g
    dg = jnp.sum(dy * xh, axis=0, keepdims=True)
    dx = r * (dxh - xh * (jnp.sum(dxh * xh, axis=-1, keepdims=True) * (1.0 / n)))
    return dx, dg


def _accumulate(ref, val):
    @pl.when(pl.program_id(0) == 0)
    def _():
        ref[...] = jnp.zeros_like(ref)

    ref[...] += val


def _rot(xv):
    lane = lax.broadcasted_iota(jnp.int32, xv.shape, 1)
    return jnp.where(lane < HW // 2, -1.0, 1.0) * pltpu.roll(xv, HW // 2, 1)


def _rope(xv, c, s):
    return xv * c + _rot(xv) * s


def _rope_bwd(dy, c, s):
    return dy * c - _rot(dy * s)


def _sigmoid(z):
    return 1.0 / (1.0 + jnp.exp(-z))


def _fwd_inproj(x, g1, w1, w2, b):
    t = x.shape[0]

    def body(x_ref, g_ref, w1_ref, w2_ref, b_ref, u_ref, zm_ref, gate_ref):
        u = _norm_fwd(x_ref[...], g_ref[...]).astype(BF)
        u_ref[...] = u
        for c in range(0, ZM_W, 768):
            zm_ref[:, c:c + 768] = _dot(u, w1_ref[:, c:c + 768])
        for c in range(0, ZG_W, 512):
            gate_ref[:, c:c + 512] = _sigmoid(_dot(u, w2_ref[:, c:c + 512]) + b_ref[:, c:c + 512]).astype(BF)

    return pl.pallas_call(
        body, name="fwd_inproj", grid=(t // TM,),
        in_specs=[_rows(TM, D), _whole((1, D)), _whole((D, ZM_W)), _whole((D, ZG_W)), _whole((1, ZG_W))],
        out_specs=[_rows(TM, D), _rows(TM, ZM_W), _rows(TM, ZG_W)],
        out_shape=[jax.ShapeDtypeStruct((t, D), BF), jax.ShapeDtypeStruct((t, ZM_W), F32),
                   jax.ShapeDtypeStruct((t, ZG_W), BF)],
        compiler_params=_params(("parallel",)),
    )(x, g1, w1, w2, b)


def _fwd_mid(zm, gq, gk, gqa, gkva, wq, wk, wv, ca, sa, cb, sb):
    t = zm.shape[0]

    def body(zm_ref, gq_ref, gk_ref, gqa_ref, gkva_ref, wq_ref, wk_ref, wv_ref, ca_ref, sa_ref, cb_ref, sb_ref,
             qa_ref, ka_ref, va_ref, qb_ref, kb_ref, vb_ref, cqn_ref, ckvn_ref):
        ca_, sa_, cb_, sb_ = ca_ref[...], sa_ref[...], cb_ref[...], sb_ref[...]
        for h in range(NH):
            xv = zm_ref[:, O_QA + h * HW:O_QA + (h + 1) * HW]
            qa_ref[:, h * HW:(h + 1) * HW] = (_rope(_norm_fwd(xv, gq_ref[...], HEAD), ca_, sa_) * C2_GQA).astype(BF)
        for h in range(GQA_KV):
            xv = zm_ref[:, O_KA + h * HW:O_KA + (h + 1) * HW]
            ka_ref[:, h * HW:(h + 1) * HW] = _rope(_norm_fwd(xv, gk_ref[...], HEAD), ca_, sa_).astype(BF)
        ones = lax.broadcasted_iota(jnp.int32, (TM, HW), 1) == ONES_LANE
        for h in range(GQA_KV):
            va_ref[:, h * HW:(h + 1) * HW] = jnp.where(ones, 1.0, zm_ref[:, O_VA + h * HW:O_VA + (h + 1) * HW]).astype(BF)
        cqn = _norm_fwd(zm_ref[:, O_CQ:O_CQ + Q_RANK], gqa_ref[...]).astype(BF)
        ckvn = _norm_fwd(zm_ref[:, O_CKV:O_CKV + KV_RANK], gkva_ref[...]).astype(BF)
        cqn_ref[...] = cqn
        ckvn_ref[...] = ckvn
        qb = _dot(cqn, wq_ref[...])
        kpre = _dot(ckvn, wk_ref[...])
        kr = _rope(zm_ref[:, O_KR:O_KR + HW], cb_, sb_)
        for h in range(NH):
            sl = slice(h * HW, (h + 1) * HW)
            qb_ref[:, sl] = (_rope(qb[:, sl], cb_, sb_) * C2_MLA).astype(BF)
            kb_ref[:, sl] = (kpre[:, sl] + kr).astype(BF)
        vb = _dot(ckvn, wv_ref[...])
        for h in range(NH):
            vb_ref[:, h * HW:(h + 1) * HW] = jnp.where(ones, 1.0, vb[:, h * HW:(h + 1) * HW]).astype(BF)

    hw8 = NH * HW
    return pl.pallas_call(
        body, name="fwd_mid", grid=(t // TM,),
        in_specs=[_rows(TM, ZM_W), _whole((1, HW)), _whole((1, HW)), _whole((1, Q_RANK)), _whole((1, KV_RANK)),
                  _whole((Q_RANK, hw8)), _whole((KV_RANK, hw8)), _whole((KV_RANK, hw8)),
                  _rows(TM, HW), _rows(TM, HW), _rows(TM, HW), _rows(TM, HW)],
        out_specs=[_rows(TM, hw8), _rows(TM, GQA_KV * HW), _rows(TM, GQA_KV * HW), _rows(TM, hw8), _rows(TM, hw8),
                   _rows(TM, hw8), _rows(TM, Q_RANK), _rows(TM, KV_RANK)],
        out_shape=[jax.ShapeDtypeStruct((t, hw8), BF), jax.ShapeDtypeStruct((t, GQA_KV * HW), BF),
                   jax.ShapeDtypeStruct((t, GQA_KV * HW), BF), jax.ShapeDtypeStruct((t, hw8), BF),
                   jax.ShapeDtypeStruct((t, hw8), BF), jax.ShapeDtypeStruct((t, hw8), BF),
                   jax.ShapeDtypeStruct((t, Q_RANK), BF), jax.ShapeDtypeStruct((t, KV_RANK), BF)],
        compiler_params=_params(("parallel",)),
    )(zm, gq, gk, gqa, gkva, wq, wk, wv, ca, sa, cb, sb)


class _ChipExchange:
    def __init__(self, srcs, per_dest):
        self.srcs, self.per_dest, self.n = list(srcs), per_dest, len(srcs)
        self.out_shape = [jax.ShapeDtypeStruct((4,) + tuple(s.shape[1:] if per_dest else s.shape), s.dtype) for s in srcs]
        self.specs = [pl.BlockSpec(memory_space=pl.ANY)] * self.n
        self.scratch = [pltpu.SemaphoreType.DMA((3 * self.n,)), pltpu.SemaphoreType.DMA((3 * self.n,)),
                        pltpu.SemaphoreType.DMA((self.n,))]

    def _copies(self, src_refs, out_refs, send_sems, recv_sems, local_sems):
        x, y, c = lax.axis_index("x"), lax.axis_index("y"), lax.axis_index("c")
        me = 2 * x + y
        chips = [(1 - x, y), (x, 1 - y), (1 - x, 1 - y)]

        def piece(a, k):
            return src_refs[a].at[k] if self.per_dest else src_refs[a]

        def remote(a, j, src_chip, dst_slab):
            px, py = chips[j]
            return pltpu.make_async_remote_copy(
                src_ref=piece(a, src_chip), dst_ref=out_refs[a].at[dst_slab], send_sem=send_sems.at[3 * a + j],
                recv_sem=recv_sems.at[3 * a + j], device_id=(px, py, c), device_id_type=MESH)

        local = [pltpu.make_async_copy(piece(a, me), out_refs[a].at[me], local_sems.at[a]) for a in range(self.n)]
        sends = [remote(a, j, 2 * chips[j][0] + chips[j][1], me) for a in range(self.n) for j in range(3)]
        recvs = [remote(a, j, me, 2 * chips[j][0] + chips[j][1]) for a in range(self.n) for j in range(3)]
        return local, sends, recvs

    def start(self, *refs):
        local, sends, _ = self._copies(*refs)
        for cp in local + sends:
            cp.start()

    def finish(self, *refs):
        local, sends, recvs = self._copies(*refs)
        for cp in recvs:
            cp.wait_recv()
        for cp in sends:
            cp.wait_send()
        for cp in local:
            cp.wait()


def _hosted(exchange, refs, n_in, n_out):
    n = exchange.n if exchange else 0
    ins, srcs = refs[:n_in], refs[n_in:n_in + n]
    outs = refs[n_in + n:n_in + n + n_out]
    rest = refs[n_in + n + n_out:]
    return ins, outs, (tuple(srcs), tuple(rest[:n])) + tuple(rest[n:])


def _flash_fwd(q, k, v, group, name, exchange=None):
    t = q.shape[0]
    tq = min(TQ_FWD_STEP, t)
    nq = t // tq

    def body(*refs):
        (q_ref, k_ref, v_ref), (o_ref, lse_ref), ex_refs = _hosted(exchange, refs, 3, 2)
        if exchange:
            @pl.when((pl.program_id(0) == 0) & (pl.program_id(1) == 0))
            def _():
                exchange.start(*ex_refs)

        for r0 in range(0, tq, TQ):
            rows = slice(r0, r0 + TQ)
            s = _dot_nt(q_ref[rows, :], k_ref[...])
            m = jnp.max(s, axis=-1, keepdims=True)
            acc = _dot(jnp.exp2(s - m).astype(BF), v_ref[...])
            l = acc[:, ONES_LANE:ONES_LANE + 1]
            o_ref[rows, :] = (acc / l).astype(BF)
            lse_ref[rows, :] = jnp.broadcast_to(m + jnp.log2(l), (TQ, HW))
        if exchange:
            @pl.when((pl.program_id(0) == NH - 1) & (pl.program_id(1) == nq - 1))
            def _():
                exchange.finish(*ex_refs)

    qspec = pl.BlockSpec((tq, HW), lambda h, i: (i, h))
    kspec = pl.BlockSpec((t, HW), lambda h, i: (0, h // group))
    out = pl.pallas_call(
        body, name=name, grid=(NH, nq),
        in_specs=[qspec, kspec, kspec] + (exchange.specs if exchange else []),
        out_specs=[qspec, qspec] + (exchange.specs if exchange else []),
        out_shape=[jax.ShapeDtypeStruct((t, NH * HW), BF), jax.ShapeDtypeStruct((t, NH * HW), F32)]
        + (exchange.out_shape if exchange else []),
        scratch_shapes=exchange.scratch if exchange else [],
        compiler_params=_params(("arbitrary", "arbitrary") if exchange else ("parallel", "parallel")),
    )(q, k, v, *(exchange.srcs if exchange else []))
    return out[0], out[1], out[2:]


def _fwd_merge(ya, yb, gate, wa, wb, wo, x, g2):
    t = x.shape[0]

    def body(ya_ref, yb_ref, gate_ref, wa_ref, wb_ref, wo_ref, x_ref, g_ref, pa_ref, pb_ref, mg_ref, m_ref, x1_ref):
        pa = _dot(ya_ref[...], wa_ref[...])
        pb = _dot(yb_ref[...], wb_ref[...])
        pa_ref[...] = pa.astype(BF)
        pb_ref[...] = pb.astype(BF)
        mg = (gate_ref[:, :D].astype(F32) * pa + gate_ref[:, D:].astype(F32) * pb).astype(BF)
        mg_ref[...] = mg
        m = _dot(mg, wo_ref[...])
        m_ref[...] = m
        x1_ref[...] = x_ref[...] + _norm_fwd(m, g_ref[...])

    return pl.pallas_call(
        body, name="fwd_merge", grid=(t // TM,),
        in_specs=[_rows(TM, D), _rows(TM, D), _rows(TM, ZG_W), _whole((D, D)), _whole((D, D)),
                  _whole((D, D)), _rows(TM, D), _whole((1, D))],
        out_specs=[_rows(TM, D), _rows(TM, D), _rows(TM, D), _rows(TM, D), _rows(TM, D)],
        out_shape=[jax.ShapeDtypeStruct((t, D), BF), jax.ShapeDtypeStruct((t, D), BF), jax.ShapeDtypeStruct((t, D), BF),
                   jax.ShapeDtypeStruct((t, D), F32), jax.ShapeDtypeStruct((t, D), F32)],
        compiler_params=_params(("parallel",)),
    )(ya, yb, gate, wa, wb, wo, x, g2)


def _fwd_ffn_up(x1, g3, wup):
    t = x1.shape[0]

    def body(x_ref, g_ref, w_ref, hn_ref, a_ref, r2_ref):
        hn = _norm_fwd(x_ref[...], g_ref[...]).astype(BF)
        hn_ref[...] = hn
        for c in range(0, DFF, 1024):
            h = _dot(hn, w_ref[:, c:c + 1024])
            r = jnp.maximum(h, 0.0)
            a_ref[:, c:c + 1024] = (r * r).astype(BF)
            r2_ref[:, c:c + 1024] = (2.0 * r).astype(BF)

    return pl.pallas_call(
        body, name="fwd_ffn_up", grid=(t // TM,),
        in_specs=[_rows(TM, D), _whole((1, D)), _whole((D, DFF))],
        out_specs=[_rows(TM, D), _rows(TM, DFF), _rows(TM, DFF)],
        out_shape=[jax.ShapeDtypeStruct((t, D), BF), jax.ShapeDtypeStruct((t, DFF), BF), jax.ShapeDtypeStruct((t, DFF), BF)],
        compiler_params=_params(("parallel",)),
    )(x1, g3, wup)


def _fwd_ffn_down(a, wdn, x1, g4):
    t = x1.shape[0]

    def body(a_ref, w_ref, x_ref, g_ref, f_ref, x2_ref):
        f = _dot(a_ref[...], w_ref[...])
        f_ref[...] = f
        x2_ref[...] = x_ref[...] + _norm_fwd(f, g_ref[...])

    return pl.pallas_call(
        body, name="fwd_ffn_down", grid=(t // TM,),
        in_specs=[_rows(TM, DFF), _whole((DFF, D)), _rows(TM, D), _whole((1, D))],
        out_specs=[_rows(TM, D), _rows(TM, D)],
        out_shape=[jax.ShapeDtypeStruct((t, D), F32), jax.ShapeDtypeStruct((t, D), F32)],
        compiler_params=_params(("parallel",)),
    )(a, wdn, x1, g4)


def _loss_head(y, target):
    t = y.shape[0]

    def body(y_ref, t_ref, dy_ref, loss_ref):
        d = y_ref[...] - t_ref[...]
        dy_ref[...] = d * (1.0 / D)
        part = 0.5 * jnp.sum(jnp.sum(d * d, axis=-1, keepdims=True) * (1.0 / D), axis=0, keepdims=True)
        _accumulate(loss_ref, jnp.broadcast_to(part, (8, HW)))

    return pl.pallas_call(
        body, name="loss_head", grid=(t // TM,),
        in_specs=[_rows(TM, D), _rows(TM, D)],
        out_specs=[_rows(TM, D), _whole((8, HW))],
        out_shape=[jax.ShapeDtypeStruct((t, D), F32), jax.ShapeDtypeStruct((8, HW), F32)],
        compiler_params=_params(("arbitrary",)),
    )(y, target)


def _bwd_ffn_down(f, dx2, g4, wdn, r2):
    t = f.shape[0]

    def body(f_ref, dx2_ref, g_ref, w_ref, r2_ref, df_ref, dh_ref, dg_ref):
        df, dg = _norm_bwd(f_ref[...], g_ref[...], dx2_ref[...])
        _accumulate(dg_ref, dg)
        df16 = df.astype(BF)
        df_ref[...] = df16
        for c in range(0, DFF, 1024):
            da = _dot_nt(df16, w_ref[c:c + 1024, :])
            dh_ref[:, c:c + 1024] = (da * r2_ref[:, c:c + 1024].astype(F32)).astype(BF)

    return pl.pallas_call(
        body, name="bwd_ffn_down", grid=(t // TM,),
        in_specs=[_rows(TM, D), _rows(TM, D), _whole((1, D)), _whole((DFF, D)), _rows(TM, DFF)],
        out_specs=[_rows(TM, D), _rows(TM, DFF), _whole((1, D))],
        out_shape=[jax.ShapeDtypeStruct((t, D), BF), jax.ShapeDtypeStruct((t, DFF), BF), jax.ShapeDtypeStruct((1, D), F32)],
        compiler_params=_params(("arbitrary",)),
    )(f, dx2, g4, wdn, r2)


def _bwd_ffn_up(dh, wup, x1, g3, dx2):
    t = x1.shape[0]

    def body(dh_ref, w_ref, x_ref, g_ref, dx2_ref, dx1_ref, dg_ref):
        dhn = _dot_nt(dh_ref[...], w_ref[...])
        dx, dg = _norm_bwd(x_ref[...], g_ref[...], dhn)
        _accumulate(dg_ref, dg)
        dx1_ref[...] = dx2_ref[...] + dx

    return pl.pallas_call(
        body, name="bwd_ffn_up", grid=(t // TM,),
        in_specs=[_rows(TM, DFF), _whole((D, DFF)), _rows(TM, D), _whole((1, D)), _rows(TM, D)],
        out_specs=[_rows(TM, D), _whole((1, D))],
        out_shape=[jax.ShapeDtypeStruct((t, D), F32), jax.ShapeDtypeStruct((1, D), F32)],
        compiler_params=_params(("arbitrary",)),
    )(dh, wup, x1, g3, dx2)


def _bwd_merge(m, dx1, g2, wo, gate, pa, pb, wa, wb):
    t = m.shape[0]

    def body(m_ref, dx1_ref, g_ref, wo_ref, gate_ref, pa_ref, pb_ref, wa_ref, wb_ref,
             dm_ref, dpa_ref, dpb_ref, dzg_ref, dya_ref, dyb_ref, dg_ref, db_ref):
        dm, dg = _norm_bwd(m_ref[...], g_ref[...], dx1_ref[...])
        _accumulate(dg_ref, dg)
        dm16 = dm.astype(BF)
        dm_ref[...] = dm16
        dmg = _dot_nt(dm16, wo_ref[...])
        ga = gate_ref[:, :D].astype(F32)
        gb = gate_ref[:, D:].astype(F32)
        dpa = (dmg * ga).astype(BF)
        dpb = (dmg * gb).astype(BF)
        dpa_ref[...] = dpa
        dpb_ref[...] = dpb
        dza = (dmg * pa_ref[...].astype(F32)) * (ga * (1.0 - ga))
        dzb = (dmg * pb_ref[...].astype(F32)) * (gb * (1.0 - gb))
        dzg_ref[:, :D] = dza.astype(BF)
        dzg_ref[:, D:] = dzb.astype(BF)

        @pl.when(pl.program_id(0) == 0)
        def _():
            db_ref[...] = jnp.zeros_like(db_ref)

        db_ref[:, :D] += jnp.sum(dza, axis=0, keepdims=True)
        db_ref[:, D:] += jnp.sum(dzb, axis=0, keepdims=True)
        dya_ref[...] = _dot_nt(dpa, wa_ref[...]).astype(BF)
        dyb_ref[...] = _dot_nt(dpb, wb_ref[...]).astype(BF)

    return pl.pallas_call(
        body, name="bwd_merge", grid=(t // TM,),
        in_specs=[_rows(TM, D), _rows(TM, D), _whole((1, D)), _whole((D, D)), _rows(TM, ZG_W),
                  _rows(TM, D), _rows(TM, D), _whole((D, D)), _whole((D, D))],
        out_specs=[_rows(TM, D), _rows(TM, D), _rows(TM, D), _rows(TM, ZG_W), _rows(TM, D), _rows(TM, D),
                   _whole((1, D)), _whole((1, ZG_W))],
        out_shape=[jax.ShapeDtypeStruct((t, D), BF), jax.ShapeDtypeStruct((t, D), BF), jax.ShapeDtypeStruct((t, D), BF),
                   jax.ShapeDtypeStruct((t, ZG_W), BF), jax.ShapeDtypeStruct((t, D), BF), jax.ShapeDtypeStruct((t, D), BF),
                   jax.ShapeDtypeStruct((1, D), F32), jax.ShapeDtypeStruct((1, ZG_W), F32)],
        compiler_params=_params(("arbitrary",)),
    )(m, dx1, g2, wo, gate, pa, pb, wa, wb)


def _flash_bwd(q, do, o, lse, k, v, group, scale, name, exchange=None):
    t = q.shape[0]
    tq = min(TQ_BWD_STEP, t)
    nq = t // tq
    nkv = NH // group

    def body(*refs):
        (q_ref, do_ref, o_ref, lse_ref, k_ref, v_ref), (dq_ref, dkt_ref, dvt_ref), ex_refs = _hosted(exchange, refs, 6, 3)
        first = (pl.program_id(1) == 0) & (pl.program_id(2) == 0)
        if exchange:
            @pl.when(first & (pl.program_id(0) == 0))
            def _():
                exchange.start(*ex_refs)

        @pl.when(first)
        def _():
            dkt_ref[...] = jnp.zeros_like(dkt_ref)
            dvt_ref[...] = jnp.zeros_like(dvt_ref)

        dvt = dkt = None
        for r0 in range(0, tq, TQ):
            rows = slice(r0, r0 + TQ)
            qv = q_ref[rows, :]
            dov = do_ref[rows, :]
            dsum = jnp.sum(dov.astype(F32) * o_ref[rows, :].astype(F32), axis=-1, keepdims=True)
            p = jnp.exp2(_dot_nt(qv, k_ref[...]) - lse_ref[rows, :1])
            ds = (p * (_dot_nt(dov, v_ref[...]) - dsum)).astype(BF)
            dq_ref[rows, :] = _dot(ds, k_ref[...]) * scale
            dvt_tile = _dot(dov.astype(F32).T.astype(BF), p.astype(BF))
            dkt_tile = _dot(qv.astype(F32).T.astype(BF), ds)
            dvt = dvt_tile if dvt is None else dvt + dvt_tile
            dkt = dkt_tile if dkt is None else dkt + dkt_tile
        dvt_ref[...] += dvt
        dkt_ref[...] += dkt * LN2
        if exchange:
            @pl.when((pl.program_id(0) == nkv - 1) & (pl.program_id(1) == group - 1) & (pl.program_id(2) == nq - 1))
            def _():
                exchange.finish(*ex_refs)

    qspec = pl.BlockSpec((tq, HW), lambda hk, g, i: (i, hk * group + g))
    kspec = pl.BlockSpec((t, HW), lambda hk, g, i: (0, hk))
    tspec = pl.BlockSpec((HW, t), lambda hk, g, i: (hk, 0))
    out = pl.pallas_call(
        body, name=name, grid=(nkv, group, nq),
        in_specs=[qspec, qspec, qspec, qspec, kspec, kspec] + (exchange.specs if exchange else []),
        out_specs=[qspec, tspec, tspec] + (exchange.specs if exchange else []),
        out_shape=[jax.ShapeDtypeStruct((t, NH * HW), F32), jax.ShapeDtypeStruct((nkv * HW, t), F32),
                   jax.ShapeDtypeStruct((nkv * HW, t), F32)] + (exchange.out_shape if exchange else []),
        scratch_shapes=exchange.scratch if exchange else [],
        compiler_params=_params(("arbitrary" if exchange else "parallel", "arbitrary", "arbitrary")),
    )(q, do, o, lse, k, v, *(exchange.srcs if exchange else []))
    return out[0], out[1], out[2], out[3:]


def _bwd_mid(zm, dqa, dkta, dvta, dqb, dktb, dvtb, gq, gk, gqa, gkva, wq, wk, wv, ca, sa, cb, sb):
    t = zm.shape[0]
    hw8 = NH * HW

    def body(zm_ref, dqa_ref, dkta_ref, dvta_ref, dqb_ref, dktb_ref, dvtb_ref, gq_ref, gk_ref, gqa_ref, gkva_ref,
             wq_ref, wk_ref, wv_ref, ca_ref, sa_ref, cb_ref, sb_ref,
             dzm_ref, dqbp_ref, dkb16_ref, dvb16_ref, dgq_ref, dgk_ref, dgqa_ref, dgkva_ref):
        ca_, sa_, cb_, sb_ = ca_ref[...], sa_ref[...], cb_ref[...], sb_ref[...]
        dgq = jnp.zeros((1, HW), F32)
        for h in range(NH):
            sl = slice(h * HW, (h + 1) * HW)
            dqn = _rope_bwd(dqa_ref[:, sl], ca_, sa_)
            dx, dg = _norm_bwd(zm_ref[:, O_QA + h * HW:O_QA + (h + 1) * HW], gq_ref[...], dqn, HEAD)
            dzm_ref[:, O_QA + h * HW:O_QA + (h + 1) * HW] = dx.astype(BF)
            dgq = dgq + dg
        _accumulate(dgq_ref, dgq)
        dgk = jnp.zeros((1, HW), F32)
        for j in range(GQA_KV):
            dk = dkta_ref[j * HW:(j + 1) * HW, :].T
            dv = dvta_ref[j * HW:(j + 1) * HW, :].T
            dkn = _rope_bwd(dk, ca_, sa_)
            dx, dg = _norm_bwd(zm_ref[:, O_KA + j * HW:O_KA + (j + 1) * HW], gk_ref[...], dkn, HEAD)
            dzm_ref[:, O_KA + j * HW:O_KA + (j + 1) * HW] = dx.astype(BF)
            dzm_ref[:, O_VA + j * HW:O_VA + (j + 1) * HW] = dv.astype(BF)
            dgk = dgk + dg
        _accumulate(dgk_ref, dgk)
        for h in range(NH):
            sl = slice(h * HW, (h + 1) * HW)
            dqbp_ref[:, sl] = _rope_bwd(dqb_ref[:, sl], cb_, sb_).astype(BF)
        dcqn = _dot_nt(dqbp_ref[...], wq_ref[...])
        dx, dg = _norm_bwd(zm_ref[:, O_CQ:O_CQ + Q_RANK], gqa_ref[...], dcqn)
        dzm_ref[:, O_CQ:O_CQ + Q_RANK] = dx.astype(BF)
        _accumulate(dgqa_ref, dg)
        dkr = jnp.zeros((TM, HW), F32)
        for h in range(NH):
            sl = slice(h * HW, (h + 1) * HW)
            dkh = dktb_ref[sl, :].T
            dkr = dkr + dkh
            dkb16_ref[:, sl] = dkh.astype(BF)
            dvb16_ref[:, sl] = dvtb_ref[sl, :].T.astype(BF)
        dkb16 = dkb16_ref[...]
        dvb16 = dvb16_ref[...]
        lane = lax.broadcasted_iota(jnp.int32, (TM, HW), 1)
        in_rope = (lane % (HW // 2)) < MLA_ROPE // 2
        dzm_ref[:, O_KR:O_KR + HW] = jnp.where(in_rope, _rope_bwd(dkr, cb_, sb_), 0.0).astype(BF)
        dckvn = _dot_nt(dkb16, wk_ref[...]) + _dot_nt(dvb16, wv_ref[...])
        dx, dg = _norm_bwd(zm_ref[:, O_CKV:O_CKV + KV_RANK], gkva_ref[...], dckvn)
        dzm_ref[:, O_CKV:O_CKV + KV_RANK] = dx.astype(BF)
        _accumulate(dgkva_ref, dg)

    return pl.pallas_call(
        body, name="bwd_mid", grid=(t // TM,),
        in_specs=[_rows(TM, ZM_W), _rows(TM, hw8), _cols(GQA_KV * HW, TM), _cols(GQA_KV * HW, TM), _rows(TM, hw8),
                  _cols(hw8, TM), _cols(hw8, TM), _whole((1, HW)), _whole((1, HW)), _whole((1, Q_RANK)), _whole((1, KV_RANK)),
                  _whole((Q_RANK, hw8)), _whole((KV_RANK, hw8)), _whole((KV_RANK, hw8)),
                  _rows(TM, HW), _rows(TM, HW), _rows(TM, HW), _rows(TM, HW)],
        out_specs=[_rows(TM, ZM_W), _rows(TM, hw8), _rows(TM, hw8), _rows(TM, hw8),
                   _whole((1, HW)), _whole((1, HW)), _whole((1, Q_RANK)), _whole((1, KV_RANK))],
        out_shape=[jax.ShapeDtypeStruct((t, ZM_W), BF), jax.ShapeDtypeStruct((t, hw8), BF), jax.ShapeDtypeStruct((t, hw8), BF),
                   jax.ShapeDtypeStruct((t, hw8), BF), jax.ShapeDtypeStruct((1, HW), F32), jax.ShapeDtypeStruct((1, HW), F32),
                   jax.ShapeDtypeStruct((1, Q_RANK), F32), jax.ShapeDtypeStruct((1, KV_RANK), F32)],
        compiler_params=_params(("arbitrary",)),
    )(zm, dqa, dkta, dvta, dqb, dktb, dvtb, gq, gk, gqa, gkva, wq, wk, wv, ca, sa, cb, sb)


def _bwd_inproj(dzm, dzg, w1, w2, x, g1, dx1):
    t = x.shape[0]

    def body(dzm_ref, dzg_ref, w1_ref, w2_ref, x_ref, g_ref, dx1_ref, dx_ref, dg_ref):
        du = _dot_nt(dzm_ref[...], w1_ref[...]) + _dot_nt(dzg_ref[...], w2_ref[...])
        dx, dg = _norm_bwd(x_ref[...], g_ref[...], du)
        _accumulate(dg_ref, dg)
        dx_ref[...] = dx1_ref[...] + dx

    return pl.pallas_call(
        body, name="bwd_inproj", grid=(t // TM,),
        in_specs=[_rows(TM, ZM_W), _rows(TM, ZG_W), _whole((D, ZM_W)), _whole((D, ZG_W)), _rows(TM, D), _whole((1, D)),
                  _rows(TM, D)],
        out_specs=[_rows(TM, D), _whole((1, D))],
        out_shape=[jax.ShapeDtypeStruct((t, D), F32), jax.ShapeDtypeStruct((1, D), F32)],
        compiler_params=_params(("arbitrary",)),
    )(dzm, dzg, w1, w2, x, g1, dx1)


def _matmul_tn(a, b, tn, name):
    t, kdim = a.shape
    n = b.shape[1]
    tm = min(TN_ROWS_NARROW if kdim <= D else TN_ROWS_WIDE, t)
    nsteps = t // tm

    def body(a_ref, b_ref, o_ref, acc_ref):
        i = pl.program_id(1)

        @pl.when(i == 0)
        def _():
            acc_ref[...] = jnp.zeros_like(acc_ref)

        acc_ref[...] += _dot_tn(a_ref[...].astype(BF), b_ref[...])

        @pl.when(i == nsteps - 1)
        def _():
            o_ref[...] = acc_ref[...].astype(BF)

    return pl.pallas_call(
        body, name=name, grid=(n // tn, nsteps),
        in_specs=[pl.BlockSpec((tm, kdim), lambda j, i: (i, 0)), pl.BlockSpec((tm, tn), lambda j, i: (i, j))],
        out_specs=pl.BlockSpec((kdim, tn), lambda j, i: (0, j)),
        out_shape=jax.ShapeDtypeStruct((kdim, n), BF),
        scratch_shapes=[pltpu.VMEM((kdim, tn), F32)],
        compiler_params=_params(("parallel", "arbitrary")),
    )(a, b)


LANES_GQA = [(0, 16), (32, 48), 32, (16, 32), (48, 64), 32]
LANES_MLA_Q = [(64, 72), (80, 88), (0, 48), (72, 80), (88, 96), (48, 64), 32]
LANES_MLA_KNOPE = [16, (0, 48), 16, (48, 64), 32]
LANES_MLA_ROPE = [(0, 8), (16, 24), 48, (8, 16), (24, 32), 48]
LANES_PLAIN = [(0, 64), 64]


def _to_lanes(w, layout, fill=0.0):
    width = max(seg[1] for seg in layout if not isinstance(seg, int))
    lead = w.shape[:-1]
    w = w.reshape(lead + (-1, width))
    parts = [jnp.full(w.shape[:-1] + (seg,), fill, w.dtype) if isinstance(seg, int) else w[..., seg[0]:seg[1]]
             for seg in layout]
    return jnp.concatenate(parts, axis=-1).reshape(lead + (-1,))


def _from_lanes(p, layout):
    lead = p.shape[:-1]
    p = p.reshape(lead + (-1, HW))
    found, lane = [], 0
    for seg in layout:
        if isinstance(seg, int):
            lane += seg
        else:
            found.append((seg[0], p[..., lane:lane + seg[1] - seg[0]]))
            lane += seg[1] - seg[0]
    return jnp.concatenate([piece for _, piece in sorted(found, key=lambda f: f[0])], axis=-1).reshape(lead + (-1,))


def _rope_tables(t):
    pos = jnp.arange(t, dtype=jnp.int32)
    row = (pos // GRID_W).astype(F32)
    col = (pos % GRID_W).astype(F32)

    def table(rot_dim):
        half = rot_dim // 2
        inv = ROPE_THETA ** (-jnp.arange(0, half, 2, dtype=F32) / half)
        ar = row[:, None] * inv[None, :]
        ac = col[:, None] * inv[None, :]
        ang = jnp.concatenate([ar, ar, ac, ac], axis=-1)
        return jnp.cos(ang), jnp.sin(ang)

    c64, s64 = table(HEAD)
    c32, s32 = table(MLA_ROPE)
    return (_to_lanes(c64, LANES_GQA, 1.0), _to_lanes(s64, LANES_GQA), _to_lanes(c32, LANES_MLA_ROPE, 1.0),
            _to_lanes(s32, LANES_MLA_ROPE))


def _pad_heads_rows(w, heads, width):
    n = w.shape[1]
    w = w.reshape(heads, width, n)
    return jnp.pad(w, ((0, 0), (0, HW - width), (0, 0))).reshape(heads * HW, n)


def _unpad_heads_rows(w, heads, width):
    n = w.shape[1]
    return w.reshape(heads, HW, n)[:, :width, :].reshape(heads * width, n)


def _pad_first(w):
    w_in = w["w_in"]
    qa = _to_lanes(w_in[:, 0:512], LANES_GQA)
    ka = _to_lanes(w_in[:, 512:640], LANES_GQA)
    va = _to_lanes(w_in[:, 640:768], LANES_PLAIN)
    cq = w_in[:, 768:1152]
    ckv = w_in[:, 1152:1408]
    kr = _to_lanes(w_in[:, 1408:1440], LANES_MLA_ROPE)
    kvu = w["w_kv_up"].reshape(KV_RANK, NH, 2 * HEAD)
    return dict(
        w1=jnp.concatenate([qa, ka, va, cq, ckv, kr], axis=1),
        w2=w_in[:, 1440:],
        wq=_to_lanes(w["w_q_up"], LANES_MLA_Q),
        wk=_to_lanes(kvu[:, :, :HEAD].reshape(KV_RANK, NH * HEAD), LANES_MLA_KNOPE),
        wv=_to_lanes(kvu[:, :, HEAD:].reshape(KV_RANK, NH * HEAD), LANES_PLAIN),
    )


def _pad_rest(w):
    return dict(
        wa=_pad_heads_rows(w["w_branch_a"], NH, HEAD),
        wb=_pad_heads_rows(w["w_branch_b"], NH, HEAD),
        wo=w["w_o"], wup=w["w_ffn_up"], wdn=w["w_ffn_down"],
    )


def _unpad_first(g):
    d1 = g["w1"]
    w_in = jnp.concatenate([
        _from_lanes(d1[:, O_QA:O_KA], LANES_GQA), _from_lanes(d1[:, O_KA:O_VA], LANES_GQA),
        _from_lanes(d1[:, O_VA:O_CQ], LANES_PLAIN), d1[:, O_CQ:O_CKV], d1[:, O_CKV:O_KR],
        _from_lanes(d1[:, O_KR:O_KR + HW], LANES_MLA_ROPE), g["w2"]], axis=1)
    dk = _from_lanes(g["wk"], LANES_MLA_KNOPE).reshape(KV_RANK, NH, HEAD)
    dv = _from_lanes(g["wv"], LANES_PLAIN).reshape(KV_RANK, NH, HEAD)
    return dict(
        w_in=w_in,
        w_q_up=_from_lanes(g["wq"], LANES_MLA_Q),
        w_kv_up=jnp.concatenate([dk, dv], axis=2).reshape(KV_RANK, NH * 2 * HEAD),
    )


def _unpad_rest(g):
    return dict(
        w_branch_a=_unpad_heads_rows(g["wa"], NH, HEAD),
        w_branch_b=_unpad_heads_rows(g["wb"], NH, HEAD),
        w_o=g["wo"], w_ffn_up=g["wup"], w_ffn_down=g["wdn"],
    )


SHARD_AXIS = dict(w_in=1, w_q_up=1, w_kv_up=1, w_branch_a=1, w_branch_b=1, w_o=0, w_ffn_up=1, w_ffn_down=0)


def _join(parts, name):
    if SHARD_AXIS[name] == 0:
        return parts.reshape(-1, parts.shape[-1])
    return jnp.transpose(parts, (1, 0, 2)).reshape(parts.shape[1], -1)


def _split(full, name):
    if SHARD_AXIS[name] == 0:
        return full.reshape(4, -1, full.shape[-1])
    return jnp.transpose(full.reshape(full.shape[0], 4, -1), (1, 0, 2))


def _local_step(x, target, shards, smalls):
    t = x.shape[0]
    nl = len(smalls)
    ca, sa, cb, sb = _rope_tables(t)

    def joined(names, got):
        return {n: _join(g, n) for n, g in zip(names, got)}

    w = _pad_first(joined(FIRST, _chip_exchange([shards[n][0] for n in FIRST], False, "gather_weights")))
    layers, saved = [], []
    for li, s in enumerate(smalls):
        gq, gk = _to_lanes(s["q_norm_g"][None], LANES_GQA), _to_lanes(s["k_norm_g"][None], LANES_GQA)
        gqa, gkva = s["q_a_norm_g"].reshape(1, -1), s["kv_a_norm_g"].reshape(1, -1)
        g1, g2, g3, g4 = (s[n].reshape(1, D) for n in ("pre_mix_g", "post_mix_g", "pre_ffn_g", "post_ffn_g"))
        b = s["b_gate"].reshape(1, ZG_W)
        u, zm, gate = _fwd_inproj(x, g1, w["w1"], w["w2"], b)
        qa, ka, va, qb, kb, vb, cqn, ckvn = _fwd_mid(zm, gq, gk, gqa, gkva, w["wq"], w["wk"], w["wv"], ca, sa, cb, sb)
        last = li + 1 == nl
        gather = _ChipExchange([shards[n][li] for n in REST] + ([] if last else [shards[n][li + 1] for n in FIRST]), False)
        ya, lse_a, got = _flash_fwd(qa, ka, va, GQA_G, "flash_fwd_gqa", gather)
        yb, lse_b, _ = _flash_fwd(qb, kb, vb, 1, "flash_fwd_mla")
        w = {**w, **_pad_rest(joined(REST, got[:len(REST)]))}
        pa, pb, mg, m, x1 = _fwd_merge(ya, yb, gate, w["wa"], w["wb"], w["wo"], x, g2)
        hn, a, r2 = _fwd_ffn_up(x1, g3, w["wup"])
        f, x2 = _fwd_ffn_down(a, w["wdn"], x1, g4)
        saved.append(dict(x=x, u=u, zm=zm, gate=gate, qa=qa, ka=ka, va=va, qb=qb, kb=kb, vb=vb, cqn=cqn, ckvn=ckvn,
                          ya=ya, lse_a=lse_a, yb=yb, lse_b=lse_b, pa=pa, pb=pb, mg=mg,
                          m=m, x1=x1, hn=hn, a=a, r2=r2, f=f, gq=gq, gk=gk, gqa=gqa, gkva=gkva, g1=g1, g2=g2, g3=g3,
                          g4=g4))
        layers.append(w)
        x = x2
        if not last:
            w = _pad_first(joined(FIRST, got[len(REST):]))

    dx, loss8 = _loss_head(x, target)
    loss = loss8[0, 0]

    received, small_grads, send_first = [dict() for _ in range(nl)], [None] * nl, None
    for li in range(nl - 1, -1, -1):
        w, r = layers[li], saved[li]
        df, dh, dg4 = _bwd_ffn_down(r["f"], dx, r["g4"], w["wdn"], r["r2"])
        dx1, dg3 = _bwd_ffn_up(dh, w["wup"], r["x1"], r["g3"], dx)
        dm, dpa, dpb, dzg, dya, dyb, dg2, db = _bwd_merge(r["m"], dx1, r["g2"], w["wo"], r["gate"], r["pa"], r["pb"],
                                                         w["wa"], w["wb"])
        nat = _unpad_rest(dict(
            wa=_matmul_tn(r["ya"], dpa, 512, "dw_branch_a"), wb=_matmul_tn(r["yb"], dpb, 512, "dw_branch_b"),
            wo=_matmul_tn(r["mg"], dm, 512, "dw_o"), wup=_matmul_tn(r["hn"], dh, 512, "dw_ffn_up"),
            wdn=_matmul_tn(r["a"], df, 512, "dw_ffn_down")))
        scatter = _ChipExchange([_split(nat[n], n) for n in REST] + (send_first or []), True)
        dqa, dkta, dvta, got = _flash_bwd(r["qa"], dya, r["ya"], r["lse_a"], r["ka"], r["va"], GQA_G, SCALE_GQA,
                                          "flash_bwd_gqa", scatter)
        received[li].update(zip(REST, got[:len(REST)]))
        if send_first:
            received[li + 1].update(zip(FIRST, got[len(REST):]))
        dqb, dktb, dvtb, _ = _flash_bwd(r["qb"], dyb, r["yb"], r["lse_b"], r["kb"], r["vb"], 1, SCALE_MLA, "flash_bwd_mla")
        dzm, dqbp, dkb16, dvb16, dgq, dgk, dgqa, dgkva = _bwd_mid(
            r["zm"], dqa, dkta, dvta, dqb, dktb, dvtb, r["gq"], r["gk"], r["gqa"], r["gkva"], w["wq"], w["wk"], w["wv"],
            ca, sa, cb, sb)
        dx, dg1 = _bwd_inproj(dzm, dzg, w["w1"], w["w2"], r["x"], r["g1"], dx1)
        nat = _unpad_first(dict(
            w1=_matmul_tn(r["u"], dzm, 768, "dw_in_main"), w2=_matmul_tn(r["u"], dzg, 512, "dw_in_gate"),
            wq=_matmul_tn(r["cqn"], dqbp, 512, "dw_q_up"), wk=_matmul_tn(r["ckvn"], dkb16, 512, "dw_k_up"),
            wv=_matmul_tn(r["ckvn"], dvb16, 512, "dw_v_up")))
        send_first = [_split(nat[n], n) for n in FIRST]
        small_grads[li] = dict(b_gate=db[0], q_norm_g=_from_lanes(dgq, LANES_GQA)[0], k_norm_g=_from_lanes(dgk, LANES_GQA)[0], q_a_norm_g=dgqa[0],
                               kv_a_norm_g=dgkva[0], pre_mix_g=dg1[0], post_mix_g=dg2[0], pre_ffn_g=dg3[0],
                               post_ffn_g=dg4[0])
    received[0].update(zip(FIRST, _chip_exchange(send_first, True, "scatter_grads")))
    return loss, dx, received, small_grads


def _chip_exchange(srcs, per_dest, name):
    ex = _ChipExchange(srcs, per_dest)

    def body(*refs):
        _, _, ex_refs = _hosted(ex, refs, 0, 0)
        ex.start(*ex_refs)
        ex.finish(*ex_refs)

    return pl.pallas_call(
        body, name=name, in_specs=ex.specs, out_specs=ex.specs, out_shape=ex.out_shape, scratch_shapes=ex.scratch,
    )(*ex.srcs)


def _sibling_exchange(srcs, name):
    n = len(srcs)

    def body(*refs):
        src_refs, out_refs = refs[:n], refs[n:2 * n]
        send_sems, recv_sems = refs[2 * n:]
        x, y, c = lax.axis_index("x"), lax.axis_index("y"), lax.axis_index("c")
        cps = [pltpu.make_async_remote_copy(src_ref=src_refs[a], dst_ref=out_refs[a], send_sem=send_sems.at[a],
                                            recv_sem=recv_sems.at[a], device_id=(x, y, 1 - c), device_id_type=MESH)
               for a in range(n)]
        for cp in cps:
            cp.start()
        for cp in cps:
            cp.wait()

    return pl.pallas_call(
        body, name=name,
        in_specs=[pl.BlockSpec(memory_space=pl.ANY)] * n,
        out_specs=[pl.BlockSpec(memory_space=pl.ANY)] * n,
        out_shape=[jax.ShapeDtypeStruct(s.shape, s.dtype) for s in srcs],
        scratch_shapes=[pltpu.SemaphoreType.DMA((n,)), pltpu.SemaphoreType.DMA((n,))],
    )(*srcs)


def _allgather_small(v):
    m_per, n = v.shape

    def body(x_ref, out_ref, send_sems, recv_sems, local_sem):
        x, y, c = lax.axis_index("x"), lax.axis_index("y"), lax.axis_index("c")
        me, sibling = (x, y, c), (x, y, 1 - c)
        chips = [(1 - x, y), (x, 1 - y), (1 - x, 1 - y)]

        def rows(px, py, pc):
            return out_ref.at[pl.ds((4 * px + 2 * py + pc) * m_per, m_per), :]

        def copy(k, block, to, src=None):
            return pltpu.make_async_remote_copy(
                src_ref=rows(*block) if src is None else src, dst_ref=rows(*block),
                send_sem=send_sems.at[k], recv_sem=recv_sems.at[k], device_id=to, device_id_type=MESH)

        mine = pltpu.make_async_copy(x_ref, rows(*me), local_sem)
        mine.start()
        first = [copy(0, me, sibling, src=x_ref)]
        first += [copy(1 + j, me, (*chip, c), src=x_ref) for j, chip in enumerate(chips)]
        for cp in first:
            cp.start()
        passed = [copy(4 + j, (*chip, c), sibling) for j, chip in enumerate(chips)]
        for j, chip in enumerate(chips):
            copy(1 + j, (*chip, c), me).wait_recv()
            passed[j].start()
        copy(0, sibling, me).wait_recv()
        for j, chip in enumerate(chips):
            copy(4 + j, (*chip, 1 - c), me).wait_recv()
        for cp in first + passed:
            cp.wait_send()
        mine.wait()

    return pl.pallas_call(
        body, name="allgather_small",
        out_shape=jax.ShapeDtypeStruct((8 * m_per, n), v.dtype),
        in_specs=[pl.BlockSpec(memory_space=pltpu.VMEM)],
        out_specs=pl.BlockSpec(memory_space=pltpu.VMEM),
        scratch_shapes=[pltpu.SemaphoreType.DMA((7,)), pltpu.SemaphoreType.DMA((7,)), pltpu.SemaphoreType.DMA],
    )(v)


def _pick_rows(r):
    return next(t for t in (512, 256, 128, 64, 32, 16, 8, r) if r % t == 0)


def _sum_slabs(a, name):
    s, r, c = a.shape
    tm = _pick_rows(r)

    def body(a_ref, o_ref):
        acc = a_ref[0].astype(F32)
        for k in range(1, s):
            acc = acc + a_ref[k].astype(F32)
        o_ref[...] = acc

    return pl.pallas_call(
        body, name=name, grid=(r // tm,),
        in_specs=[pl.BlockSpec((s, tm, c), lambda i: (0, i, 0))],
        out_specs=_rows(tm, c),
        out_shape=jax.ShapeDtypeStruct((r, c), F32),
        compiler_params=_params(("parallel",)),
    )(a)


def _adamw(w, ga, gb, m, v, name):
    r, c = w.shape
    tm = min(256, _pick_rows(r))

    def body(w_ref, ga_ref, gb_ref, m_ref, v_ref, g_ref, d_ref, nm_ref, nv_ref):
        gv = ga_ref[...] + gb_ref[...]
        g_ref[...] = gv
        nm = ADAM_B1 * m_ref[...] + (1.0 - ADAM_B1) * gv
        nv = ADAM_B2 * v_ref[...] + (1.0 - ADAM_B2) * (gv * gv)
        m_hat = nm / (1.0 - ADAM_B1 ** ADAM_STEP)
        v_hat = nv / (1.0 - ADAM_B2 ** ADAM_STEP)
        d_ref[...] = -ADAM_LR * (m_hat / (jnp.sqrt(v_hat) + ADAM_EPS) + ADAM_WD * w_ref[...])
        nm_ref[...] = nm
        nv_ref[...] = nv

    spec = _rows(tm, c)
    return pl.pallas_call(
        body, name=name, grid=(r // tm,),
        in_specs=[spec] * 5, out_specs=[spec] * 4,
        out_shape=[jax.ShapeDtypeStruct((r, c), F32)] * 4,
        compiler_params=_params(("parallel",)),
    )(w, ga, gb, m, v)


def _adamw_small(w, gparts, m, v):
    mrows, n = w.shape

    def body(w_ref, g_ref, m_ref, v_ref, go_ref, d_ref, nm_ref, nv_ref):
        gv = g_ref[0]
        for k in range(1, 8):
            gv = gv + g_ref[k]
        go_ref[...] = gv
        nm = ADAM_B1 * m_ref[...] + (1.0 - ADAM_B1) * gv
        nv = ADAM_B2 * v_ref[...] + (1.0 - ADAM_B2) * (gv * gv)
        m_hat = nm / (1.0 - ADAM_B1 ** ADAM_STEP)
        v_hat = nv / (1.0 - ADAM_B2 ** ADAM_STEP)
        d_ref[...] = -ADAM_LR * (m_hat / (jnp.sqrt(v_hat) + ADAM_EPS) + ADAM_WD * w_ref[...])
        nm_ref[...] = nm
        nv_ref[...] = nv

    return pl.pallas_call(
        body, name="adamw_small",
        out_shape=[jax.ShapeDtypeStruct((mrows, n), F32)] * 4,
    )(w, gparts, m, v)


def _flat(parts):
    nl = parts[0].shape[0]
    return jnp.concatenate([p.reshape(nl, -1) for p in parts], axis=1)


def _unflat(flat, shapes):
    out, off = [], 0
    nl = flat.shape[0]
    for s in shapes:
        n = math.prod(s)
        out.append(flat[:, off:off + n].reshape((nl,) + tuple(s)))
        off += n
    return out


def kernel(x, w_in, b_gate, q_norm_g, k_norm_g, q_a_norm_g, kv_a_norm_g, w_q_up, w_kv_up, w_branch_a, w_branch_b, w_o, w_ffn_up, w_ffn_down, pre_mix_g, post_mix_g, pre_ffn_g, post_ffn_g, loss_target, m_w_in, m_b_gate, m_q_norm_g, m_k_norm_g, m_q_a_norm_g, m_kv_a_norm_g, m_w_q_up, m_w_kv_up, m_w_branch_a, m_w_branch_b, m_w_o, m_w_ffn_up, m_w_ffn_down, m_pre_mix_g, m_post_mix_g, m_pre_ffn_g, m_post_ffn_g, v_w_in, v_b_gate, v_q_norm_g, v_k_norm_g, v_q_a_norm_g, v_kv_a_norm_g, v_w_q_up, v_w_kv_up, v_w_branch_a, v_w_branch_b, v_w_o, v_w_ffn_up, v_w_ffn_down, v_pre_mix_g, v_post_mix_g, v_pre_ffn_g, v_post_ffn_g):
    wts = dict(w_in=w_in, b_gate=b_gate, q_norm_g=q_norm_g, k_norm_g=k_norm_g, q_a_norm_g=q_a_norm_g,
               kv_a_norm_g=kv_a_norm_g, w_q_up=w_q_up, w_kv_up=w_kv_up, w_branch_a=w_branch_a, w_branch_b=w_branch_b,
               w_o=w_o, w_ffn_up=w_ffn_up, w_ffn_down=w_ffn_down, pre_mix_g=pre_mix_g, post_mix_g=post_mix_g,
               pre_ffn_g=pre_ffn_g, post_ffn_g=post_ffn_g)
    mom = dict(w_in=m_w_in, b_gate=m_b_gate, q_norm_g=m_q_norm_g, k_norm_g=m_k_norm_g, q_a_norm_g=m_q_a_norm_g,
               kv_a_norm_g=m_kv_a_norm_g, w_q_up=m_w_q_up, w_kv_up=m_w_kv_up, w_branch_a=m_w_branch_a,
               w_branch_b=m_w_branch_b, w_o=m_w_o, w_ffn_up=m_w_ffn_up, w_ffn_down=m_w_ffn_down, pre_mix_g=m_pre_mix_g,
               post_mix_g=m_post_mix_g, pre_ffn_g=m_pre_ffn_g, post_ffn_g=m_post_ffn_g)
    var = dict(w_in=v_w_in, b_gate=v_b_gate, q_norm_g=v_q_norm_g, k_norm_g=v_k_norm_g, q_a_norm_g=v_q_a_norm_g,
               kv_a_norm_g=v_kv_a_norm_g, w_q_up=v_w_q_up, w_kv_up=v_w_kv_up, w_branch_a=v_w_branch_a,
               w_branch_b=v_w_branch_b, w_o=v_w_o, w_ffn_up=v_w_ffn_up, w_ffn_down=v_w_ffn_down, pre_mix_g=v_pre_mix_g,
               post_mix_g=v_post_mix_g, pre_ffn_g=v_pre_ffn_g, post_ffn_g=v_post_ffn_g)
    nl = w_in.shape[0]

    shards = {n: wts[n].astype(BF) for n in BIG}
    smalls = [{n: wts[n][li] for n in SMALL} for li in range(nl)]
    loss_local, dx, received, small_grads = _local_step(x[0], loss_target[0], shards, smalls)
    loss = lax.psum(loss_local, ("x", "y", "c"))
    part = []
    for i, n in enumerate(BIG):
        got = jnp.stack([received[li][n] for li in range(nl)], axis=1)
        part.append(_sum_slabs(got.reshape(4, -1, got.shape[-1]), "sum_chips_" + n))
    other = _sibling_exchange(part, "swap_cores")

    small_shapes = [wts[n].shape[1:] for n in SMALL]
    g_loc = _flat([jnp.stack([small_grads[li][n] for li in range(nl)]) for n in SMALL]).reshape(-1, 128)
    g_all = _allgather_small(g_loc).reshape(8, -1, 128)
    pack = lambda d: _flat([d[n] for n in SMALL]).reshape(-1, 128)
    gs, ds, ms, vs = _adamw_small(pack(wts), g_all, pack(mom), pack(var))
    unpack = lambda a: dict(zip(SMALL, _unflat(a.reshape(nl, -1), small_shapes)))
    g_small, d_small, m_small, v_small = unpack(gs), unpack(ds), unpack(ms), unpack(vs)

    out_g, out_d, out_m, out_v = dict(g_small), dict(d_small), dict(m_small), dict(v_small)
    for i, n in enumerate(BIG):
        shp = wts[n].shape
        two = lambda a: a.reshape(-1, shp[-1])
        g, d, nm, nv = _adamw(two(wts[n]), part[i], other[i], two(mom[n]), two(var[n]), "adamw_" + n)
        out_g[n], out_d[n], out_m[n], out_v[n] = g.reshape(shp), d.reshape(shp), nm.reshape(shp), nv.reshape(shp)

    return (loss, dx[None], *[out_g[n] for n in ORDER], *[out_d[n] for n in ORDER], *[out_m[n] for n in ORDER],
            *[out_v[n] for n in ORDER])
```

```python
import functools
import math

import jax
import jax.numpy as jnp
from jax import lax
from jax.experimental import pallas as pl
from jax.experimental.pallas import tpu as pltpu

F32 = jnp.float32
BF = jnp.bfloat16
MESH = pl.DeviceIdType.MESH

EPS = 1e-6
D = 1024
NH = 8
HW = 128
GQA_KV = 2
GQA_G = 4
HEAD = 64
MLA_NOPE = 64
MLA_ROPE = 32
MLA_QK = 96
Q_RANK = 384
KV_RANK = 256
DFF = 4096
GRID_W = 64
ROPE_THETA = 10000.0

O_QA, O_KA, O_VA, O_CQ, O_CKV, O_KR, ZM_W = 0, 1024, 1280, 1536, 1920, 2176, 2304
ZG_W = 2048

ADAM_LR, ADAM_B1, ADAM_B2, ADAM_EPS, ADAM_WD, ADAM_STEP = 0.001, 0.9, 0.999, 1e-08, 0.01, 10

TM = 256
TQ = 256
TQ_FWD_STEP = 1024
TQ_BWD_STEP = 512
ONES_LANE = 64
LOG2E = 1.4426950408889634
LN2 = 0.6931471805599453
SCALE_GQA = 1.0 / math.sqrt(HEAD)
SCALE_MLA = 1.0 / math.sqrt(MLA_QK)
C2_GQA = SCALE_GQA * LOG2E
C2_MLA = SCALE_MLA * LOG2E
VMEM_LIMIT = 56 * 1024 * 1024
TN_ROWS_NARROW, TN_ROWS_WIDE = 2048, 1024

BIG = ("w_in", "w_q_up", "w_kv_up", "w_branch_a", "w_branch_b", "w_o", "w_ffn_up", "w_ffn_down")
FIRST = BIG[:3]
REST = BIG[3:]
SMALL = ("b_gate", "q_norm_g", "k_norm_g", "q_a_norm_g", "kv_a_norm_g", "pre_mix_g", "post_mix_g", "pre_ffn_g", "post_ffn_g")
ORDER = ("w_in", "b_gate", "q_norm_g", "k_norm_g", "q_a_norm_g", "kv_a_norm_g", "w_q_up", "w_kv_up", "w_branch_a",
         "w_branch_b", "w_o", "w_ffn_up", "w_ffn_down", "pre_mix_g", "post_mix_g", "pre_ffn_g", "post_ffn_g")


def _params(sem=None):
    return pltpu.CompilerParams(dimension_semantics=sem, vmem_limit_bytes=VMEM_LIMIT)


def _rows(tm, w):
    return pl.BlockSpec((tm, w), lambda i: (i, 0))


def _cols(h, tm):
    return pl.BlockSpec((h, tm), lambda i: (0, i))


def _whole(shape):
    return pl.BlockSpec(shape, lambda i: (0,) * len(shape))


def _dot(a, b):
    return jnp.dot(a, b, preferred_element_type=F32)


def _dot_nt(a, b):
    return lax.dot_general(a, b, (((1,), (1,)), ((), ())), preferred_element_type=F32)


def _dot_tn(a, b):
    return lax.dot_general(a, b, (((0,), (0,)), ((), ())), preferred_element_type=F32)


def _norm_fwd(xv, g, n=None):
    n = xv.shape[-1] if n is None else n
    r = lax.rsqrt(jnp.sum(xv * xv, axis=-1, keepdims=True) * (1.0 / n) + EPS)
    return (xv * r) * g


def _norm_bwd(xv, g, dy, n=None):
    n = xv.shape[-1] if n is None else n
    r = lax.rsqrt(jnp.sum(xv * xv, axis=-1, keepdims=True) * (1.0 / n) + EPS)
    xh = xv * r
    dxh = dy * g
    dg = jnp.sum(dy * xh, axis=0, keepdims=True)
    dx = r * (dxh - xh * (jnp.sum(dxh * xh, axis=-1, keepdims=True) * (1.0 / n)))
    return dx, dg


def _accumulate(ref, val):
    @pl.when(pl.program_id(0) == 0)
    def _():
        ref[...] = jnp.zeros_like(ref)

    ref[...] += val


def _rot(xv):
    lane = lax.broadcasted_iota(jnp.int32, xv.shape, 1)
    return jnp.where(lane < HW // 2, -1.0, 1.0) * pltpu.roll(xv, HW // 2, 1)


def _rope(xv, c, s):
    return xv * c + _rot(xv) * s


def _rope_bwd(dy, c, s):
    return dy * c - _rot(dy * s)


def _sigmoid(z):
    return 1.0 / (1.0 + jnp.exp(-z))


def _fwd_inproj(x, g1, w1, w2, b):
    t = x.shape[0]

    def body(x_ref, g_ref, w1_ref, w2_ref, b_ref, u_ref, zm_ref, gate_ref):
        u = _norm_fwd(x_ref[...], g_ref[...]).astype(BF)
        u_ref[...] = u
        for c in range(0, ZM_W, 768):
            zm_ref[:, c:c + 768] = _dot(u, w1_ref[:, c:c + 768])
        for c in range(0, ZG_W, 512):
            gate_ref[:, c:c + 512] = _sigmoid(_dot(u, w2_ref[:, c:c + 512]) + b_ref[:, c:c + 512]).astype(BF)

    return pl.pallas_call(
        body, name="fwd_inproj", grid=(t // TM,),
        in_specs=[_rows(TM, D), _whole((1, D)), _whole((D, ZM_W)), _whole((D, ZG_W)), _whole((1, ZG_W))],
        out_specs=[_rows(TM, D), _rows(TM, ZM_W), _rows(TM, ZG_W)],
        out_shape=[jax.ShapeDtypeStruct((t, D), BF), jax.ShapeDtypeStruct((t, ZM_W), F32),
                   jax.ShapeDtypeStruct((t, ZG_W), BF)],
        compiler_params=_params(("parallel",)),
    )(x, g1, w1, w2, b)


def _fwd_mid(zm, gq, gk, gqa, gkva, wq, wk, wv, ca, sa, cb, sb):
    t = zm.shape[0]

    def body(zm_ref, gq_ref, gk_ref, gqa_ref, gkva_ref, wq_ref, wk_ref, wv_ref, ca_ref, sa_ref, cb_ref, sb_ref,
             qa_ref, ka_ref, va_ref, qb_ref, kb_ref, vb_ref, cqn_ref, ckvn_ref):
        ca_, sa_, cb_, sb_ = ca_ref[...], sa_ref[...], cb_ref[...], sb_ref[...]
        for h in range(NH):
            xv = zm_ref[:, O_QA + h * HW:O_QA + (h + 1) * HW]
            qa_ref[:, h * HW:(h + 1) * HW] = (_rope(_norm_fwd(xv, gq_ref[...], HEAD), ca_, sa_) * C2_GQA).astype(BF)
        for h in range(GQA_KV):
            xv = zm_ref[:, O_KA + h * HW:O_KA + (h + 1) * HW]
            ka_ref[:, h * HW:(h + 1) * HW] = _rope(_norm_fwd(xv, gk_ref[...], HEAD), ca_, sa_).astype(BF)
        ones = lax.broadcasted_iota(jnp.int32, (TM, HW), 1) == ONES_LANE
        for h in range(GQA_KV):
            va_ref[:, h * HW:(h + 1) * HW] = jnp.where(ones, 1.0, zm_ref[:, O_VA + h * HW:O_VA + (h + 1) * HW]).astype(BF)
        cqn = _norm_fwd(zm_ref[:, O_CQ:O_CQ + Q_RANK], gqa_ref[...]).astype(BF)
        ckvn = _norm_fwd(zm_ref[:, O_CKV:O_CKV + KV_RANK], gkva_ref[...]).astype(BF)
        cqn_ref[...] = cqn
        ckvn_ref[...] = ckvn
        qb = _dot(cqn, wq_ref[...])
        kpre = _dot(ckvn, wk_ref[...])
        kr = _rope(zm_ref[:, O_KR:O_KR + HW], cb_, sb_)
        for h in range(NH):
            sl = slice(h * HW, (h + 1) * HW)
            qb_ref[:, sl] = (_rope(qb[:, sl], cb_, sb_) * C2_MLA).astype(BF)
            kb_ref[:, sl] = (kpre[:, sl] + kr).astype(BF)
        vb = _dot(ckvn, wv_ref[...])
        for h in range(NH):
            vb_ref[:, h * HW:(h + 1) * HW] = jnp.where(ones, 1.0, vb[:, h * HW:(h + 1) * HW]).astype(BF)

    hw8 = NH * HW
    return pl.pallas_call(
        body, name="fwd_mid", grid=(t // TM,),
        in_specs=[_rows(TM, ZM_W), _whole((1, HW)), _whole((1, HW)), _whole((1, Q_RANK)), _whole((1, KV_RANK)),
                  _whole((Q_RANK, hw8)), _whole((KV_RANK, hw8)), _whole((KV_RANK, hw8)),
                  _rows(TM, HW), _rows(TM, HW), _rows(TM, HW), _rows(TM, HW)],
        out_specs=[_rows(TM, hw8), _rows(TM, GQA_KV * HW), _rows(TM, GQA_KV * HW), _rows(TM, hw8), _rows(TM, hw8),
                   _rows(TM, hw8), _rows(TM, Q_RANK), _rows(TM, KV_RANK)],
        out_shape=[jax.ShapeDtypeStruct((t, hw8), BF), jax.ShapeDtypeStruct((t, GQA_KV * HW), BF),
                   jax.ShapeDtypeStruct((t, GQA_KV * HW), BF), jax.ShapeDtypeStruct((t, hw8), BF),
                   jax.ShapeDtypeStruct((t, hw8), BF), jax.ShapeDtypeStruct((t, hw8), BF),
                   jax.ShapeDtypeStruct((t, Q_RANK), BF), jax.ShapeDtypeStruct((t, KV_RANK), BF)],
        compiler_params=_params(("parallel",)),
    )(zm, gq, gk, gqa, gkva, wq, wk, wv, ca, sa, cb, sb)


class _ChipExchange:
    def __init__(self, srcs, per_dest):
        self.srcs, self.per_dest, self.n = list(srcs), per_dest, len(srcs)
        self.out_shape = [jax.ShapeDtypeStruct((4,) + tuple(s.shape[1:] if per_dest else s.shape), s.dtype) for s in srcs]
        self.specs = [pl.BlockSpec(memory_space=pl.ANY)] * self.n
        self.scratch = [pltpu.SemaphoreType.DMA((3 * self.n,)), pltpu.SemaphoreType.DMA((3 * self.n,)),
                        pltpu.SemaphoreType.DMA((self.n,))]

    def _copies(self, src_refs, out_refs, send_sems, recv_sems, local_sems):
        x, y, c = lax.axis_index("x"), lax.axis_index("y"), lax.axis_index("c")
        me = 2 * x + y
        chips = [(1 - x, y), (x, 1 - y), (1 - x, 1 - y)]

        def piece(a, k):
            return src_refs[a].at[k] if self.per_dest else src_refs[a]

        def remote(a, j, src_chip, dst_slab):
            px, py = chips[j]
            return pltpu.make_async_remote_copy(
                src_ref=piece(a, src_chip), dst_ref=out_refs[a].at[dst_slab], send_sem=send_sems.at[3 * a + j],
                recv_sem=recv_sems.at[3 * a + j], device_id=(px, py, c), device_id_type=MESH)

        local = [pltpu.make_async_copy(piece(a, me), out_refs[a].at[me], local_sems.at[a]) for a in range(self.n)]
        sends = [remote(a, j, 2 * chips[j][0] + chips[j][1], me) for a in range(self.n) for j in range(3)]
        recvs = [remote(a, j, me, 2 * chips[j][0] + chips[j][1]) for a in range(self.n) for j in range(3)]
        return local, sends, recvs

    def start(self, *refs):
        local, sends, _ = self._copies(*refs)
        for cp in local + sends:
            cp.start()

    def finish(self, *refs):
        local, sends, recvs = self._copies(*refs)
        for cp in recvs:
            cp.wait_recv()
        for cp in sends:
            cp.wait_send()
        for cp in local:
            cp.wait()


def _hosted(exchange, refs, n_in, n_out):
    n = exchange.n if exchange else 0
    ins, srcs = refs[:n_in], refs[n_in:n_in + n]
    outs = refs[n_in + n:n_in + n + n_out]
    rest = refs[n_in + n + n_out:]
    return ins, outs, (tuple(srcs), tuple(rest[:n])) + tuple(rest[n:])


def _flash_fwd(q, k, v, group, name, exchange=None):
    t = q.shape[0]
    tq = min(TQ_FWD_STEP, t)
    nq = t // tq

    def body(*refs):
        (q_ref, k_ref, v_ref), (o_ref, lse_ref), ex_refs = _hosted(exchange, refs, 3, 2)
        if exchange:
            @pl.when((pl.program_id(0) == 0) & (pl.program_id(1) == 0))
            def _():
                exchange.start(*ex_refs)

        for r0 in range(0, tq, TQ):
            rows = slice(r0, r0 + TQ)
            s = _dot_nt(q_ref[rows, :], k_ref[...])
            m = jnp.max(s, axis=-1, keepdims=True)
            acc = _dot(jnp.exp2(s - m).astype(BF), v_ref[...])
            l = acc[:, ONES_LANE:ONES_LANE + 1]
            o_ref[rows, :] = (acc / l).astype(BF)
            lse_ref[rows, :] = jnp.broadcast_to(m + jnp.log2(l), (TQ, HW))
        if exchange:
            @pl.when((pl.program_id(0) == NH - 1) & (pl.program_id(1) == nq - 1))
            def _():
                exchange.finish(*ex_refs)

    qspec = pl.BlockSpec((tq, HW), lambda h, i: (i, h))
    kspec = pl.BlockSpec((t, HW), lambda h, i: (0, h // group))
    out = pl.pallas_call(
        body, name=name, grid=(NH, nq),
        in_specs=[qspec, kspec, kspec] + (exchange.specs if exchange else []),
        out_specs=[qspec, qspec] + (exchange.specs if exchange else []),
        out_shape=[jax.ShapeDtypeStruct((t, NH * HW), BF), jax.ShapeDtypeStruct((t, NH * HW), F32)]
        + (exchange.out_shape if exchange else []),
        scratch_shapes=exchange.scratch if exchange else [],
        compiler_params=_params(("arbitrary", "arbitrary") if exchange else ("parallel", "parallel")),
    )(q, k, v, *(exchange.srcs if exchange else []))
    return out[0], out[1], out[2:]


def _fwd_merge(ya, yb, gate, wa, wb, wo, x, g2):
    t = x.shape[0]

    def body(ya_ref, yb_ref, gate_ref, wa_ref, wb_ref, wo_ref, x_ref, g_ref, pa_ref, pb_ref, mg_ref, m_ref, x1_ref):
        pa = _dot(ya_ref[...], wa_ref[...])
        pb = _dot(yb_ref[...], wb_ref[...])
        pa_ref[...] = pa.astype(BF)
        pb_ref[...] = pb.astype(BF)
        mg = (gate_ref[:, :D].astype(F32) * pa + gate_ref[:, D:].astype(F32) * pb).astype(BF)
        mg_ref[...] = mg
        m = _dot(mg, wo_ref[...])
        m_ref[...] = m
        x1_ref[...] = x_ref[...] + _norm_fwd(m, g_ref[...])

    return pl.pallas_call(
        body, name="fwd_merge", grid=(t // TM,),
        in_specs=[_rows(TM, D), _rows(TM, D), _rows(TM, ZG_W), _whole((D, D)), _whole((D, D)),
                  _whole((D, D)), _rows(TM, D), _whole((1, D))],
        out_specs=[_rows(TM, D), _rows(TM, D), _rows(TM, D), _rows(TM, D), _rows(TM, D)],
        out_shape=[jax.ShapeDtypeStruct((t, D), BF), jax.ShapeDtypeStruct((t, D), BF), jax.ShapeDtypeStruct((t, D), BF),
                   jax.ShapeDtypeStruct((t, D), F32), jax.ShapeDtypeStruct((t, D), F32)],
        compiler_params=_params(("parallel",)),
    )(ya, yb, gate, wa, wb, wo, x, g2)


def _fwd_ffn_up(x1, g3, wup):
    t = x1.shape[0]

    def body(x_ref, g_ref, w_ref, hn_ref, a_ref, r2_ref):
        hn = _norm_fwd(x_ref[...], g_ref[...]).astype(BF)
        hn_ref[...] = hn
        for c in range(0, DFF, 1024):
            h = _dot(hn, w_ref[:, c:c + 1024])
            r = jnp.maximum(h, 0.0)
            a_ref[:, c:c + 1024] = (r * r).astype(BF)
            r2_ref[:, c:c + 1024] = (2.0 * r).astype(BF)

    return pl.pallas_call(
        body, name="fwd_ffn_up", grid=(t // TM,),
        in_specs=[_rows(TM, D), _whole((1, D)), _whole((D, DFF))],
        out_specs=[_rows(TM, D), _rows(TM, DFF), _rows(TM, DFF)],
        out_shape=[jax.ShapeDtypeStruct((t, D), BF), jax.ShapeDtypeStruct((t, DFF), BF), jax.ShapeDtypeStruct((t, DFF), BF)],
        compiler_params=_params(("parallel",)),
    )(x1, g3, wup)


def _fwd_ffn_down(a, wdn, x1, g4):
    t = x1.shape[0]

    def body(a_ref, w_ref, x_ref, g_ref, f_ref, x2_ref):
        f = _dot(a_ref[...], w_ref[...])
        f_ref[...] = f
        x2_ref[...] = x_ref[...] + _norm_fwd(f, g_ref[...])

    return pl.pallas_call(
        body, name="fwd_ffn_down", grid=(t // TM,),
        in_specs=[_rows(TM, DFF), _whole((DFF, D)), _rows(TM, D), _whole((1, D))],
        out_specs=[_rows(TM, D), _rows(TM, D)],
        out_shape=[jax.ShapeDtypeStruct((t, D), F32), jax.ShapeDtypeStruct((t, D), F32)],
        compiler_params=_params(("parallel",)),
    )(a, wdn, x1, g4)


def _loss_head(y, target):
    t = y.shape[0]

    def body(y_ref, t_ref, dy_ref, loss_ref):
        d = y_ref[...] - t_ref[...]
        dy_ref[...] = d * (1.0 / D)
        part = 0.5 * jnp.sum(jnp.sum(d * d, axis=-1, keepdims=True) * (1.0 / D), axis=0, keepdims=True)
        _accumulate(loss_ref, jnp.broadcast_to(part, (8, HW)))

    return pl.pallas_call(
        body, name="loss_head", grid=(t // TM,),
        in_specs=[_rows(TM, D), _rows(TM, D)],
        out_specs=[_rows(TM, D), _whole((8, HW))],
        out_shape=[jax.ShapeDtypeStruct((t, D), F32), jax.ShapeDtypeStruct((8, HW), F32)],
        compiler_params=_params(("arbitrary",)),
    )(y, target)


def _bwd_ffn_down(f, dx2, g4, wdn, r2):
    t = f.shape[0]

    def body(f_ref, dx2_ref, g_ref, w_ref, r2_ref, df_ref, dh_ref, dg_ref):
        df, dg = _norm_bwd(f_ref[...], g_ref[...], dx2_ref[...])
        _accumulate(dg_ref, dg)
        df16 = df.astype(BF)
        df_ref[...] = df16
        for c in range(0, DFF, 1024):
            da = _dot_nt(df16, w_ref[c:c + 1024, :])
            dh_ref[:, c:c + 1024] = (da * r2_ref[:, c:c + 1024].astype(F32)).astype(BF)

    return pl.pallas_call(
        body, name="bwd_ffn_down", grid=(t // TM,),
        in_specs=[_rows(TM, D), _rows(TM, D), _whole((1, D)), _whole((DFF, D)), _rows(TM, DFF)],
        out_specs=[_rows(TM, D), _rows(TM, DFF), _whole((1, D))],
        out_shape=[jax.ShapeDtypeStruct((t, D), BF), jax.ShapeDtypeStruct((t, DFF), BF), jax.ShapeDtypeStruct((1, D), F32)],
        compiler_params=_params(("arbitrary",)),
    )(f, dx2, g4, wdn, r2)


def _bwd_ffn_up(dh, wup, x1, g3, dx2):
    t = x1.shape[0]

    def body(dh_ref, w_ref, x_ref, g_ref, dx2_ref, dx1_ref, dg_ref):
        dhn = _dot_nt(dh_ref[...], w_ref[...])
        dx, dg = _norm_bwd(x_ref[...], g_ref[...], dhn)
        _accumulate(dg_ref, dg)
        dx1_ref[...] = dx2_ref[...] + dx

    return pl.pallas_call(
        body, name="bwd_ffn_up", grid=(t // TM,),
        in_specs=[_rows(TM, DFF), _whole((D, DFF)), _rows(TM, D), _whole((1, D)), _rows(TM, D)],
        out_specs=[_rows(TM, D), _whole((1, D))],
        out_shape=[jax.ShapeDtypeStruct((t, D), F32), jax.ShapeDtypeStruct((1, D), F32)],
        compiler_params=_params(("arbitrary",)),
    )(dh, wup, x1, g3, dx2)


def _bwd_merge(m, dx1, g2, wo, gate, pa, pb, wa, wb):
    t = m.shape[0]

    def body(m_ref, dx1_ref, g_ref, wo_ref, gate_ref, pa_ref, pb_ref, wa_ref, wb_ref,
             dm_ref, dpa_ref, dpb_ref, dzg_ref, dya_ref, dyb_ref, dg_ref, db_ref):
        dm, dg = _norm_bwd(m_ref[...], g_ref[...], dx1_ref[...])
        _accumulate(dg_ref, dg)
        dm16 = dm.astype(BF)
        dm_ref[...] = dm16
        dmg = _dot_nt(dm16, wo_ref[...])
        ga = gate_ref[:, :D].astype(F32)
        gb = gate_ref[:, D:].astype(F32)
        dpa = (dmg * ga).astype(BF)
        dpb = (dmg * gb).astype(BF)
        dpa_ref[...] = dpa
        dpb_ref[...] = dpb
        dza = (dmg * pa_ref[...].astype(F32)) * (ga * (1.0 - ga))
        dzb = (dmg * pb_ref[...].astype(F32)) * (gb * (1.0 - gb))
        dzg_ref[:, :D] = dza.astype(BF)
        dzg_ref[:, D:] = dzb.astype(BF)

        @pl.when(pl.program_id(0) == 0)
        def _():
            db_ref[...] = jnp.zeros_like(db_ref)

        db_ref[:, :D] += jnp.sum(dza, axis=0, keepdims=True)
        db_ref[:, D:] += jnp.sum(dzb, axis=0, keepdims=True)
        dya_ref[...] = _dot_nt(dpa, wa_ref[...]).astype(BF)
        dyb_ref[...] = _dot_nt(dpb, wb_ref[...]).astype(BF)

    return pl.pallas_call(
        body, name="bwd_merge", grid=(t // TM,),
        in_specs=[_rows(TM, D), _rows(TM, D), _whole((1, D)), _whole((D, D)), _rows(TM, ZG_W),
                  _rows(TM, D), _rows(TM, D), _whole((D, D)), _whole((D, D))],
        out_specs=[_rows(TM, D), _rows(TM, D), _rows(TM, D), _rows(TM, ZG_W), _rows(TM, D), _rows(TM, D),
                   _whole((1, D)), _whole((1, ZG_W))],
        out_shape=[jax.ShapeDtypeStruct((t, D), BF), jax.ShapeDtypeStruct((t, D), BF), jax.ShapeDtypeStruct((t, D), BF),
                   jax.ShapeDtypeStruct((t, ZG_W), BF), jax.ShapeDtypeStruct((t, D), BF), jax.ShapeDtypeStruct((t, D), BF),
                   jax.ShapeDtypeStruct((1, D), F32), jax.ShapeDtypeStruct((1, ZG_W), F32)],
        compiler_params=_params(("arbitrary",)),
    )(m, dx1, g2, wo, gate, pa, pb, wa, wb)


def _flash_bwd(q, do, o, lse, k, v, group, scale, name, exchange=None):
    t = q.shape[0]
    tq = min(TQ_BWD_STEP, t)
    nq = t // tq
    nkv = NH // group

    def body(*refs):
        (q_ref, do_ref, o_ref, lse_ref, k_ref, v_ref), (dq_ref, dkt_ref, dvt_ref), ex_refs = _hosted(exchange, refs, 6, 3)
        first = (pl.program_id(1) == 0) & (pl.program_id(2) == 0)
        if exchange:
            @pl.when(first & (pl.program_id(0) == 0))
            def _():
                exchange.start(*ex_refs)

        @pl.when(first)
        def _():
            dkt_ref[...] = jnp.zeros_like(dkt_ref)
            dvt_ref[...] = jnp.zeros_like(dvt_ref)

        dvt = dkt = None
        for r0 in range(0, tq, TQ):
            rows = slice(r0, r0 + TQ)
            qv = q_ref[rows, :]
            dov = do_ref[rows, :]
            dsum = jnp.sum(dov.astype(F32) * o_ref[rows, :].astype(F32), axis=-1, keepdims=True)
            p = jnp.exp2(_dot_nt(qv, k_ref[...]) - lse_ref[rows, :1])
            ds = (p * (_dot_nt(dov, v_ref[...]) - dsum)).astype(BF)
            dq_ref[rows, :] = _dot(ds, k_ref[...]) * scale
            dvt_tile = _dot(dov.astype(F32).T.astype(BF), p.astype(BF))
            dkt_tile = _dot(qv.astype(F32).T.astype(BF), ds)
            dvt = dvt_tile if dvt is None else dvt + dvt_tile
            dkt = dkt_tile if dkt is None else dkt + dkt_tile
        dvt_ref[...] += dvt
        dkt_ref[...] += dkt * LN2
        if exchange:
            @pl.when((pl.program_id(0) == nkv - 1) & (pl.program_id(1) == group - 1) & (pl.program_id(2) == nq - 1))
            def _():
                exchange.finish(*ex_refs)

    qspec = pl.BlockSpec((tq, HW), lambda hk, g, i: (i, hk * group + g))
    kspec = pl.BlockSpec((t, HW), lambda hk, g, i: (0, hk))
    tspec = pl.BlockSpec((HW, t), lambda hk, g, i: (hk, 0))
    out = pl.pallas_call(
        body, name=name, grid=(nkv, group, nq),
        in_specs=[qspec, qspec, qspec, qspec, kspec, kspec] + (exchange.specs if exchange else []),
        out_specs=[qspec, tspec, tspec] + (exchange.specs if exchange else []),
        out_shape=[jax.ShapeDtypeStruct((t, NH * HW), F32), jax.ShapeDtypeStruct((nkv * HW, t), F32),
                   jax.ShapeDtypeStruct((nkv * HW, t), F32)] + (exchange.out_shape if exchange else []),
        scratch_shapes=exchange.scratch if exchange else [],
        compiler_params=_params(("arbitrary" if exchange else "parallel", "arbitrary", "arbitrary")),
    )(q, do, o, lse, k, v, *(exchange.srcs if exchange else []))
    return out[0], out[1], out[2], out[3:]


def _bwd_mid(zm, dqa, dkta, dvta, dqb, dktb, dvtb, gq, gk, gqa, gkva, wq, wk, wv, ca, sa, cb, sb):
    t = zm.shape[0]
    hw8 = NH * HW

    def body(zm_ref, dqa_ref, dkta_ref, dvta_ref, dqb_ref, dktb_ref, dvtb_ref, gq_ref, gk_ref, gqa_ref, gkva_ref,
             wq_ref, wk_ref, wv_ref, ca_ref, sa_ref, cb_ref, sb_ref,
             dzm_ref, dqbp_ref, dkb16_ref, dvb16_ref, dgq_ref, dgk_ref, dgqa_ref, dgkva_ref):
        ca_, sa_, cb_, sb_ = ca_ref[...], sa_ref[...], cb_ref[...], sb_ref[...]
        dgq = jnp.zeros((1, HW), F32)
        for h in range(NH):
            sl = slice(h * HW, (h + 1) * HW)
            dqn = _rope_bwd(dqa_ref[:, sl], ca_, sa_)
            dx, dg = _norm_bwd(zm_ref[:, O_QA + h * HW:O_QA + (h + 1) * HW], gq_ref[...], dqn, HEAD)
            dzm_ref[:, O_QA + h * HW:O_QA + (h + 1) * HW] = dx.astype(BF)
            dgq = dgq + dg
        _accumulate(dgq_ref, dgq)
        dgk = jnp.zeros((1, HW), F32)
        for j in range(GQA_KV):
            dk = dkta_ref[j * HW:(j + 1) * HW, :].T
            dv = dvta_ref[j * HW:(j + 1) * HW, :].T
            dkn = _rope_bwd(dk, ca_, sa_)
            dx, dg = _norm_bwd(zm_ref[:, O_KA + j * HW:O_KA + (j + 1) * HW], gk_ref[...], dkn, HEAD)
            dzm_ref[:, O_KA + j * HW:O_KA + (j + 1) * HW] = dx.astype(BF)
            dzm_ref[:, O_VA + j * HW:O_VA + (j + 1) * HW] = dv.astype(BF)
            dgk = dgk + dg
        _accumulate(dgk_ref, dgk)
        for h in range(NH):
            sl = slice(h * HW, (h + 1) * HW)
            dqbp_ref[:, sl] = _rope_bwd(dqb_ref[:, sl], cb_, sb_).astype(BF)
        dcqn = _dot_nt(dqbp_ref[...], wq_ref[...])
        dx, dg = _norm_bwd(zm_ref[:, O_CQ:O_CQ + Q_RANK], gqa_ref[...], dcqn)
        dzm_ref[:, O_CQ:O_CQ + Q_RANK] = dx.astype(BF)
        _accumulate(dgqa_ref, dg)
        dkr = jnp.zeros((TM, HW), F32)
        for h in range(NH):
            sl = slice(h * HW, (h + 1) * HW)
            dkh = dktb_ref[sl, :].T
            dkr = dkr + dkh
            dkb16_ref[:, sl] = dkh.astype(BF)
            dvb16_ref[:, sl] = dvtb_ref[sl, :].T.astype(BF)
        dkb16 = dkb16_ref[...]
        dvb16 = dvb16_ref[...]
        lane = lax.broadcasted_iota(jnp.int32, (TM, HW), 1)
        in_rope = (lane % (HW // 2)) < MLA_ROPE // 2
        dzm_ref[:, O_KR:O_KR + HW] = jnp.where(in_rope, _rope_bwd(dkr, cb_, sb_), 0.0).astype(BF)
        dckvn = _dot_nt(dkb16, wk_ref[...]) + _dot_nt(dvb16, wv_ref[...])
        dx, dg = _norm_bwd(zm_ref[:, O_CKV:O_CKV + KV_RANK], gkva_ref[...], dckvn)
        dzm_ref[:, O_CKV:O_CKV + KV_RANK] = dx.astype(BF)
        _accumulate(dgkva_ref, dg)

    return pl.pallas_call(
        body, name="bwd_mid", grid=(t // TM,),
        in_specs=[_rows(TM, ZM_W), _rows(TM, hw8), _cols(GQA_KV * HW, TM), _cols(GQA_KV * HW, TM), _rows(TM, hw8),
                  _cols(hw8, TM), _cols(hw8, TM), _whole((1, HW)), _whole((1, HW)), _whole((1, Q_RANK)), _whole((1, KV_RANK)),
                  _whole((Q_RANK, hw8)), _whole((KV_RANK, hw8)), _whole((KV_RANK, hw8)),
                  _rows(TM, HW), _rows(TM, HW), _rows(TM, HW), _rows(TM, HW)],
        out_specs=[_rows(TM, ZM_W), _rows(TM, hw8), _rows(TM, hw8), _rows(TM, hw8),
                   _whole((1, HW)), _whole((1, HW)), _whole((1, Q_RANK)), _whole((1, KV_RANK))],
        out_shape=[jax.ShapeDtypeStruct((t, ZM_W), BF), jax.ShapeDtypeStruct((t, hw8), BF), jax.ShapeDtypeStruct((t, hw8), BF),
                   jax.ShapeDtypeStruct((t, hw8), BF), jax.ShapeDtypeStruct((1, HW), F32), jax.ShapeDtypeStruct((1, HW), F32),
                   jax.ShapeDtypeStruct((1, Q_RANK), F32), jax.ShapeDtypeStruct((1, KV_RANK), F32)],
        compiler_params=_params(("arbitrary",)),
    )(zm, dqa, dkta, dvta, dqb, dktb, dvtb, gq, gk, gqa, gkva, wq, wk, wv, ca, sa, cb, sb)


def _bwd_inproj(dzm, dzg, w1, w2, x, g1, dx1):
    t = x.shape[0]

    def body(dzm_ref, dzg_ref, w1_ref, w2_ref, x_ref, g_ref, dx1_ref, dx_ref, dg_ref):
        du = _dot_nt(dzm_ref[...], w1_ref[...]) + _dot_nt(dzg_ref[...], w2_ref[...])
        dx, dg = _norm_bwd(x_ref[...], g_ref[...], du)
        _accumulate(dg_ref, dg)
        dx_ref[...] = dx1_ref[...] + dx

    return pl.pallas_call(
        body, name="bwd_inproj", grid=(t // TM,),
        in_specs=[_rows(TM, ZM_W), _rows(TM, ZG_W), _whole((D, ZM_W)), _whole((D, ZG_W)), _rows(TM, D), _whole((1, D)),
                  _rows(TM, D)],
        out_specs=[_rows(TM, D), _whole((1, D))],
        out_shape=[jax.ShapeDtypeStruct((t, D), F32), jax.ShapeDtypeStruct((1, D), F32)],
        compiler_params=_params(("arbitrary",)),
    )(dzm, dzg, w1, w2, x, g1, dx1)


def _matmul_tn(a, b, tn, name):
    t, kdim = a.shape
    n = b.shape[1]
    tm = min(TN_ROWS_NARROW if kdim <= D else TN_ROWS_WIDE, t)
    nsteps = t // tm

    def body(a_ref, b_ref, o_ref, acc_ref):
        i = pl.program_id(1)

        @pl.when(i == 0)
        def _():
            acc_ref[...] = jnp.zeros_like(acc_ref)

        acc_ref[...] += _dot_tn(a_ref[...].astype(BF), b_ref[...])

        @pl.when(i == nsteps - 1)
        def _():
            o_ref[...] = acc_ref[...].astype(BF)

    return pl.pallas_call(
        body, name=name, grid=(n // tn, nsteps),
        in_specs=[pl.BlockSpec((tm, kdim), lambda j, i: (i, 0)), pl.BlockSpec((tm, tn), lambda j, i: (i, j))],
        out_specs=pl.BlockSpec((kdim, tn), lambda j, i: (0, j)),
        out_shape=jax.ShapeDtypeStruct((kdim, n), BF),
        scratch_shapes=[pltpu.VMEM((kdim, tn), F32)],
        compiler_params=_params(("parallel", "arbitrary")),
    )(a, b)


def _pair_halves(w, heads, q):
    lead = w.shape[:-1]
    return jnp.swapaxes(w.reshape(lead + (heads, 2, 2, q)), -3, -2).reshape(lead + (heads, 2, 2 * q))


def _unpair_halves(p, heads, q):
    lead = p.shape[:-3]
    return jnp.swapaxes(p.reshape(lead + (heads, 2, 2, q)), -3, -2).reshape(lead + (heads * 4 * q,))


def _halves_to_lanes(h, before=0, fill=0.0):
    pad = [(0, 0)] * (h.ndim - 1) + [(before, HW // 2 - before - h.shape[-1])]
    return jnp.pad(h, pad, constant_values=fill).reshape(h.shape[:-3] + (-1,))


def _lanes_to_halves(p, start, width):
    return p.reshape(p.shape[:-1] + (-1, 2, HW // 2))[..., start:start + width]


def _rot_to_lanes(w, heads, q, fill=0.0):
    return _halves_to_lanes(_pair_halves(w, heads, q), 0, fill)


def _rot_from_lanes(p, heads, q):
    return _unpair_halves(_lanes_to_halves(p, 0, 2 * q), heads, q)


def _plain_to_lanes(w, heads):
    w = w.reshape(w.shape[:-1] + (heads, HEAD))
    return jnp.pad(w, [(0, 0)] * (w.ndim - 1) + [(0, HW - HEAD)]).reshape(w.shape[:-2] + (-1,))


def _plain_from_lanes(p, heads):
    return p.reshape(p.shape[:-1] + (heads, HW))[..., :HEAD].reshape(p.shape[:-1] + (-1,))


def _rope_tables(t):
    pos = jnp.arange(t, dtype=jnp.int32)
    row = (pos // GRID_W).astype(F32)
    col = (pos % GRID_W).astype(F32)

    def table(rot_dim):
        half = rot_dim // 2
        inv = ROPE_THETA ** (-jnp.arange(0, half, 2, dtype=F32) / half)
        ar = row[:, None] * inv[None, :]
        ac = col[:, None] * inv[None, :]
        ang = jnp.concatenate([ar, ar, ac, ac], axis=-1)
        return jnp.cos(ang), jnp.sin(ang)

    c64, s64 = table(HEAD)
    c32, s32 = table(MLA_ROPE)
    return (_rot_to_lanes(c64, 1, 16, 1.0), _rot_to_lanes(s64, 1, 16), _rot_to_lanes(c32, 1, 8, 1.0),
            _rot_to_lanes(s32, 1, 8))


def _pad_heads_rows(w, heads, width):
    n = w.shape[1]
    w = w.reshape(heads, width, n)
    return jnp.pad(w, ((0, 0), (0, HW - width), (0, 0))).reshape(heads * HW, n)


def _unpad_heads_rows(w, heads, width):
    n = w.shape[1]
    return w.reshape(heads, HW, n)[:, :width, :].reshape(heads * width, n)


def _pad_first(w):
    w_in = w["w_in"]
    qa = _rot_to_lanes(w_in[:, 0:512], NH, 16)
    ka = _rot_to_lanes(w_in[:, 512:640], GQA_KV, 16)
    va = _plain_to_lanes(w_in[:, 640:768], GQA_KV)
    cq = w_in[:, 768:1152]
    ckv = w_in[:, 1152:1408]
    kr = _rot_to_lanes(w_in[:, 1408:1440], 1, 8)
    qu = w["w_q_up"].reshape(Q_RANK, NH, MLA_QK)
    q_halves = jnp.concatenate([_pair_halves(qu[:, :, MLA_NOPE:].reshape(Q_RANK, NH * MLA_ROPE), NH, 8),
                                qu[:, :, :MLA_NOPE].reshape(Q_RANK, NH, 2, MLA_NOPE // 2)], axis=-1)
    kvu = w["w_kv_up"].reshape(KV_RANK, NH, 2 * HEAD)
    return dict(
        w1=jnp.concatenate([qa, ka, va, cq, ckv, kr], axis=1),
        w2=w_in[:, 1440:],
        wq=_halves_to_lanes(q_halves),
        wk=_halves_to_lanes(kvu[:, :, :HEAD].reshape(KV_RANK, NH, 2, HEAD // 2), MLA_ROPE // 2),
        wv=_plain_to_lanes(kvu[:, :, HEAD:].reshape(KV_RANK, NH * HEAD), NH),
    )


def _pad_rest(w):
    return dict(
        wa=_pad_heads_rows(w["w_branch_a"], NH, HEAD),
        wb=_pad_heads_rows(w["w_branch_b"], NH, HEAD),
        wo=w["w_o"], wup=w["w_ffn_up"], wdn=w["w_ffn_down"],
    )


def _unpad_first(g):
    d1 = g["w1"]
    w_in = jnp.concatenate([
        _rot_from_lanes(d1[:, O_QA:O_KA], NH, 16), _rot_from_lanes(d1[:, O_KA:O_VA], GQA_KV, 16),
        _plain_from_lanes(d1[:, O_VA:O_CQ], GQA_KV), d1[:, O_CQ:O_CKV], d1[:, O_CKV:O_KR],
        _rot_from_lanes(d1[:, O_KR:O_KR + HW], 1, 8), g["w2"]], axis=1)
    dk = _lanes_to_halves(g["wk"], MLA_ROPE // 2, HEAD // 2).reshape(KV_RANK, NH, HEAD)
    dv = _plain_from_lanes(g["wv"], NH).reshape(KV_RANK, NH, HEAD)
    dq_rope = _unpair_halves(_lanes_to_halves(g["wq"], 0, MLA_ROPE // 2), NH, 8).reshape(Q_RANK, NH, MLA_ROPE)
    dq_nope = _lanes_to_halves(g["wq"], MLA_ROPE // 2, MLA_NOPE // 2).reshape(Q_RANK, NH, MLA_NOPE)
    return dict(
        w_in=w_in,
        w_q_up=jnp.concatenate([dq_nope, dq_rope], axis=-1).reshape(Q_RANK, NH * MLA_QK),
        w_kv_up=jnp.concatenate([dk, dv], axis=2).reshape(KV_RANK, NH * 2 * HEAD),
    )


def _unpad_rest(g):
    return dict(
        w_branch_a=_unpad_heads_rows(g["wa"], NH, HEAD),
        w_branch_b=_unpad_heads_rows(g["wb"], NH, HEAD),
        w_o=g["wo"], w_ffn_up=g["wup"], w_ffn_down=g["wdn"],
    )


SHARD_AXIS = dict(w_in=1, w_q_up=1, w_kv_up=1, w_branch_a=1, w_branch_b=1, w_o=0, w_ffn_up=1, w_ffn_down=0)


def _join(parts, name):
    if SHARD_AXIS[name] == 0:
        return parts.reshape(-1, parts.shape[-1])
    return jnp.transpose(parts, (1, 0, 2)).reshape(parts.shape[1], -1)


def _split(full, name):
    if SHARD_AXIS[name] == 0:
        return full.reshape(4, -1, full.shape[-1])
    return jnp.transpose(full.reshape(full.shape[0], 4, -1), (1, 0, 2))


def _local_step(x, target, shards, smalls):
    t = x.shape[0]
    nl = len(smalls)
    ca, sa, cb, sb = _rope_tables(t)

    def joined(names, got):
        return {n: _join(g, n) for n, g in zip(names, got)}

    w = _pad_first(joined(FIRST, _chip_exchange([shards[n][0] for n in FIRST], False, "gather_weights")))
    layers, saved = [], []
    for li, s in enumerate(smalls):
        gq, gk = _rot_to_lanes(s["q_norm_g"][None], 1, 16), _rot_to_lanes(s["k_norm_g"][None], 1, 16)
        gqa, gkva = s["q_a_norm_g"].reshape(1, -1), s["kv_a_norm_g"].reshape(1, -1)
        g1, g2, g3, g4 = (s[n].reshape(1, D) for n in ("pre_mix_g", "post_mix_g", "pre_ffn_g", "post_ffn_g"))
        b = s["b_gate"].reshape(1, ZG_W)
        u, zm, gate = _fwd_inproj(x, g1, w["w1"], w["w2"], b)
        qa, ka, va, qb, kb, vb, cqn, ckvn = _fwd_mid(zm, gq, gk, gqa, gkva, w["wq"], w["wk"], w["wv"], ca, sa, cb, sb)
        last = li + 1 == nl
        gather = _ChipExchange([shards[n][li] for n in REST] + ([] if last else [shards[n][li + 1] for n in FIRST]), False)
        ya, lse_a, got = _flash_fwd(qa, ka, va, GQA_G, "flash_fwd_gqa", gather)
        yb, lse_b, _ = _flash_fwd(qb, kb, vb, 1, "flash_fwd_mla")
        w = {**w, **_pad_rest(joined(REST, got[:len(REST)]))}
        pa, pb, mg, m, x1 = _fwd_merge(ya, yb, gate, w["wa"], w["wb"], w["wo"], x, g2)
        hn, a, r2 = _fwd_ffn_up(x1, g3, w["wup"])
        f, x2 = _fwd_ffn_down(a, w["wdn"], x1, g4)
        saved.append(dict(x=x, u=u, zm=zm, gate=gate, qa=qa, ka=ka, va=va, qb=qb, kb=kb, vb=vb, cqn=cqn, ckvn=ckvn,
                          ya=ya, lse_a=lse_a, yb=yb, lse_b=lse_b, pa=pa, pb=pb, mg=mg,
                          m=m, x1=x1, hn=hn, a=a, r2=r2, f=f, gq=gq, gk=gk, gqa=gqa, gkva=gkva, g1=g1, g2=g2, g3=g3,
                          g4=g4))
        layers.append(w)
        x = x2
        if not last:
            w = _pad_first(joined(FIRST, got[len(REST):]))

    dx, loss8 = _loss_head(x, target)
    loss = loss8[0, 0]

    received, small_grads, send_first = [dict() for _ in range(nl)], [None] * nl, None
    for li in range(nl - 1, -1, -1):
        w, r = layers[li], saved[li]
        df, dh, dg4 = _bwd_ffn_down(r["f"], dx, r["g4"], w["wdn"], r["r2"])
        dx1, dg3 = _bwd_ffn_up(dh, w["wup"], r["x1"], r["g3"], dx)
        dm, dpa, dpb, dzg, dya, dyb, dg2, db = _bwd_merge(r["m"], dx1, r["g2"], w["wo"], r["gate"], r["pa"], r["pb"],
                                                         w["wa"], w["wb"])
        nat = _unpad_rest(dict(
            wa=_matmul_tn(r["ya"], dpa, 512, "dw_branch_a"), wb=_matmul_tn(r["yb"], dpb, 512, "dw_branch_b"),
            wo=_matmul_tn(r["mg"], dm, 512, "dw_o"), wup=_matmul_tn(r["hn"], dh, 512, "dw_ffn_up"),
            wdn=_matmul_tn(r["a"], df, 512, "dw_ffn_down")))
        scatter = _ChipExchange([_split(nat[n], n) for n in REST] + (send_first or []), True)
        dqa, dkta, dvta, got = _flash_bwd(r["qa"], dya, r["ya"], r["lse_a"], r["ka"], r["va"], GQA_G, SCALE_GQA,
                                          "flash_bwd_gqa", scatter)
        received[li].update(zip(REST, got[:len(REST)]))
        if send_first:
            received[li + 1].update(zip(FIRST, got[len(REST):]))
        dqb, dktb, dvtb, _ = _flash_bwd(r["qb"], dyb, r["yb"], r["lse_b"], r["kb"], r["vb"], 1, SCALE_MLA, "flash_bwd_mla")
        dzm, dqbp, dkb16, dvb16, dgq, dgk, dgqa, dgkva = _bwd_mid(
            r["zm"], dqa, dkta, dvta, dqb, dktb, dvtb, r["gq"], r["gk"], r["gqa"], r["gkva"], w["wq"], w["wk"], w["wv"],
            ca, sa, cb, sb)
        dx, dg1 = _bwd_inproj(dzm, dzg, w["w1"], w["w2"], r["x"], r["g1"], dx1)
        nat = _unpad_first(dict(
            w1=_matmul_tn(r["u"], dzm, 768, "dw_in_main"), w2=_matmul_tn(r["u"], dzg, 512, "dw_in_gate"),
            wq=_matmul_tn(r["cqn"], dqbp, 512, "dw_q_up"), wk=_matmul_tn(r["ckvn"], dkb16, 512, "dw_k_up"),
            wv=_matmul_tn(r["ckvn"], dvb16, 512, "dw_v_up")))
        send_first = [_split(nat[n], n) for n in FIRST]
        small_grads[li] = dict(b_gate=db[0], q_norm_g=_rot_from_lanes(dgq, 1, 16)[0], k_norm_g=_rot_from_lanes(dgk, 1, 16)[0], q_a_norm_g=dgqa[0],
                               kv_a_norm_g=dgkva[0], pre_mix_g=dg1[0], post_mix_g=dg2[0], pre_ffn_g=dg3[0],
                               post_ffn_g=dg4[0])
    received[0].update(zip(FIRST, _chip_exchange(send_first, True, "scatter_grads")))
    return loss, dx, received, small_grads


def _chip_exchange(srcs, per_dest, name):
    ex = _ChipExchange(srcs, per_dest)

    def body(*refs):
        _, _, ex_refs = _hosted(ex, refs, 0, 0)
        ex.start(*ex_refs)
        ex.finish(*ex_refs)

    return pl.pallas_call(
        body, name=name, in_specs=ex.specs, out_specs=ex.specs, out_shape=ex.out_shape, scratch_shapes=ex.scratch,
    )(*ex.srcs)


def _sibling_exchange(srcs, name):
    n = len(srcs)

    def body(*refs):
        src_refs, out_refs = refs[:n], refs[n:2 * n]
        send_sems, recv_sems = refs[2 * n:]
        x, y, c = lax.axis_index("x"), lax.axis_index("y"), lax.axis_index("c")
        cps = [pltpu.make_async_remote_copy(src_ref=src_refs[a], dst_ref=out_refs[a], send_sem=send_sems.at[a],
                                            recv_sem=recv_sems.at[a], device_id=(x, y, 1 - c), device_id_type=MESH)
               for a in range(n)]
        for cp in cps:
            cp.start()
        for cp in cps:
            cp.wait()

    return pl.pallas_call(
        body, name=name,
        in_specs=[pl.BlockSpec(memory_space=pl.ANY)] * n,
        out_specs=[pl.BlockSpec(memory_space=pl.ANY)] * n,
        out_shape=[jax.ShapeDtypeStruct(s.shape, s.dtype) for s in srcs],
        scratch_shapes=[pltpu.SemaphoreType.DMA((n,)), pltpu.SemaphoreType.DMA((n,))],
    )(*srcs)


def _allgather_small(v):
    m_per, n = v.shape

    def body(x_ref, out_ref, send_sems, recv_sems, local_sem):
        x, y, c = lax.axis_index("x"), lax.axis_index("y"), lax.axis_index("c")
        me, sibling = (x, y, c), (x, y, 1 - c)
        chips = [(1 - x, y), (x, 1 - y), (1 - x, 1 - y)]

        def rows(px, py, pc):
            return out_ref.at[pl.ds((4 * px + 2 * py + pc) * m_per, m_per), :]

        def copy(k, block, to, src=None):
            return pltpu.make_async_remote_copy(
                src_ref=rows(*block) if src is None else src, dst_ref=rows(*block),
                send_sem=send_sems.at[k], recv_sem=recv_sems.at[k], device_id=to, device_id_type=MESH)

        mine = pltpu.make_async_copy(x_ref, rows(*me), local_sem)
        mine.start()
        first = [copy(0, me, sibling, src=x_ref)]
        first += [copy(1 + j, me, (*chip, c), src=x_ref) for j, chip in enumerate(chips)]
        for cp in first:
            cp.start()
        passed = [copy(4 + j, (*chip, c), sibling) for j, chip in enumerate(chips)]
        for j, chip in enumerate(chips):
            copy(1 + j, (*chip, c), me).wait_recv()
            passed[j].start()
        copy(0, sibling, me).wait_recv()
        for j, chip in enumerate(chips):
            copy(4 + j, (*chip, 1 - c), me).wait_recv()
        for cp in first + passed:
            cp.wait_send()
        mine.wait()

    return pl.pallas_call(
        body, name="allgather_small",
        out_shape=jax.ShapeDtypeStruct((8 * m_per, n), v.dtype),
        in_specs=[pl.BlockSpec(memory_space=pltpu.VMEM)],
        out_specs=pl.BlockSpec(memory_space=pltpu.VMEM),
        scratch_shapes=[pltpu.SemaphoreType.DMA((7,)), pltpu.SemaphoreType.DMA((7,)), pltpu.SemaphoreType.DMA],
    )(v)


def _pick_rows(r):
    return next(t for t in (512, 256, 128, 64, 32, 16, 8, r) if r % t == 0)


def _sum_slabs(a, name):
    s, r, c = a.shape
    tm = _pick_rows(r)

    def body(a_ref, o_ref):
        acc = a_ref[0].astype(F32)
        for k in range(1, s):
            acc = acc + a_ref[k].astype(F32)
        o_ref[...] = acc

    return pl.pallas_call(
        body, name=name, grid=(r // tm,),
        in_specs=[pl.BlockSpec((s, tm, c), lambda i: (0, i, 0))],
        out_specs=_rows(tm, c),
        out_shape=jax.ShapeDtypeStruct((r, c), F32),
        compiler_params=_params(("parallel",)),
    )(a)


def _adamw(w, ga, gb, m, v, name):
    r, c = w.shape
    tm = min(256, _pick_rows(r))

    def body(w_ref, ga_ref, gb_ref, m_ref, v_ref, g_ref, d_ref, nm_ref, nv_ref):
        gv = ga_ref[...] + gb_ref[...]
        g_ref[...] = gv
        nm = ADAM_B1 * m_ref[...] + (1.0 - ADAM_B1) * gv
        nv = ADAM_B2 * v_ref[...] + (1.0 - ADAM_B2) * (gv * gv)
        m_hat = nm / (1.0 - ADAM_B1 ** ADAM_STEP)
        v_hat = nv / (1.0 - ADAM_B2 ** ADAM_STEP)
        d_ref[...] = -ADAM_LR * (m_hat / (jnp.sqrt(v_hat) + ADAM_EPS) + ADAM_WD * w_ref[...])
        nm_ref[...] = nm
        nv_ref[...] = nv

    spec = _rows(tm, c)
    return pl.pallas_call(
        body, name=name, grid=(r // tm,),
        in_specs=[spec] * 5, out_specs=[spec] * 4,
        out_shape=[jax.ShapeDtypeStruct((r, c), F32)] * 4,
        compiler_params=_params(("parallel",)),
    )(w, ga, gb, m, v)


def _adamw_small(w, gparts, m, v):
    mrows, n = w.shape

    def body(w_ref, g_ref, m_ref, v_ref, go_ref, d_ref, nm_ref, nv_ref):
        gv = g_ref[0]
        for k in range(1, 8):
            gv = gv + g_ref[k]
        go_ref[...] = gv
        nm = ADAM_B1 * m_ref[...] + (1.0 - ADAM_B1) * gv
        nv = ADAM_B2 * v_ref[...] + (1.0 - ADAM_B2) * (gv * gv)
        m_hat = nm / (1.0 - ADAM_B1 ** ADAM_STEP)
        v_hat = nv / (1.0 - ADAM_B2 ** ADAM_STEP)
        d_ref[...] = -ADAM_LR * (m_hat / (jnp.sqrt(v_hat) + ADAM_EPS) + ADAM_WD * w_ref[...])
        nm_ref[...] = nm
        nv_ref[...] = nv

    return pl.pallas_call(
        body, name="adamw_small",
        out_shape=[jax.ShapeDtypeStruct((mrows, n), F32)] * 4,
    )(w, gparts, m, v)


def _flat(parts):
    nl = parts[0].shape[0]
    return jnp.concatenate([p.reshape(nl, -1) for p in parts], axis=1)


def _unflat(flat, shapes):
    out, off = [], 0
    nl = flat.shape[0]
    for s in shapes:
        n = math.prod(s)
        out.append(flat[:, off:off + n].reshape((nl,) + tuple(s)))
        off += n
    return out


def kernel(x, w_in, b_gate, q_norm_g, k_norm_g, q_a_norm_g, kv_a_norm_g, w_q_up, w_kv_up, w_branch_a, w_branch_b, w_o, w_ffn_up, w_ffn_down, pre_mix_g, post_mix_g, pre_ffn_g, post_ffn_g, loss_target, m_w_in, m_b_gate, m_q_norm_g, m_k_norm_g, m_q_a_norm_g, m_kv_a_norm_g, m_w_q_up, m_w_kv_up, m_w_branch_a, m_w_branch_b, m_w_o, m_w_ffn_up, m_w_ffn_down, m_pre_mix_g, m_post_mix_g, m_pre_ffn_g, m_post_ffn_g, v_w_in, v_b_gate, v_q_norm_g, v_k_norm_g, v_q_a_norm_g, v_kv_a_norm_g, v_w_q_up, v_w_kv_up, v_w_branch_a, v_w_branch_b, v_w_o, v_w_ffn_up, v_w_ffn_down, v_pre_mix_g, v_post_mix_g, v_pre_ffn_g, v_post_ffn_g):
    wts = dict(w_in=w_in, b_gate=b_gate, q_norm_g=q_norm_g, k_norm_g=k_norm_g, q_a_norm_g=q_a_norm_g,
               kv_a_norm_g=kv_a_norm_g, w_q_up=w_q_up, w_kv_up=w_kv_up, w_branch_a=w_branch_a, w_branch_b=w_branch_b,
               w_o=w_o, w_ffn_up=w_ffn_up, w_ffn_down=w_ffn_down, pre_mix_g=pre_mix_g, post_mix_g=post_mix_g,
               pre_ffn_g=pre_ffn_g, post_ffn_g=post_ffn_g)
    mom = dict(w_in=m_w_in, b_gate=m_b_gate, q_norm_g=m_q_norm_g, k_norm_g=m_k_norm_g, q_a_norm_g=m_q_a_norm_g,
               kv_a_norm_g=m_kv_a_norm_g, w_q_up=m_w_q_up, w_kv_up=m_w_kv_up, w_branch_a=m_w_branch_a,
               w_branch_b=m_w_branch_b, w_o=m_w_o, w_ffn_up=m_w_ffn_up, w_ffn_down=m_w_ffn_down, pre_mix_g=m_pre_mix_g,
               post_mix_g=m_post_mix_g, pre_ffn_g=m_pre_ffn_g, post_ffn_g=m_post_ffn_g)
    var = dict(w_in=v_w_in, b_gate=v_b_gate, q_norm_g=v_q_norm_g, k_norm_g=v_k_norm_g, q_a_norm_g=v_q_a_norm_g,
               kv_a_norm_g=v_kv_a_norm_g, w_q_up=v_w_q_up, w_kv_up=v_w_kv_up, w_branch_a=v_w_branch_a,
               w_branch_b=v_w_branch_b, w_o=v_w_o, w_ffn_up=v_w_ffn_up, w_ffn_down=v_w_ffn_down, pre_mix_g=v_pre_mix_g,
               post_mix_g=v_post_mix_g, pre_ffn_g=v_pre_ffn_g, post_ffn_g=v_post_ffn_g)
    nl = w_in.shape[0]

    shards = {n: wts[n].astype(BF) for n in BIG}
    smalls = [{n: wts[n][li] for n in SMALL} for li in range(nl)]
    loss_local, dx, received, small_grads = _local_step(x[0], loss_target[0], shards, smalls)
    loss = lax.psum(loss_local, ("x", "y", "c"))
    part = []
    for i, n in enumerate(BIG):
        got = jnp.stack([received[li][n] for li in range(nl)], axis=1)
        part.append(_sum_slabs(got.reshape(4, -1, got.shape[-1]), "sum_chips_" + n))
    other = _sibling_exchange(part, "swap_cores")

    small_shapes = [wts[n].shape[1:] for n in SMALL]
    g_loc = _flat([jnp.stack([small_grads[li][n] for li in range(nl)]) for n in SMALL]).reshape(-1, 128)
    g_all = _allgather_small(g_loc).reshape(8, -1, 128)
    pack = lambda d: _flat([d[n] for n in SMALL]).reshape(-1, 128)
    gs, ds, ms, vs = _adamw_small(pack(wts), g_all, pack(mom), pack(var))
    unpack = lambda a: dict(zip(SMALL, _unflat(a.reshape(nl, -1), small_shapes)))
    g_small, d_small, m_small, v_small = unpack(gs), unpack(ds), unpack(ms), unpack(vs)

    out_g, out_d, out_m, out_v = dict(g_small), dict(d_small), dict(m_small), dict(v_small)
    for i, n in enumerate(BIG):
        shp = wts[n].shape
        two = lambda a: a.reshape(-1, shp[-1])
        g, d, nm, nv = _adamw(two(wts[n]), part[i], other[i], two(mom[n]), two(var[n]), "adamw_" + n)
        out_g[n], out_d[n], out_m[n], out_v[n] = g.reshape(shp), d.reshape(shp), nm.reshape(shp), nv.reshape(shp)

    return (loss, dx[None], *[out_g[n] for n in ORDER], *[out_d[n] for n in ORDER], *[out_m[n] for n in ORDER],
            *[out_v[n] for n in ORDER])
```

```python
import functools
import math

import jax
import jax.numpy as jnp
from jax import lax
from jax.experimental import pallas as pl
from jax.experimental.pallas import tpu as pltpu

F32 = jnp.float32
BF = jnp.bfloat16
MESH = pl.DeviceIdType.MESH

EPS = 1e-6
D = 1024
NH = 8
HW = 128
GQA_KV = 2
GQA_G = 4
HEAD = 64
MLA_NOPE = 64
MLA_ROPE = 32
MLA_QK = 96
Q_RANK = 384
KV_RANK = 256
DFF = 4096
GRID_W = 64
ROPE_THETA = 10000.0

O_QA, O_KA, O_VA, O_CQ, O_CKV, O_KR, ZM_W = 0, 1024, 1280, 1536, 1920, 2176, 2304
ZG_W = 2048

ADAM_LR, ADAM_B1, ADAM_B2, ADAM_EPS, ADAM_WD, ADAM_STEP = 0.001, 0.9, 0.999, 1e-08, 0.01, 10

TM = 512
TQ = 256
TQ_FWD_STEP = 1024
TQ_BWD_STEP = 512
ONES_LANE = 64
LOG2E = 1.4426950408889634
LN2 = 0.6931471805599453
SCALE_GQA = 1.0 / math.sqrt(HEAD)
SCALE_MLA = 1.0 / math.sqrt(MLA_QK)
C2_GQA = SCALE_GQA * LOG2E
C2_MLA = SCALE_MLA * LOG2E
VMEM_LIMIT = 56 * 1024 * 1024
TN_ROWS_NARROW, TN_ROWS_WIDE = 2048, 1024

BIG = ("w_in", "w_q_up", "w_kv_up", "w_branch_a", "w_branch_b", "w_o", "w_ffn_up", "w_ffn_down")
FIRST = BIG[:3]
REST = BIG[3:]
SMALL = ("b_gate", "q_norm_g", "k_norm_g", "q_a_norm_g", "kv_a_norm_g", "pre_mix_g", "post_mix_g", "pre_ffn_g", "post_ffn_g")
ORDER = ("w_in", "b_gate", "q_norm_g", "k_norm_g", "q_a_norm_g", "kv_a_norm_g", "w_q_up", "w_kv_up", "w_branch_a",
         "w_branch_b", "w_o", "w_ffn_up", "w_ffn_down", "pre_mix_g", "post_mix_g", "pre_ffn_g", "post_ffn_g")


def _params(sem=None):
    return pltpu.CompilerParams(dimension_semantics=sem, vmem_limit_bytes=VMEM_LIMIT)


def _rows(tm, w):
    return pl.BlockSpec((tm, w), lambda i: (i, 0))


def _cols(h, tm):
    return pl.BlockSpec((h, tm), lambda i: (0, i))


def _whole(shape):
    return pl.BlockSpec(shape, lambda i: (0,) * len(shape))


def _dot(a, b):
    return jnp.dot(a, b, preferred_element_type=F32)


def _dot_nt(a, b):
    return lax.dot_general(a, b, (((1,), (1,)), ((), ())), preferred_element_type=F32)


def _dot_tn(a, b):
    return lax.dot_general(a, b, (((0,), (0,)), ((), ())), preferred_element_type=F32)


def _norm_fwd(xv, g, n=None):
    n = xv.shape[-1] if n is None else n
    r = lax.rsqrt(jnp.sum(xv * xv, axis=-1, keepdims=True) * (1.0 / n) + EPS)
    return (xv * r) * g


def _norm_bwd(xv, g, dy, n=None):
    n = xv.shape[-1] if n is None else n
    r = lax.rsqrt(jnp.sum(xv * xv, axis=-1, keepdims=True) * (1.0 / n) + EPS)
    xh = xv * r
    dxh = dy * g
    dg = jnp.sum(dy * xh, axis=0, keepdims=True)
    dx = r * (dxh - xh * (jnp.sum(dxh * xh, axis=-1, keepdims=True) * (1.0 / n)))
    return dx, dg


def _accumulate(ref, val):
    @pl.when(pl.program_id(0) == 0)
    def _():
        ref[...] = jnp.zeros_like(ref)

    ref[...] += val


def _rot(xv):
    lane = lax.broadcasted_iota(jnp.int32, xv.shape, 1)
    return jnp.where(lane < HW // 2, -1.0, 1.0) * pltpu.roll(xv, HW // 2, 1)


def _rope(xv, c, s):
    return xv * c + _rot(xv) * s


def _rope_bwd(dy, c, s):
    return dy * c - _rot(dy * s)


def _sigmoid(z):
    return 1.0 / (1.0 + jnp.exp(-z))


def _fwd_inproj(x, g1, w1, w2, b):
    t = x.shape[0]

    def body(x_ref, g_ref, w1_ref, w2_ref, b_ref, u_ref, zm_ref, gate_ref):
        u = _norm_fwd(x_ref[...], g_ref[...]).astype(BF)
        u_ref[...] = u
        for c in range(0, ZM_W, 768):
            zm_ref[:, c:c + 768] = _dot(u, w1_ref[:, c:c + 768])
        for c in range(0, ZG_W, 512):
            gate_ref[:, c:c + 512] = _sigmoid(_dot(u, w2_ref[:, c:c + 512]) + b_ref[:, c:c + 512]).astype(BF)

    return pl.pallas_call(
        body, name="fwd_inproj", grid=(t // TM,),
        in_specs=[_rows(TM, D), _whole((1, D)), _whole((D, ZM_W)), _whole((D, ZG_W)), _whole((1, ZG_W))],
        out_specs=[_rows(TM, D), _rows(TM, ZM_W), _rows(TM, ZG_W)],
        out_shape=[jax.ShapeDtypeStruct((t, D), BF), jax.ShapeDtypeStruct((t, ZM_W), F32),
                   jax.ShapeDtypeStruct((t, ZG_W), BF)],
        compiler_params=_params(("parallel",)),
    )(x, g1, w1, w2, b)


def _fwd_mid(zm, gq, gk, gqa, gkva, wq, wk, wv, ca, sa, cb, sb):
    t = zm.shape[0]

    def body(zm_ref, gq_ref, gk_ref, gqa_ref, gkva_ref, wq_ref, wk_ref, wv_ref, ca_ref, sa_ref, cb_ref, sb_ref,
             qa_ref, ka_ref, va_ref, qb_ref, kb_ref, vb_ref, cqn_ref, ckvn_ref):
        ca_, sa_, cb_, sb_ = ca_ref[...], sa_ref[...], cb_ref[...], sb_ref[...]
        for h in range(NH):
            xv = zm_ref[:, O_QA + h * HW:O_QA + (h + 1) * HW]
            qa_ref[:, h * HW:(h + 1) * HW] = (_rope(_norm_fwd(xv, gq_ref[...], HEAD), ca_, sa_) * C2_GQA).astype(BF)
        for h in range(GQA_KV):
            xv = zm_ref[:, O_KA + h * HW:O_KA + (h + 1) * HW]
            ka_ref[:, h * HW:(h + 1) * HW] = _rope(_norm_fwd(xv, gk_ref[...], HEAD), ca_, sa_).astype(BF)
        ones = lax.broadcasted_iota(jnp.int32, (TM, HW), 1) == ONES_LANE
        for h in range(GQA_KV):
            va_ref[:, h * HW:(h + 1) * HW] = jnp.where(ones, 1.0, zm_ref[:, O_VA + h * HW:O_VA + (h + 1) * HW]).astype(BF)
        cqn = _norm_fwd(zm_ref[:, O_CQ:O_CQ + Q_RANK], gqa_ref[...]).astype(BF)
        ckvn = _norm_fwd(zm_ref[:, O_CKV:O_CKV + KV_RANK], gkva_ref[...]).astype(BF)
        cqn_ref[...] = cqn
        ckvn_ref[...] = ckvn
        qb = _dot(cqn, wq_ref[...])
        kpre = _dot(ckvn, wk_ref[...])
        kr = _rope(zm_ref[:, O_KR:O_KR + HW], cb_, sb_)
        for h in range(NH):
            sl = slice(h * HW, (h + 1) * HW)
            qb_ref[:, sl] = (_rope(qb[:, sl], cb_, sb_) * C2_MLA).astype(BF)
            kb_ref[:, sl] = (kpre[:, sl] + kr).astype(BF)
        vb = _dot(ckvn, wv_ref[...])
        for h in range(NH):
            vb_ref[:, h * HW:(h + 1) * HW] = jnp.where(ones, 1.0, vb[:, h * HW:(h + 1) * HW]).astype(BF)

    hw8 = NH * HW
    return pl.pallas_call(
        body, name="fwd_mid", grid=(t // TM,),
        in_specs=[_rows(TM, ZM_W), _whole((1, HW)), _whole((1, HW)), _whole((1, Q_RANK)), _whole((1, KV_RANK)),
                  _whole((Q_RANK, hw8)), _whole((KV_RANK, hw8)), _whole((KV_RANK, hw8)),
                  _rows(TM, HW), _rows(TM, HW), _rows(TM, HW), _rows(TM, HW)],
        out_specs=[_rows(TM, hw8), _rows(TM, GQA_KV * HW), _rows(TM, GQA_KV * HW), _rows(TM, hw8), _rows(TM, hw8),
                   _rows(TM, hw8), _rows(TM, Q_RANK), _rows(TM, KV_RANK)],
        out_shape=[jax.ShapeDtypeStruct((t, hw8), BF), jax.ShapeDtypeStruct((t, GQA_KV * HW), BF),
                   jax.ShapeDtypeStruct((t, GQA_KV * HW), BF), jax.ShapeDtypeStruct((t, hw8), BF),
                   jax.ShapeDtypeStruct((t, hw8), BF), jax.ShapeDtypeStruct((t, hw8), BF),
                   jax.ShapeDtypeStruct((t, Q_RANK), BF), jax.ShapeDtypeStruct((t, KV_RANK), BF)],
        compiler_params=_params(("parallel",)),
    )(zm, gq, gk, gqa, gkva, wq, wk, wv, ca, sa, cb, sb)


class _ChipExchange:
    def __init__(self, srcs, per_dest):
        self.srcs, self.per_dest, self.n = list(srcs), per_dest, len(srcs)
        self.out_shape = [jax.ShapeDtypeStruct((4,) + tuple(s.shape[1:] if per_dest else s.shape), s.dtype) for s in srcs]
        self.specs = [pl.BlockSpec(memory_space=pl.ANY)] * self.n
        self.scratch = [pltpu.SemaphoreType.DMA((3 * self.n,)), pltpu.SemaphoreType.DMA((3 * self.n,)),
                        pltpu.SemaphoreType.DMA((self.n,))]

    def _copies(self, src_refs, out_refs, send_sems, recv_sems, local_sems):
        x, y, c = lax.axis_index("x"), lax.axis_index("y"), lax.axis_index("c")
        me = 2 * x + y
        chips = [(1 - x, y), (x, 1 - y), (1 - x, 1 - y)]

        def piece(a, k):
            return src_refs[a].at[k] if self.per_dest else src_refs[a]

        def remote(a, j, src_chip, dst_slab):
            px, py = chips[j]
            return pltpu.make_async_remote_copy(
                src_ref=piece(a, src_chip), dst_ref=out_refs[a].at[dst_slab], send_sem=send_sems.at[3 * a + j],
                recv_sem=recv_sems.at[3 * a + j], device_id=(px, py, c), device_id_type=MESH)

        local = [pltpu.make_async_copy(piece(a, me), out_refs[a].at[me], local_sems.at[a]) for a in range(self.n)]
        sends = [remote(a, j, 2 * chips[j][0] + chips[j][1], me) for a in range(self.n) for j in range(3)]
        recvs = [remote(a, j, me, 2 * chips[j][0] + chips[j][1]) for a in range(self.n) for j in range(3)]
        return local, sends, recvs

    def start(self, *refs):
        local, sends, _ = self._copies(*refs)
        for cp in local + sends:
            cp.start()

    def finish(self, *refs):
        local, sends, recvs = self._copies(*refs)
        for cp in recvs:
            cp.wait_recv()
        for cp in sends:
            cp.wait_send()
        for cp in local:
            cp.wait()


def _hosted(exchange, refs, n_in, n_out):
    n = exchange.n if exchange else 0
    ins, srcs = refs[:n_in], refs[n_in:n_in + n]
    outs = refs[n_in + n:n_in + n + n_out]
    rest = refs[n_in + n + n_out:]
    return ins, outs, (tuple(srcs), tuple(rest[:n])) + tuple(rest[n:])


def _flash_fwd(q, k, v, group, name, exchange=None):
    t = q.shape[0]
    tq = min(TQ_FWD_STEP, t)
    nq = t // tq

    def body(*refs):
        (q_ref, k_ref, v_ref), (o_ref, lse_ref), ex_refs = _hosted(exchange, refs, 3, 2)
        if exchange:
            @pl.when((pl.program_id(0) == 0) & (pl.program_id(1) == 0))
            def _():
                exchange.start(*ex_refs)

        for r0 in range(0, tq, TQ):
            rows = slice(r0, r0 + TQ)
            s = _dot_nt(q_ref[rows, :], k_ref[...])
            m = jnp.max(s, axis=-1, keepdims=True)
            acc = _dot(jnp.exp2(s - m).astype(BF), v_ref[...])
            l = acc[:, ONES_LANE:ONES_LANE + 1]
            o_ref[rows, :] = (acc / l).astype(BF)
            lse_ref[rows, :] = jnp.broadcast_to(m + jnp.log2(l), (TQ, HW))
        if exchange:
            @pl.when((pl.program_id(0) == NH - 1) & (pl.program_id(1) == nq - 1))
            def _():
                exchange.finish(*ex_refs)

    qspec = pl.BlockSpec((tq, HW), lambda h, i: (i, h))
    kspec = pl.BlockSpec((t, HW), lambda h, i: (0, h // group))
    out = pl.pallas_call(
        body, name=name, grid=(NH, nq),
        in_specs=[qspec, kspec, kspec] + (exchange.specs if exchange else []),
        out_specs=[qspec, qspec] + (exchange.specs if exchange else []),
        out_shape=[jax.ShapeDtypeStruct((t, NH * HW), BF), jax.ShapeDtypeStruct((t, NH * HW), F32)]
        + (exchange.out_shape if exchange else []),
        scratch_shapes=exchange.scratch if exchange else [],
        compiler_params=_params(("arbitrary", "arbitrary") if exchange else ("parallel", "parallel")),
    )(q, k, v, *(exchange.srcs if exchange else []))
    return out[0], out[1], out[2:]


def _fwd_merge(ya, yb, gate, wa, wb, wo, x, g2):
    t = x.shape[0]

    def body(ya_ref, yb_ref, gate_ref, wa_ref, wb_ref, wo_ref, x_ref, g_ref, pa_ref, pb_ref, mg_ref, m_ref, x1_ref):
        pa = _dot(ya_ref[...], wa_ref[...])
        pb = _dot(yb_ref[...], wb_ref[...])
        pa_ref[...] = pa.astype(BF)
        pb_ref[...] = pb.astype(BF)
        mg = (gate_ref[:, :D].astype(F32) * pa + gate_ref[:, D:].astype(F32) * pb).astype(BF)
        mg_ref[...] = mg
        m = _dot(mg, wo_ref[...])
        m_ref[...] = m
        x1_ref[...] = x_ref[...] + _norm_fwd(m, g_ref[...])

    return pl.pallas_call(
        body, name="fwd_merge", grid=(t // TM,),
        in_specs=[_rows(TM, D), _rows(TM, D), _rows(TM, ZG_W), _whole((D, D)), _whole((D, D)),
                  _whole((D, D)), _rows(TM, D), _whole((1, D))],
        out_specs=[_rows(TM, D), _rows(TM, D), _rows(TM, D), _rows(TM, D), _rows(TM, D)],
        out_shape=[jax.ShapeDtypeStruct((t, D), BF), jax.ShapeDtypeStruct((t, D), BF), jax.ShapeDtypeStruct((t, D), BF),
                   jax.ShapeDtypeStruct((t, D), F32), jax.ShapeDtypeStruct((t, D), F32)],
        compiler_params=_params(("parallel",)),
    )(ya, yb, gate, wa, wb, wo, x, g2)


def _fwd_ffn_up(x1, g3, wup):
    t = x1.shape[0]

    def body(x_ref, g_ref, w_ref, hn_ref, a_ref, r2_ref):
        hn = _norm_fwd(x_ref[...], g_ref[...]).astype(BF)
        hn_ref[...] = hn
        for c in range(0, DFF, 1024):
            h = _dot(hn, w_ref[:, c:c + 1024])
            r = jnp.maximum(h, 0.0)
            a_ref[:, c:c + 1024] = (r * r).astype(BF)
            r2_ref[:, c:c + 1024] = (2.0 * r).astype(BF)

    return pl.pallas_call(
        body, name="fwd_ffn_up", grid=(t // TM,),
        in_specs=[_rows(TM, D), _whole((1, D)), _whole((D, DFF))],
        out_specs=[_rows(TM, D), _rows(TM, DFF), _rows(TM, DFF)],
        out_shape=[jax.ShapeDtypeStruct((t, D), BF), jax.ShapeDtypeStruct((t, DFF), BF), jax.ShapeDtypeStruct((t, DFF), BF)],
        compiler_params=_params(("parallel",)),
    )(x1, g3, wup)


def _fwd_ffn_down(a, wdn, x1, g4):
    t = x1.shape[0]

    def body(a_ref, w_ref, x_ref, g_ref, f_ref, x2_ref):
        f = _dot(a_ref[...], w_ref[...])
        f_ref[...] = f
        x2_ref[...] = x_ref[...] + _norm_fwd(f, g_ref[...])

    return pl.pallas_call(
        body, name="fwd_ffn_down", grid=(t // TM,),
        in_specs=[_rows(TM, DFF), _whole((DFF, D)), _rows(TM, D), _whole((1, D))],
        out_specs=[_rows(TM, D), _rows(TM, D)],
        out_shape=[jax.ShapeDtypeStruct((t, D), F32), jax.ShapeDtypeStruct((t, D), F32)],
        compiler_params=_params(("parallel",)),
    )(a, wdn, x1, g4)


def _loss_head(y, target):
    t = y.shape[0]

    def body(y_ref, t_ref, dy_ref, loss_ref):
        d = y_ref[...] - t_ref[...]
        dy_ref[...] = d * (1.0 / D)
        part = 0.5 * jnp.sum(jnp.sum(d * d, axis=-1, keepdims=True) * (1.0 / D), axis=0, keepdims=True)
        _accumulate(loss_ref, jnp.broadcast_to(part, (8, HW)))

    return pl.pallas_call(
        body, name="loss_head", grid=(t // TM,),
        in_specs=[_rows(TM, D), _rows(TM, D)],
        out_specs=[_rows(TM, D), _whole((8, HW))],
        out_shape=[jax.ShapeDtypeStruct((t, D), F32), jax.ShapeDtypeStruct((8, HW), F32)],
        compiler_params=_params(("arbitrary",)),
    )(y, target)


def _bwd_ffn_down(f, dx2, g4, wdn, r2):
    t = f.shape[0]

    def body(f_ref, dx2_ref, g_ref, w_ref, r2_ref, df_ref, dh_ref, dg_ref):
        df, dg = _norm_bwd(f_ref[...], g_ref[...], dx2_ref[...])
        _accumulate(dg_ref, dg)
        df16 = df.astype(BF)
        df_ref[...] = df16
        for c in range(0, DFF, 1024):
            da = _dot_nt(df16, w_ref[c:c + 1024, :])
            dh_ref[:, c:c + 1024] = (da * r2_ref[:, c:c + 1024].astype(F32)).astype(BF)

    return pl.pallas_call(
        body, name="bwd_ffn_down", grid=(t // TM,),
        in_specs=[_rows(TM, D), _rows(TM, D), _whole((1, D)), _whole((DFF, D)), _rows(TM, DFF)],
        out_specs=[_rows(TM, D), _rows(TM, DFF), _whole((1, D))],
        out_shape=[jax.ShapeDtypeStruct((t, D), BF), jax.ShapeDtypeStruct((t, DFF), BF), jax.ShapeDtypeStruct((1, D), F32)],
        compiler_params=_params(("arbitrary",)),
    )(f, dx2, g4, wdn, r2)


def _bwd_ffn_up(dh, wup, x1, g3, dx2):
    t = x1.shape[0]

    def body(dh_ref, w_ref, x_ref, g_ref, dx2_ref, dx1_ref, dg_ref):
        dhn = _dot_nt(dh_ref[...], w_ref[...])
        dx, dg = _norm_bwd(x_ref[...], g_ref[...], dhn)
        _accumulate(dg_ref, dg)
        dx1_ref[...] = dx2_ref[...] + dx

    return pl.pallas_call(
        body, name="bwd_ffn_up", grid=(t // TM,),
        in_specs=[_rows(TM, DFF), _whole((D, DFF)), _rows(TM, D), _whole((1, D)), _rows(TM, D)],
        out_specs=[_rows(TM, D), _whole((1, D))],
        out_shape=[jax.ShapeDtypeStruct((t, D), F32), jax.ShapeDtypeStruct((1, D), F32)],
        compiler_params=_params(("arbitrary",)),
    )(dh, wup, x1, g3, dx2)


def _bwd_merge(m, dx1, g2, wo, gate, pa, pb, wa, wb):
    t = m.shape[0]

    def body(m_ref, dx1_ref, g_ref, wo_ref, gate_ref, pa_ref, pb_ref, wa_ref, wb_ref,
             dm_ref, dpa_ref, dpb_ref, dzg_ref, dya_ref, dyb_ref, dg_ref, db_ref):
        dm, dg = _norm_bwd(m_ref[...], g_ref[...], dx1_ref[...])
        _accumulate(dg_ref, dg)
        dm16 = dm.astype(BF)
        dm_ref[...] = dm16
        dmg = _dot_nt(dm16, wo_ref[...])
        ga = gate_ref[:, :D].astype(F32)
        gb = gate_ref[:, D:].astype(F32)
        dpa = (dmg * ga).astype(BF)
        dpb = (dmg * gb).astype(BF)
        dpa_ref[...] = dpa
        dpb_ref[...] = dpb
        dza = (dmg * pa_ref[...].astype(F32)) * (ga * (1.0 - ga))
        dzb = (dmg * pb_ref[...].astype(F32)) * (gb * (1.0 - gb))
        dzg_ref[:, :D] = dza.astype(BF)
        dzg_ref[:, D:] = dzb.astype(BF)

        @pl.when(pl.program_id(0) == 0)
        def _():
            db_ref[...] = jnp.zeros_like(db_ref)

        db_ref[:, :D] += jnp.sum(dza, axis=0, keepdims=True)
        db_ref[:, D:] += jnp.sum(dzb, axis=0, keepdims=True)
        dya_ref[...] = _dot_nt(dpa, wa_ref[...]).astype(BF)
        dyb_ref[...] = _dot_nt(dpb, wb_ref[...]).astype(BF)

    return pl.pallas_call(
        body, name="bwd_merge", grid=(t // TM,),
        in_specs=[_rows(TM, D), _rows(TM, D), _whole((1, D)), _whole((D, D)), _rows(TM, ZG_W),
                  _rows(TM, D), _rows(TM, D), _whole((D, D)), _whole((D, D))],
        out_specs=[_rows(TM, D), _rows(TM, D), _rows(TM, D), _rows(TM, ZG_W), _rows(TM, D), _rows(TM, D),
                   _whole((1, D)), _whole((1, ZG_W))],
        out_shape=[jax.ShapeDtypeStruct((t, D), BF), jax.ShapeDtypeStruct((t, D), BF), jax.ShapeDtypeStruct((t, D), BF),
                   jax.ShapeDtypeStruct((t, ZG_W), BF), jax.ShapeDtypeStruct((t, D), BF), jax.ShapeDtypeStruct((t, D), BF),
                   jax.ShapeDtypeStruct((1, D), F32), jax.ShapeDtypeStruct((1, ZG_W), F32)],
        compiler_params=_params(("arbitrary",)),
    )(m, dx1, g2, wo, gate, pa, pb, wa, wb)


def _flash_bwd(q, do, o, lse, k, v, group, scale, name, exchange=None):
    t = q.shape[0]
    tq = min(TQ_BWD_STEP, t)
    nq = t // tq
    nkv = NH // group

    def body(*refs):
        (q_ref, do_ref, o_ref, lse_ref, k_ref, v_ref), (dq_ref, dkt_ref, dvt_ref), ex_refs = _hosted(exchange, refs, 6, 3)
        first = (pl.program_id(1) == 0) & (pl.program_id(2) == 0)
        if exchange:
            @pl.when(first & (pl.program_id(0) == 0))
            def _():
                exchange.start(*ex_refs)

        @pl.when(first)
        def _():
            dkt_ref[...] = jnp.zeros_like(dkt_ref)
            dvt_ref[...] = jnp.zeros_like(dvt_ref)

        dvt = dkt = None
        for r0 in range(0, tq, TQ):
            rows = slice(r0, r0 + TQ)
            qv = q_ref[rows, :]
            dov = do_ref[rows, :]
            dsum = jnp.sum(dov.astype(F32) * o_ref[rows, :].astype(F32), axis=-1, keepdims=True)
            p = jnp.exp2(_dot_nt(qv, k_ref[...]) - lse_ref[rows, :1])
            ds = (p * (_dot_nt(dov, v_ref[...]) - dsum)).astype(BF)
            dq_ref[rows, :] = _dot(ds, k_ref[...]) * scale
            dvt_tile = _dot(dov.astype(F32).T.astype(BF), p.astype(BF))
            dkt_tile = _dot(qv.astype(F32).T.astype(BF), ds)
            dvt = dvt_tile if dvt is None else dvt + dvt_tile
            dkt = dkt_tile if dkt is None else dkt + dkt_tile
        dvt_ref[...] += dvt
        dkt_ref[...] += dkt * LN2
        if exchange:
            @pl.when((pl.program_id(0) == nkv - 1) & (pl.program_id(1) == group - 1) & (pl.program_id(2) == nq - 1))
            def _():
                exchange.finish(*ex_refs)

    qspec = pl.BlockSpec((tq, HW), lambda hk, g, i: (i, hk * group + g))
    kspec = pl.BlockSpec((t, HW), lambda hk, g, i: (0, hk))
    tspec = pl.BlockSpec((HW, t), lambda hk, g, i: (hk, 0))
    out = pl.pallas_call(
        body, name=name, grid=(nkv, group, nq),
        in_specs=[qspec, qspec, qspec, qspec, kspec, kspec] + (exchange.specs if exchange else []),
        out_specs=[qspec, tspec, tspec] + (exchange.specs if exchange else []),
        out_shape=[jax.ShapeDtypeStruct((t, NH * HW), F32), jax.ShapeDtypeStruct((nkv * HW, t), F32),
                   jax.ShapeDtypeStruct((nkv * HW, t), F32)] + (exchange.out_shape if exchange else []),
        scratch_shapes=exchange.scratch if exchange else [],
        compiler_params=_params(("arbitrary" if exchange else "parallel", "arbitrary", "arbitrary")),
    )(q, do, o, lse, k, v, *(exchange.srcs if exchange else []))
    return out[0], out[1], out[2], out[3:]


def _bwd_mid(zm, dqa, dkta, dvta, dqb, dktb, dvtb, gq, gk, gqa, gkva, wq, wk, wv, ca, sa, cb, sb):
    t = zm.shape[0]
    hw8 = NH * HW

    def body(zm_ref, dqa_ref, dkta_ref, dvta_ref, dqb_ref, dktb_ref, dvtb_ref, gq_ref, gk_ref, gqa_ref, gkva_ref,
             wq_ref, wk_ref, wv_ref, ca_ref, sa_ref, cb_ref, sb_ref,
             dzm_ref, dqbp_ref, dkb16_ref, dvb16_ref, dgq_ref, dgk_ref, dgqa_ref, dgkva_ref):
        ca_, sa_, cb_, sb_ = ca_ref[...], sa_ref[...], cb_ref[...], sb_ref[...]
        dgq = jnp.zeros((1, HW), F32)
        for h in range(NH):
            sl = slice(h * HW, (h + 1) * HW)
            dqn = _rope_bwd(dqa_ref[:, sl], ca_, sa_)
            dx, dg = _norm_bwd(zm_ref[:, O_QA + h * HW:O_QA + (h + 1) * HW], gq_ref[...], dqn, HEAD)
            dzm_ref[:, O_QA + h * HW:O_QA + (h + 1) * HW] = dx.astype(BF)
            dgq = dgq + dg
        _accumulate(dgq_ref, dgq)
        dgk = jnp.zeros((1, HW), F32)
        for j in range(GQA_KV):
            dk = dkta_ref[j * HW:(j + 1) * HW, :].T
            dv = dvta_ref[j * HW:(j + 1) * HW, :].T
            dkn = _rope_bwd(dk, ca_, sa_)
            dx, dg = _norm_bwd(zm_ref[:, O_KA + j * HW:O_KA + (j + 1) * HW], gk_ref[...], dkn, HEAD)
            dzm_ref[:, O_KA + j * HW:O_KA + (j + 1) * HW] = dx.astype(BF)
            dzm_ref[:, O_VA + j * HW:O_VA + (j + 1) * HW] = dv.astype(BF)
            dgk = dgk + dg
        _accumulate(dgk_ref, dgk)
        for h in range(NH):
            sl = slice(h * HW, (h + 1) * HW)
            dqbp_ref[:, sl] = _rope_bwd(dqb_ref[:, sl], cb_, sb_).astype(BF)
        dcqn = _dot_nt(dqbp_ref[...], wq_ref[...])
        dx, dg = _norm_bwd(zm_ref[:, O_CQ:O_CQ + Q_RANK], gqa_ref[...], dcqn)
        dzm_ref[:, O_CQ:O_CQ + Q_RANK] = dx.astype(BF)
        _accumulate(dgqa_ref, dg)
        dkr = jnp.zeros((TM, HW), F32)
        for h in range(NH):
            sl = slice(h * HW, (h + 1) * HW)
            dkh = dktb_ref[sl, :].T
            dkr = dkr + dkh
            dkb16_ref[:, sl] = dkh.astype(BF)
            dvb16_ref[:, sl] = dvtb_ref[sl, :].T.astype(BF)
        dkb16 = dkb16_ref[...]
        dvb16 = dvb16_ref[...]
        lane = lax.broadcasted_iota(jnp.int32, (TM, HW), 1)
        in_rope = (lane % (HW // 2)) < MLA_ROPE // 2
        dzm_ref[:, O_KR:O_KR + HW] = jnp.where(in_rope, _rope_bwd(dkr, cb_, sb_), 0.0).astype(BF)
        dckvn = _dot_nt(dkb16, wk_ref[...]) + _dot_nt(dvb16, wv_ref[...])
        dx, dg = _norm_bwd(zm_ref[:, O_CKV:O_CKV + KV_RANK], gkva_ref[...], dckvn)
        dzm_ref[:, O_CKV:O_CKV + KV_RANK] = dx.astype(BF)
        _accumulate(dgkva_ref, dg)

    return pl.pallas_call(
        body, name="bwd_mid", grid=(t // TM,),
        in_specs=[_rows(TM, ZM_W), _rows(TM, hw8), _cols(GQA_KV * HW, TM), _cols(GQA_KV * HW, TM), _rows(TM, hw8),
                  _cols(hw8, TM), _cols(hw8, TM), _whole((1, HW)), _whole((1, HW)), _whole((1, Q_RANK)), _whole((1, KV_RANK)),
                  _whole((Q_RANK, hw8)), _whole((KV_RANK, hw8)), _whole((KV_RANK, hw8)),
                  _rows(TM, HW), _rows(TM, HW), _rows(TM, HW), _rows(TM, HW)],
        out_specs=[_rows(TM, ZM_W), _rows(TM, hw8), _rows(TM, hw8), _rows(TM, hw8),
                   _whole((1, HW)), _whole((1, HW)), _whole((1, Q_RANK)), _whole((1, KV_RANK))],
        out_shape=[jax.ShapeDtypeStruct((t, ZM_W), BF), jax.ShapeDtypeStruct((t, hw8), BF), jax.ShapeDtypeStruct((t, hw8), BF),
                   jax.ShapeDtypeStruct((t, hw8), BF), jax.ShapeDtypeStruct((1, HW), F32), jax.ShapeDtypeStruct((1, HW), F32),
                   jax.ShapeDtypeStruct((1, Q_RANK), F32), jax.ShapeDtypeStruct((1, KV_RANK), F32)],
        compiler_params=_params(("arbitrary",)),
    )(zm, dqa, dkta, dvta, dqb, dktb, dvtb, gq, gk, gqa, gkva, wq, wk, wv, ca, sa, cb, sb)


def _bwd_inproj(dzm, dzg, w1, w2, x, g1, dx1):
    t = x.shape[0]

    def body(dzm_ref, dzg_ref, w1_ref, w2_ref, x_ref, g_ref, dx1_ref, dx_ref, dg_ref):
        du = _dot_nt(dzm_ref[...], w1_ref[...]) + _dot_nt(dzg_ref[...], w2_ref[...])
        dx, dg = _norm_bwd(x_ref[...], g_ref[...], du)
        _accumulate(dg_ref, dg)
        dx_ref[...] = dx1_ref[...] + dx

    return pl.pallas_call(
        body, name="bwd_inproj", grid=(t // TM,),
        in_specs=[_rows(TM, ZM_W), _rows(TM, ZG_W), _whole((D, ZM_W)), _whole((D, ZG_W)), _rows(TM, D), _whole((1, D)),
                  _rows(TM, D)],
        out_specs=[_rows(TM, D), _whole((1, D))],
        out_shape=[jax.ShapeDtypeStruct((t, D), F32), jax.ShapeDtypeStruct((1, D), F32)],
        compiler_params=_params(("arbitrary",)),
    )(dzm, dzg, w1, w2, x, g1, dx1)


def _matmul_tn(a, b, tn, name):
    t, kdim = a.shape
    n = b.shape[1]
    tm = min(TN_ROWS_NARROW if kdim <= D else TN_ROWS_WIDE, t)
    nsteps = t // tm

    def body(a_ref, b_ref, o_ref, acc_ref):
        i = pl.program_id(1)

        @pl.when(i == 0)
        def _():
            acc_ref[...] = jnp.zeros_like(acc_ref)

        acc_ref[...] += _dot_tn(a_ref[...].astype(BF), b_ref[...])

        @pl.when(i == nsteps - 1)
        def _():
            o_ref[...] = acc_ref[...].astype(BF)

    return pl.pallas_call(
        body, name=name, grid=(n // tn, nsteps),
        in_specs=[pl.BlockSpec((tm, kdim), lambda j, i: (i, 0)), pl.BlockSpec((tm, tn), lambda j, i: (i, j))],
        out_specs=pl.BlockSpec((kdim, tn), lambda j, i: (0, j)),
        out_shape=jax.ShapeDtypeStruct((kdim, n), BF),
        scratch_shapes=[pltpu.VMEM((kdim, tn), F32)],
        compiler_params=_params(("parallel", "arbitrary")),
    )(a, b)


def _pair_halves(w, heads, q):
    lead = w.shape[:-1]
    return jnp.swapaxes(w.reshape(lead + (heads, 2, 2, q)), -3, -2).reshape(lead + (heads, 2, 2 * q))


def _unpair_halves(p, heads, q):
    lead = p.shape[:-3]
    return jnp.swapaxes(p.reshape(lead + (heads, 2, 2, q)), -3, -2).reshape(lead + (heads * 4 * q,))


def _halves_to_lanes(h, before=0, fill=0.0):
    pad = [(0, 0)] * (h.ndim - 1) + [(before, HW // 2 - before - h.shape[-1])]
    return jnp.pad(h, pad, constant_values=fill).reshape(h.shape[:-3] + (-1,))


def _lanes_to_halves(p, start, width):
    return p.reshape(p.shape[:-1] + (-1, 2, HW // 2))[..., start:start + width]


def _rot_to_lanes(w, heads, q, fill=0.0):
    return _halves_to_lanes(_pair_halves(w, heads, q), 0, fill)


def _rot_from_lanes(p, heads, q):
    return _unpair_halves(_lanes_to_halves(p, 0, 2 * q), heads, q)


def _plain_to_lanes(w, heads):
    w = w.reshape(w.shape[:-1] + (heads, HEAD))
    return jnp.pad(w, [(0, 0)] * (w.ndim - 1) + [(0, HW - HEAD)]).reshape(w.shape[:-2] + (-1,))


def _plain_from_lanes(p, heads):
    return p.reshape(p.shape[:-1] + (heads, HW))[..., :HEAD].reshape(p.shape[:-1] + (-1,))


def _rope_tables(t):
    pos = jnp.arange(t, dtype=jnp.int32)
    row = (pos // GRID_W).astype(F32)
    col = (pos % GRID_W).astype(F32)

    def table(rot_dim):
        half = rot_dim // 2
        inv = ROPE_THETA ** (-jnp.arange(0, half, 2, dtype=F32) / half)
        ar = row[:, None] * inv[None, :]
        ac = col[:, None] * inv[None, :]
        ang = jnp.concatenate([ar, ar, ac, ac], axis=-1)
        return jnp.cos(ang), jnp.sin(ang)

    c64, s64 = table(HEAD)
    c32, s32 = table(MLA_ROPE)
    return (_rot_to_lanes(c64, 1, 16, 1.0), _rot_to_lanes(s64, 1, 16), _rot_to_lanes(c32, 1, 8, 1.0),
            _rot_to_lanes(s32, 1, 8))


def _pad_heads_rows(w, heads, width):
    n = w.shape[1]
    w = w.reshape(heads, width, n)
    return jnp.pad(w, ((0, 0), (0, HW - width), (0, 0))).reshape(heads * HW, n)


def _unpad_heads_rows(w, heads, width):
    n = w.shape[1]
    return w.reshape(heads, HW, n)[:, :width, :].reshape(heads * width, n)


def _pad_first(w):
    w_in = w["w_in"]
    qa = _rot_to_lanes(w_in[:, 0:512], NH, 16)
    ka = _rot_to_lanes(w_in[:, 512:640], GQA_KV, 16)
    va = _plain_to_lanes(w_in[:, 640:768], GQA_KV)
    cq = w_in[:, 768:1152]
    ckv = w_in[:, 1152:1408]
    kr = _rot_to_lanes(w_in[:, 1408:1440], 1, 8)
    qu = w["w_q_up"].reshape(Q_RANK, NH, MLA_QK)
    q_halves = jnp.concatenate([_pair_halves(qu[:, :, MLA_NOPE:].reshape(Q_RANK, NH * MLA_ROPE), NH, 8),
                                qu[:, :, :MLA_NOPE].reshape(Q_RANK, NH, 2, MLA_NOPE // 2)], axis=-1)
    kvu = w["w_kv_up"].reshape(KV_RANK, NH, 2 * HEAD)
    return dict(
        w1=jnp.concatenate([qa, ka, va, cq, ckv, kr], axis=1),
        w2=w_in[:, 1440:],
        wq=_halves_to_lanes(q_halves),
        wk=_halves_to_lanes(kvu[:, :, :HEAD].reshape(KV_RANK, NH, 2, HEAD // 2), MLA_ROPE // 2),
        wv=_plain_to_lanes(kvu[:, :, HEAD:].reshape(KV_RANK, NH * HEAD), NH),
    )


def _pad_rest(w):
    return dict(
        wa=_pad_heads_rows(w["w_branch_a"], NH, HEAD),
        wb=_pad_heads_rows(w["w_branch_b"], NH, HEAD),
        wo=w["w_o"], wup=w["w_ffn_up"], wdn=w["w_ffn_down"],
    )


def _unpad_first(g):
    d1 = g["w1"]
    w_in = jnp.concatenate([
        _rot_from_lanes(d1[:, O_QA:O_KA], NH, 16), _rot_from_lanes(d1[:, O_KA:O_VA], GQA_KV, 16),
        _plain_from_lanes(d1[:, O_VA:O_CQ], GQA_KV), d1[:, O_CQ:O_CKV], d1[:, O_CKV:O_KR],
        _rot_from_lanes(d1[:, O_KR:O_KR + HW], 1, 8), g["w2"]], axis=1)
    dk = _lanes_to_halves(g["wk"], MLA_ROPE // 2, HEAD // 2).reshape(KV_RANK, NH, HEAD)
    dv = _plain_from_lanes(g["wv"], NH).reshape(KV_RANK, NH, HEAD)
    dq_rope = _unpair_halves(_lanes_to_halves(g["wq"], 0, MLA_ROPE // 2), NH, 8).reshape(Q_RANK, NH, MLA_ROPE)
    dq_nope = _lanes_to_halves(g["wq"], MLA_ROPE // 2, MLA_NOPE // 2).reshape(Q_RANK, NH, MLA_NOPE)
    return dict(
        w_in=w_in,
        w_q_up=jnp.concatenate([dq_nope, dq_rope], axis=-1).reshape(Q_RANK, NH * MLA_QK),
        w_kv_up=jnp.concatenate([dk, dv], axis=2).reshape(KV_RANK, NH * 2 * HEAD),
    )


def _unpad_rest(g):
    return dict(
        w_branch_a=_unpad_heads_rows(g["wa"], NH, HEAD),
        w_branch_b=_unpad_heads_rows(g["wb"], NH, HEAD),
        w_o=g["wo"], w_ffn_up=g["wup"], w_ffn_down=g["wdn"],
    )


SHARD_AXIS = dict(w_in=1, w_q_up=1, w_kv_up=1, w_branch_a=1, w_branch_b=1, w_o=0, w_ffn_up=1, w_ffn_down=0)


def _join(parts, name):
    if SHARD_AXIS[name] == 0:
        return parts.reshape(-1, parts.shape[-1])
    return jnp.transpose(parts, (1, 0, 2)).reshape(parts.shape[1], -1)


def _split(full, name):
    if SHARD_AXIS[name] == 0:
        return full.reshape(4, -1, full.shape[-1])
    return jnp.transpose(full.reshape(full.shape[0], 4, -1), (1, 0, 2))


def _local_step(x, target, shards, smalls):
    t = x.shape[0]
    nl = len(smalls)
    ca, sa, cb, sb = _rope_tables(t)

    def joined(names, got):
        return {n: _join(g, n) for n, g in zip(names, got)}

    w = _pad_first(joined(FIRST, _chip_exchange([shards[n][0] for n in FIRST], False, "gather_weights")))
    layers, saved = [], []
    for li, s in enumerate(smalls):
        gq, gk = _rot_to_lanes(s["q_norm_g"][None], 1, 16), _rot_to_lanes(s["k_norm_g"][None], 1, 16)
        gqa, gkva = s["q_a_norm_g"].reshape(1, -1), s["kv_a_norm_g"].reshape(1, -1)
        g1, g2, g3, g4 = (s[n].reshape(1, D) for n in ("pre_mix_g", "post_mix_g", "pre_ffn_g", "post_ffn_g"))
        b = s["b_gate"].reshape(1, ZG_W)
        u, zm, gate = _fwd_inproj(x, g1, w["w1"], w["w2"], b)
        qa, ka, va, qb, kb, vb, cqn, ckvn = _fwd_mid(zm, gq, gk, gqa, gkva, w["wq"], w["wk"], w["wv"], ca, sa, cb, sb)
        last = li + 1 == nl
        gather = _ChipExchange([shards[n][li] for n in REST] + ([] if last else [shards[n][li + 1] for n in FIRST]), False)
        ya, lse_a, got = _flash_fwd(qa, ka, va, GQA_G, "flash_fwd_gqa", gather)
        yb, lse_b, _ = _flash_fwd(qb, kb, vb, 1, "flash_fwd_mla")
        w = {**w, **_pad_rest(joined(REST, got[:len(REST)]))}
        pa, pb, mg, m, x1 = _fwd_merge(ya, yb, gate, w["wa"], w["wb"], w["wo"], x, g2)
        hn, a, r2 = _fwd_ffn_up(x1, g3, w["wup"])
        f, x2 = _fwd_ffn_down(a, w["wdn"], x1, g4)
        saved.append(dict(x=x, u=u, zm=zm, gate=gate, qa=qa, ka=ka, va=va, qb=qb, kb=kb, vb=vb, cqn=cqn, ckvn=ckvn,
                          ya=ya, lse_a=lse_a, yb=yb, lse_b=lse_b, pa=pa, pb=pb, mg=mg,
                          m=m, x1=x1, hn=hn, a=a, r2=r2, f=f, gq=gq, gk=gk, gqa=gqa, gkva=gkva, g1=g1, g2=g2, g3=g3,
                          g4=g4))
        layers.append(w)
        x = x2
        if not last:
            w = _pad_first(joined(FIRST, got[len(REST):]))

    dx, loss8 = _loss_head(x, target)
    loss = loss8[0, 0]

    received, small_grads, send_first = [dict() for _ in range(nl)], [None] * nl, None
    for li in range(nl - 1, -1, -1):
        w, r = layers[li], saved[li]
        df, dh, dg4 = _bwd_ffn_down(r["f"], dx, r["g4"], w["wdn"], r["r2"])
        dx1, dg3 = _bwd_ffn_up(dh, w["wup"], r["x1"], r["g3"], dx)
        dm, dpa, dpb, dzg, dya, dyb, dg2, db = _bwd_merge(r["m"], dx1, r["g2"], w["wo"], r["gate"], r["pa"], r["pb"],
                                                         w["wa"], w["wb"])
        nat = _unpad_rest(dict(
            wa=_matmul_tn(r["ya"], dpa, 512, "dw_branch_a"), wb=_matmul_tn(r["yb"], dpb, 512, "dw_branch_b"),
            wo=_matmul_tn(r["mg"], dm, 512, "dw_o"), wup=_matmul_tn(r["hn"], dh, 512, "dw_ffn_up"),
            wdn=_matmul_tn(r["a"], df, 512, "dw_ffn_down")))
        scatter = _ChipExchange([_split(nat[n], n) for n in REST] + (send_first or []), True)
        dqa, dkta, dvta, got = _flash_bwd(r["qa"], dya, r["ya"], r["lse_a"], r["ka"], r["va"], GQA_G, SCALE_GQA,
                                          "flash_bwd_gqa", scatter)
        received[li].update(zip(REST, got[:len(REST)]))
        if send_first:
            received[li + 1].update(zip(FIRST, got[len(REST):]))
        dqb, dktb, dvtb, _ = _flash_bwd(r["qb"], dyb, r["yb"], r["lse_b"], r["kb"], r["vb"], 1, SCALE_MLA, "flash_bwd_mla")
        dzm, dqbp, dkb16, dvb16, dgq, dgk, dgqa, dgkva = _bwd_mid(
            r["zm"], dqa, dkta, dvta, dqb, dktb, dvtb, r["gq"], r["gk"], r["gqa"], r["gkva"], w["wq"], w["wk"], w["wv"],
            ca, sa, cb, sb)
        dx, dg1 = _bwd_inproj(dzm, dzg, w["w1"], w["w2"], r["x"], r["g1"], dx1)
        nat = _unpad_first(dict(
            w1=_matmul_tn(r["u"], dzm, 768, "dw_in_main"), w2=_matmul_tn(r["u"], dzg, 512, "dw_in_gate"),
            wq=_matmul_tn(r["cqn"], dqbp, 512, "dw_q_up"), wk=_matmul_tn(r["ckvn"], dkb16, 512, "dw_k_up"),
            wv=_matmul_tn(r["ckvn"], dvb16, 512, "dw_v_up")))
        send_first = [_split(nat[n], n) for n in FIRST]
        small_grads[li] = dict(b_gate=db[0], q_norm_g=_rot_from_lanes(dgq, 1, 16)[0], k_norm_g=_rot_from_lanes(dgk, 1, 16)[0], q_a_norm_g=dgqa[0],
                               kv_a_norm_g=dgkva[0], pre_mix_g=dg1[0], post_mix_g=dg2[0], pre_ffn_g=dg3[0],
                               post_ffn_g=dg4[0])
    received[0].update(zip(FIRST, _chip_exchange(send_first, True, "scatter_grads")))
    return loss, dx, received, small_grads


def _chip_exchange(srcs, per_dest, name):
    ex = _ChipExchange(srcs, per_dest)

    def body(*refs):
        _, _, ex_refs = _hosted(ex, refs, 0, 0)
        ex.start(*ex_refs)
        ex.finish(*ex_refs)

    return pl.pallas_call(
        body, name=name, in_specs=ex.specs, out_specs=ex.specs, out_shape=ex.out_shape, scratch_shapes=ex.scratch,
    )(*ex.srcs)


def _sibling_exchange(srcs, name):
    n = len(srcs)

    def body(*refs):
        src_refs, out_refs = refs[:n], refs[n:2 * n]
        send_sems, recv_sems = refs[2 * n:]
        x, y, c = lax.axis_index("x"), lax.axis_index("y"), lax.axis_index("c")
        cps = [pltpu.make_async_remote_copy(src_ref=src_refs[a], dst_ref=out_refs[a], send_sem=send_sems.at[a],
                                            recv_sem=recv_sems.at[a], device_id=(x, y, 1 - c), device_id_type=MESH)
               for a in range(n)]
        for cp in cps:
            cp.start()
        for cp in cps:
            cp.wait()

    return pl.pallas_call(
        body, name=name,
        in_specs=[pl.BlockSpec(memory_space=pl.ANY)] * n,
        out_specs=[pl.BlockSpec(memory_space=pl.ANY)] * n,
        out_shape=[jax.ShapeDtypeStruct(s.shape, s.dtype) for s in srcs],
        scratch_shapes=[pltpu.SemaphoreType.DMA((n,)), pltpu.SemaphoreType.DMA((n,))],
    )(*srcs)


def _allgather_small(v):
    m_per, n = v.shape

    def body(x_ref, out_ref, send_sems, recv_sems, local_sem):
        x, y, c = lax.axis_index("x"), lax.axis_index("y"), lax.axis_index("c")
        me, sibling = (x, y, c), (x, y, 1 - c)
        chips = [(1 - x, y), (x, 1 - y), (1 - x, 1 - y)]

        def rows(px, py, pc):
            return out_ref.at[pl.ds((4 * px + 2 * py + pc) * m_per, m_per), :]

        def copy(k, block, to, src=None):
            return pltpu.make_async_remote_copy(
                src_ref=rows(*block) if src is None else src, dst_ref=rows(*block),
                send_sem=send_sems.at[k], recv_sem=recv_sems.at[k], device_id=to, device_id_type=MESH)

        mine = pltpu.make_async_copy(x_ref, rows(*me), local_sem)
        mine.start()
        first = [copy(0, me, sibling, src=x_ref)]
        first += [copy(1 + j, me, (*chip, c), src=x_ref) for j, chip in enumerate(chips)]
        for cp in first:
            cp.start()
        passed = [copy(4 + j, (*chip, c), sibling) for j, chip in enumerate(chips)]
        for j, chip in enumerate(chips):
            copy(1 + j, (*chip, c), me).wait_recv()
            passed[j].start()
        copy(0, sibling, me).wait_recv()
        for j, chip in enumerate(chips):
            copy(4 + j, (*chip, 1 - c), me).wait_recv()
        for cp in first + passed:
            cp.wait_send()
        mine.wait()

    return pl.pallas_call(
        body, name="allgather_small",
        out_shape=jax.ShapeDtypeStruct((8 * m_per, n), v.dtype),
        in_specs=[pl.BlockSpec(memory_space=pltpu.VMEM)],
        out_specs=pl.BlockSpec(memory_space=pltpu.VMEM),
        scratch_shapes=[pltpu.SemaphoreType.DMA((7,)), pltpu.SemaphoreType.DMA((7,)), pltpu.SemaphoreType.DMA],
    )(v)


def _pick_rows(r):
    return next(t for t in (512, 256, 128, 64, 32, 16, 8, r) if r % t == 0)


def _sum_slabs(a, name):
    s, r, c = a.shape
    tm = _pick_rows(r)

    def body(a_ref, o_ref):
        acc = a_ref[0].astype(F32)
        for k in range(1, s):
            acc = acc + a_ref[k].astype(F32)
        o_ref[...] = acc

    return pl.pallas_call(
        body, name=name, grid=(r // tm,),
        in_specs=[pl.BlockSpec((s, tm, c), lambda i: (0, i, 0))],
        out_specs=_rows(tm, c),
        out_shape=jax.ShapeDtypeStruct((r, c), F32),
        compiler_params=_params(("parallel",)),
    )(a)


def _adamw(w, ga, gb, m, v, name):
    r, c = w.shape
    tm = min(256, _pick_rows(r))

    def body(w_ref, ga_ref, gb_ref, m_ref, v_ref, g_ref, d_ref, nm_ref, nv_ref):
        gv = ga_ref[...] + gb_ref[...]
        g_ref[...] = gv
        nm = ADAM_B1 * m_ref[...] + (1.0 - ADAM_B1) * gv
        nv = ADAM_B2 * v_ref[...] + (1.0 - ADAM_B2) * (gv * gv)
        m_hat = nm / (1.0 - ADAM_B1 ** ADAM_STEP)
        v_hat = nv / (1.0 - ADAM_B2 ** ADAM_STEP)
        d_ref[...] = -ADAM_LR * (m_hat / (jnp.sqrt(v_hat) + ADAM_EPS) + ADAM_WD * w_ref[...])
        nm_ref[...] = nm
        nv_ref[...] = nv

    spec = _rows(tm, c)
    return pl.pallas_call(
        body, name=name, grid=(r // tm,),
        in_specs=[spec] * 5, out_specs=[spec] * 4,
        out_shape=[jax.ShapeDtypeStruct((r, c), F32)] * 4,
        compiler_params=_params(("parallel",)),
    )(w, ga, gb, m, v)


def _adamw_small(w, gparts, m, v):
    mrows, n = w.shape

    def body(w_ref, g_ref, m_ref, v_ref, go_ref, d_ref, nm_ref, nv_ref):
        gv = g_ref[0]
        for k in range(1, 8):
            gv = gv + g_ref[k]
        go_ref[...] = gv
        nm = ADAM_B1 * m_ref[...] + (1.0 - ADAM_B1) * gv
        nv = ADAM_B2 * v_ref[...] + (1.0 - ADAM_B2) * (gv * gv)
        m_hat = nm / (1.0 - ADAM_B1 ** ADAM_STEP)
        v_hat = nv / (1.0 - ADAM_B2 ** ADAM_STEP)
        d_ref[...] = -ADAM_LR * (m_hat / (jnp.sqrt(v_hat) + ADAM_EPS) + ADAM_WD * w_ref[...])
        nm_ref[...] = nm
        nv_ref[...] = nv

    return pl.pallas_call(
        body, name="adamw_small",
        out_shape=[jax.ShapeDtypeStruct((mrows, n), F32)] * 4,
    )(w, gparts, m, v)


def _flat(parts):
    nl = parts[0].shape[0]
    return jnp.concatenate([p.reshape(nl, -1) for p in parts], axis=1)


def _unflat(flat, shapes):
    out, off = [], 0
    nl = flat.shape[0]
    for s in shapes:
        n = math.prod(s)
        out.append(flat[:, off:off + n].reshape((nl,) + tuple(s)))
        off += n
    return out


def kernel(x, w_in, b_gate, q_norm_g, k_norm_g, q_a_norm_g, kv_a_norm_g, w_q_up, w_kv_up, w_branch_a, w_branch_b, w_o, w_ffn_up, w_ffn_down, pre_mix_g, post_mix_g, pre_ffn_g, post_ffn_g, loss_target, m_w_in, m_b_gate, m_q_norm_g, m_k_norm_g, m_q_a_norm_g, m_kv_a_norm_g, m_w_q_up, m_w_kv_up, m_w_branch_a, m_w_branch_b, m_w_o, m_w_ffn_up, m_w_ffn_down, m_pre_mix_g, m_post_mix_g, m_pre_ffn_g, m_post_ffn_g, v_w_in, v_b_gate, v_q_norm_g, v_k_norm_g, v_q_a_norm_g, v_kv_a_norm_g, v_w_q_up, v_w_kv_up, v_w_branch_a, v_w_branch_b, v_w_o, v_w_ffn_up, v_w_ffn_down, v_pre_mix_g, v_post_mix_g, v_pre_ffn_g, v_post_ffn_g):
    wts = dict(w_in=w_in, b_gate=b_gate, q_norm_g=q_norm_g, k_norm_g=k_norm_g, q_a_norm_g=q_a_norm_g,
               kv_a_norm_g=kv_a_norm_g, w_q_up=w_q_up, w_kv_up=w_kv_up, w_branch_a=w_branch_a, w_branch_b=w_branch_b,
               w_o=w_o, w_ffn_up=w_ffn_up, w_ffn_down=w_ffn_down, pre_mix_g=pre_mix_g, post_mix_g=post_mix_g,
               pre_ffn_g=pre_ffn_g, post_ffn_g=post_ffn_g)
    mom = dict(w_in=m_w_in, b_gate=m_b_gate, q_norm_g=m_q_norm_g, k_norm_g=m_k_norm_g, q_a_norm_g=m_q_a_norm_g,
               kv_a_norm_g=m_kv_a_norm_g, w_q_up=m_w_q_up, w_kv_up=m_w_kv_up, w_branch_a=m_w_branch_a,
               w_branch_b=m_w_branch_b, w_o=m_w_o, w_ffn_up=m_w_ffn_up, w_ffn_down=m_w_ffn_down, pre_mix_g=m_pre_mix_g,
               post_mix_g=m_post_mix_g, pre_ffn_g=m_pre_ffn_g, post_ffn_g=m_post_ffn_g)
    var = dict(w_in=v_w_in, b_gate=v_b_gate, q_norm_g=v_q_norm_g, k_norm_g=v_k_norm_g, q_a_norm_g=v_q_a_norm_g,
               kv_a_norm_g=v_kv_a_norm_g, w_q_up=v_w_q_up, w_kv_up=v_w_kv_up, w_branch_a=v_w_branch_a,
               w_branch_b=v_w_branch_b, w_o=v_w_o, w_ffn_up=v_w_ffn_up, w_ffn_down=v_w_ffn_down, pre_mix_g=v_pre_mix_g,
               post_mix_g=v_post_mix_g, pre_ffn_g=v_pre_ffn_g, post_ffn_g=v_post_ffn_g)
    nl = w_in.shape[0]

    shards = {n: wts[n].astype(BF) for n in BIG}
    smalls = [{n: wts[n][li] for n in SMALL} for li in range(nl)]
    loss_local, dx, received, small_grads = _local_step(x[0], loss_target[0], shards, smalls)
    loss = lax.psum(loss_local, ("x", "y", "c"))
    part = []
    for i, n in enumerate(BIG):
        got = jnp.stack([received[li][n] for li in range(nl)], axis=1)
        part.append(_sum_slabs(got.reshape(4, -1, got.shape[-1]), "sum_chips_" + n))
    other = _sibling_exchange(part, "swap_cores")

    small_shapes = [wts[n].shape[1:] for n in SMALL]
    g_loc = _flat([jnp.stack([small_grads[li][n] for li in range(nl)]) for n in SMALL]).reshape(-1, 128)
    g_all = _allgather_small(g_loc).reshape(8, -1, 128)
    pack = lambda d: _flat([d[n] for n in SMALL]).reshape(-1, 128)
    gs, ds, ms, vs = _adamw_small(pack(wts), g_all, pack(mom), pack(var))
    unpack = lambda a: dict(zip(SMALL, _unflat(a.reshape(nl, -1), small_shapes)))
    g_small, d_small, m_small, v_small = unpack(gs), unpack(ds), unpack(ms), unpack(vs)

    out_g, out_d, out_m, out_v = dict(g_small), dict(d_small), dict(m_small), dict(v_small)
    for i, n in enumerate(BIG):
        shp = wts[n].shape
        two = lambda a: a.reshape(-1, shp[-1])
        g, d, nm, nv = _adamw(two(wts[n]), part[i], other[i], two(mom[n]), two(var[n]), "adamw_" + n)
        out_g[n], out_d[n], out_m[n], out_v[n] = g.reshape(shp), d.reshape(shp), nm.reshape(shp), nv.reshape(shp)

    return (loss, dx[None], *[out_g[n] for n in ORDER], *[out_d[n] for n in ORDER], *[out_m[n] for n in ORDER],
            *[out_v[n] for n in ORDER])
```
